```python
import math
import jax
import jax.numpy as jnp
from jax import lax
import numpy as np

D_MODEL = 2048
BATCH = 1
SEQ = 8192
DEPTH = 4

N_BRANCHES = 4
BRANCH_WIDTH = D_MODEL // 4
D_FF = 4 * D_MODEL
NORM_EPS = 1e-6

S5_GROUP = 16
S5_GROUPS = BRANCH_WIDTH // S5_GROUP
S5_STATE = 64
S5_DT_MIN = 1e-3
S5_DT_MAX = 1e-1

GDN_HEAD_DIM = 128
GDN_HEADS = BRANCH_WIDTH // GDN_HEAD_DIM
GDN_CONV = 4
GDN_CHUNK = 64

SWA_HEAD_DIM = 64
SWA_HEADS = BRANCH_WIDTH // SWA_HEAD_DIM
SWA_KV_HEADS = SWA_HEADS // 4
WINDOW = 128
SWA_BLOCK = 128
T5_BUCKETS = 32
T5_MAX_DISTANCE = 128

MLA_HEADS = BRANCH_WIDTH // 128
MLA_Q_RANK = 384
MLA_KV_RANK = 512
MLA_NOPE = 128
MLA_ROPE = 64
MLA_V = 128
ROPE_THETA = 10000.0
ATTN_BLOCK = 128

IN_SPLITS = (
    BRANCH_WIDTH,
    3 * BRANCH_WIDTH,
    BRANCH_WIDTH,
    2 * GDN_HEADS,
    2 * GDN_HEADS,
    SWA_HEADS * SWA_HEAD_DIM,
    2 * SWA_KV_HEADS * SWA_HEAD_DIM,
    MLA_Q_RANK,
    MLA_KV_RANK,
    MLA_ROPE,
    N_BRANCHES * D_MODEL,
)
D_IN = sum(IN_SPLITS)

kernel_name = 'hybrid_parallel_gated_encoder'


def rms_norm(x, gain):
    xf = x.astype(jnp.float32)
    y = xf * lax.rsqrt(jnp.mean(xf * xf, axis=-1, keepdims=True) + NORM_EPS)
    return (y * gain.astype(jnp.float32)).astype(x.dtype)


def _cmul(ar, ai, br, bi):
    return ar * br - ai * bi, ar * bi + ai * br


def _linear_recurrence_op(left, right):
    a1r, a1i, b1r, b1i = left
    a2r, a2i, b2r, b2i = right
    ar, ai = _cmul(a2r, a2i, a1r, a1i)
    br, bi = _cmul(a2r, a2i, b1r, b1i)
    return ar, ai, br + b2r, bi + b2i


def s5_direction(u, lam_re, lam_im, log_step, b_re, b_im, c_re, c_im, reverse):
    f = lambda t: t.astype(jnp.float32)
    lam_re = jnp.minimum(f(lam_re), -1e-4)
    lam_im = f(lam_im)
    dt = jnp.exp(f(log_step))[:, None]
    mag = jnp.exp(lam_re * dt)
    abar_r = mag * jnp.cos(lam_im * dt)
    abar_i = mag * jnp.sin(lam_im * dt)
    den = lam_re * lam_re + lam_im * lam_im
    xr = abar_r - 1.0
    xi = abar_i
    coef_r = (xr * lam_re + xi * lam_im) / den
    coef_i = (xi * lam_re - xr * lam_im) / den
    b_re, b_im = f(b_re), f(b_im)
    bbar_r = coef_r[..., None] * b_re - coef_i[..., None] * b_im
    bbar_i = coef_r[..., None] * b_im + coef_i[..., None] * b_re
    bu_r = jnp.einsum('blgh,gph->blgp', u, bbar_r)
    bu_i = jnp.einsum('blgh,gph->blgp', u, bbar_i)
    a_r = jnp.broadcast_to(abar_r, bu_r.shape)
    a_i = jnp.broadcast_to(abar_i, bu_i.shape)
    _, _, s_r, s_i = lax.associative_scan(
        _linear_recurrence_op, (a_r, a_i, bu_r, bu_i), reverse=reverse, axis=1)
    return (jnp.einsum('blgp,ghp->blgh', s_r, f(c_re))
            - jnp.einsum('blgp,ghp->blgh', s_i, f(c_im)))


def s5_mixer(u, lam_re, lam_im, log_step, b_re, b_im, c_re, c_im, d_skip, w_glu, b_glu):
    B_, L, W = u.shape
    uf = u.astype(jnp.float32)
    ug = uf.reshape(B_, L, S5_GROUPS, S5_GROUP)
    y = s5_direction(ug, lam_re[0], lam_im[0], log_step[0], b_re[0], b_im[0],
                     c_re[0], c_im[0], reverse=False)
    y = y + s5_direction(ug, lam_re[1], lam_im[1], log_step[1], b_re[1], b_im[1],
                         c_re[1], c_im[1], reverse=True)
    y = y.reshape(B_, L, W) + d_skip.astype(jnp.float32) * uf
    y = jax.nn.gelu(y)
    y = y * jax.nn.sigmoid(y @ w_glu.astype(jnp.float32) + b_glu.astype(jnp.float32))
    return y.astype(u.dtype)


def l2_normalize(x):
    xf = x.astype(jnp.float32)
    return xf * lax.rsqrt(jnp.sum(xf * xf, axis=-1, keepdims=True) + 1e-6)


def depthwise_conv_centred(x, w):
    K = w.shape[0]
    return lax.conv_general_dilated(
        x, w.astype(x.dtype)[:, None, :], window_strides=(1,),
        padding=[(K // 2, K - 1 - K // 2)],
        dimension_numbers=('NWC', 'WIO', 'NWC'),
        feature_group_count=x.shape[-1])


def gated_delta_rule_chunked(q, k, v, g, beta):
    B_, L, H, dk = q.shape
    dv = v.shape[-1]
    C = GDN_CHUNK
    N = L // C

    def chunks(t):
        return t.reshape(B_, N, C, H, -1).transpose(0, 3, 1, 2, 4)

    q, k, v = chunks(q), chunks(k), chunks(v)
    g = jnp.cumsum(g.reshape(B_, N, C, H).transpose(0, 3, 1, 2), axis=-1)
    beta = beta.reshape(B_, N, C, H).transpose(0, 3, 1, 2)[..., None]
    k_beta = k * beta
    lower = jnp.tril(jnp.ones((C, C), dtype=bool))
    strict = jnp.tril(jnp.ones((C, C), dtype=bool), -1)
    decay = jnp.exp(jnp.where(lower, g[..., :, None] - g[..., None, :], -jnp.inf))
    a_mat = jnp.where(strict, jnp.einsum('bhnik,bhnjk->bhnij', k_beta, k) * decay, 0.0)
    a_mat = a_mat + jnp.eye(C, dtype=q.dtype)
    rhs = jnp.concatenate([v * beta, k_beta * jnp.exp(g)[..., None]], axis=-1)
    sol = lax.linalg.triangular_solve(a_mat, rhs, left_side=True, lower=True,
                                      unit_diagonal=True)
    u, w = sol[..., :dv], sol[..., dv:]
    attn = jnp.einsum('bhnik,bhnjk->bhnij', q, k) * decay

    def step(S, inp):
        q_c, k_c, u_c, w_c, g_c, a_c = inp
        v_new = u_c - jnp.einsum('bhck,bhkv->bhcv', w_c, S)
        o = (jnp.einsum('bhck,bhkv->bhcv', q_c * jnp.exp(g_c)[..., None], S)
             + jnp.einsum('bhcj,bhjv->bhcv', a_c, v_new))
        g_last = g_c[..., -1]
        S = (S * jnp.exp(g_last)[..., None, None]
             + jnp.einsum('bhck,bhcv->bhkv', k_c * jnp.exp(g_last[..., None] - g_c)[..., None], v_new))
        return S, o

    s0 = jnp.zeros((B_, H, dk, dv), q.dtype)
    xs = tuple(jnp.moveaxis(t, 2, 0) for t in (q, k, u, w, g, attn))
    _, o = lax.scan(step, s0, xs)
    return jnp.moveaxis(o, 0, 2).transpose(0, 2, 3, 1, 4).reshape(B_, L, H, dv)


def gdn_mixer(qkv, z, beta_logits, decay_logits, conv_w, a_log, dt_bias, o_gain):
    B_, L, _ = qkv.shape
    H, Dh = GDN_HEADS, GDN_HEAD_DIM
    qkv = jax.nn.silu(depthwise_conv_centred(qkv, conv_w))
    q, k, v = jnp.split(qkv, 3, axis=-1)
    q = l2_normalize(q.reshape(B_, L, H, Dh)) * (Dh ** -0.5)
    k = l2_normalize(k.reshape(B_, L, H, Dh))
    v = v.reshape(B_, L, H, Dh).astype(jnp.float32)
    beta = jax.nn.sigmoid(beta_logits.astype(jnp.float32)).reshape(B_, L, 2, H)
    g = -jnp.exp(a_log.astype(jnp.float32)) * jax.nn.softplus(
        decay_logits.astype(jnp.float32).reshape(B_, L, 2, H) + dt_bias.astype(jnp.float32))
    o_fwd = gated_delta_rule_chunked(q, k, v, g[:, :, 0], beta[:, :, 0])
    flip = lambda t: jnp.flip(t, axis=1)
    o_bwd = flip(gated_delta_rule_chunked(flip(q), flip(k), flip(v),
                                          flip(g[:, :, 1]), flip(beta[:, :, 1])))
    o = rms_norm(o_fwd + o_bwd, o_gain)
    o = o * jax.nn.silu(z.astype(jnp.float32).reshape(B_, L, H, Dh))
    return o.reshape(B_, L, H * Dh).astype(z.dtype)


def t5_bucket(rel):
    nb = T5_BUCKETS // 2
    max_exact = nb // 2
    ret = jnp.where(rel > 0, nb, 0)
    n = jnp.abs(rel)
    nf = jnp.maximum(n, 1).astype(jnp.float32)
    large = max_exact + (jnp.log(nf / max_exact) / math.log(T5_MAX_DISTANCE / max_exact)
                         * (nb - max_exact)).astype(jnp.int32)
    large = jnp.minimum(large, nb - 1)
    return ret + jnp.where(n < max_exact, n, large)


def swa_mixer(q, kv, sink, t5_bias):
    B_, L, _ = q.shape
    NB = L // SWA_BLOCK
    G = SWA_HEADS // SWA_KV_HEADS
    q = q.reshape(B_, NB, SWA_BLOCK, SWA_KV_HEADS, G, SWA_HEAD_DIM)
    k, v = jnp.split(kv, 2, axis=-1)

    def band(t):
        t = jnp.pad(t, ((0, 0), (SWA_BLOCK, SWA_BLOCK), (0, 0)))
        t = t.reshape(B_, NB + 2, SWA_BLOCK, SWA_KV_HEADS, SWA_HEAD_DIM)
        return jnp.concatenate([t[:, :-2], t[:, 1:-1], t[:, 2:]], axis=2)

    kb, vb = band(k), band(v)
    qi = jnp.arange(SWA_BLOCK)[:, None]
    sj = jnp.arange(3 * SWA_BLOCK)[None, :]
    rel = sj - SWA_BLOCK - qi
    bias = t5_bias[t5_bucket(rel)].astype(jnp.float32)
    bias = bias.transpose(2, 0, 1).reshape(SWA_KV_HEADS, G, SWA_BLOCK, 3 * SWA_BLOCK)
    kpos = jnp.arange(NB)[:, None] * SWA_BLOCK + sj - SWA_BLOCK
    valid = (jnp.abs(rel) <= WINDOW)[None] & ((kpos >= 0) & (kpos < L))[:, None, :]
    logits = (jnp.einsum('bnqkgd,bnskd->bnkgqs', q, kb).astype(jnp.float32)
              * (SWA_HEAD_DIM ** -0.5) + bias)
    logits = jnp.where(valid[None, :, None, None], logits, -1e30)
    sink_col = jnp.broadcast_to(
        sink.astype(jnp.float32).reshape(SWA_KV_HEADS, G, 1, 1), logits.shape[:-1] + (1,))
    probs = jax.nn.softmax(jnp.concatenate([logits, sink_col], axis=-1), axis=-1)[..., :-1]
    out = jnp.einsum('bnkgqs,bnskd->bnqkgd', probs.astype(vb.dtype), vb)
    return out.reshape(B_, L, SWA_HEADS * SWA_HEAD_DIM)


def apply_rope(x, cos, sin):
    x1, x2 = jnp.split(x, 2, axis=-1)
    return jnp.concatenate([x1 * cos - x2 * sin, x2 * cos + x1 * sin], axis=-1)


def mla_mixer(c_q, c_kv, k_rope, q_gain, kv_gain, w_uq, w_ukv):
    B_, L, _ = c_q.shape
    H = MLA_HEADS
    q = (rms_norm(c_q, q_gain) @ w_uq).reshape(B_, L, H, MLA_NOPE + MLA_ROPE)
    kv = (rms_norm(c_kv, kv_gain) @ w_ukv).reshape(B_, L, H, MLA_NOPE + MLA_V)
    q_nope, q_pe = q[..., :MLA_NOPE], q[..., MLA_NOPE:]
    k_nope, v = kv[..., :MLA_NOPE], kv[..., MLA_NOPE:]
    pos = jnp.arange(L, dtype=jnp.float32)
    inv_freq = ROPE_THETA ** (-jnp.arange(0, MLA_ROPE, 2, dtype=jnp.float32) / MLA_ROPE)
    ang = pos[:, None] * inv_freq[None, :]
    cos, sin = jnp.cos(ang)[:, None, :], jnp.sin(ang)[:, None, :]
    q_pe = apply_rope(q_pe.astype(jnp.float32), cos, sin)
    k_pe = apply_rope(k_rope.astype(jnp.float32)[:, :, None, :], cos, sin)
    qf = jnp.concatenate([q_nope.astype(jnp.float32), q_pe], axis=-1).astype(c_q.dtype)
    kf = jnp.concatenate([k_nope.astype(jnp.float32),
                          jnp.broadcast_to(k_pe, (B_, L, H, MLA_ROPE))], axis=-1).astype(c_q.dtype)
    scale = (MLA_NOPE + MLA_ROPE) ** -0.5
    NB = L // ATTN_BLOCK
    qb = qf.reshape(B_, NB, ATTN_BLOCK, H, MLA_NOPE + MLA_ROPE).transpose(1, 0, 2, 3, 4)

    def attend(qblk):
        s = jnp.einsum('bqhd,bshd->bhqs', qblk, kf).astype(jnp.float32) * scale
        p = jax.nn.softmax(s, axis=-1)
        return jnp.einsum('bhqs,bshd->bqhd', p.astype(v.dtype), v)

    out = lax.map(attend, qb)
    return out.transpose(1, 0, 2, 3, 4).reshape(B_, L, H * MLA_V)


def setup_inputs(seed: int = 0) -> dict:
    key = jax.random.key(seed)
    ks = jax.random.split(key, 28)
    f32 = jnp.float32

    def normal(k, shape, scale):
        return jax.random.normal(k, shape, f32) * scale

    G, P, Hg = S5_GROUPS, S5_STATE, S5_GROUP
    W = BRANCH_WIDTH
    gk = jax.random.split(ks[24], 4)
    gdn_dt = jnp.exp(jax.random.uniform(ks[14], (DEPTH, 2, GDN_HEADS), f32,
                                        math.log(1e-3), math.log(1e-1)))
    return {
        'x': jax.random.normal(ks[0], (BATCH, SEQ, D_MODEL), f32),
        'w_in': normal(ks[1], (DEPTH, D_MODEL, D_IN), D_MODEL ** -0.5),
        's5_lam_re': -0.5 + normal(ks[2], (DEPTH, 2, G, P), 0.01),
        's5_lam_im': math.pi * jnp.arange(P, dtype=f32) + normal(ks[3], (DEPTH, 2, G, P), 0.01),
        's5_log_step': jax.random.uniform(ks[4], (DEPTH, 2, G), f32,
                                          math.log(S5_DT_MIN), math.log(S5_DT_MAX)),
        's5_b_re': normal(ks[5], (DEPTH, 2, G, P, Hg), (2 * Hg) ** -0.5),
        's5_b_im': normal(ks[6], (DEPTH, 2, G, P, Hg), (2 * Hg) ** -0.5),
        's5_c_re': normal(ks[7], (DEPTH, 2, G, Hg, P), P ** -0.5),
        's5_c_im': normal(ks[8], (DEPTH, 2, G, Hg, P), P ** -0.5),
        's5_d': normal(ks[9], (DEPTH, W), 1.0),
        's5_w_glu': normal(ks[10], (DEPTH, W, W), W ** -0.5),
        's5_b_glu': normal(ks[11], (DEPTH, W), 0.01),
        'gdn_conv': normal(ks[12], (DEPTH, GDN_CONV, 3 * W), GDN_CONV ** -0.5),
        'gdn_a_log': jnp.log(jax.random.uniform(ks[13], (DEPTH, 2, GDN_HEADS), f32, 1.0, 16.0)),
        'gdn_dt_bias': gdn_dt + jnp.log(-jnp.expm1(-gdn_dt)),
        'gdn_o_gain': 1.0 + normal(ks[15], (DEPTH, GDN_HEAD_DIM), 0.02),
        'swa_sink': normal(ks[16], (DEPTH, SWA_HEADS), 0.5),
        't5_bias': normal(ks[17], (T5_BUCKETS, SWA_HEADS), 0.5),
        'mla_q_gain': 1.0 + normal(ks[18], (DEPTH, MLA_Q_RANK), 0.02),
        'mla_kv_gain': 1.0 + normal(ks[19], (DEPTH, MLA_KV_RANK), 0.02),
        'mla_w_uq': normal(ks[20], (DEPTH, MLA_Q_RANK, MLA_HEADS * (MLA_NOPE + MLA_ROPE)),
                           MLA_Q_RANK ** -0.5),
        'mla_w_ukv': normal(ks[21], (DEPTH, MLA_KV_RANK, MLA_HEADS * (MLA_NOPE + MLA_V)),
                            MLA_KV_RANK ** -0.5),
        'w_branch': normal(ks[22], (DEPTH, N_BRANCHES, W, D_MODEL), W ** -0.5),
        'w_out': normal(ks[23], (DEPTH, D_MODEL, D_MODEL), D_MODEL ** -0.5),
        'mix_pre_gain': 1.0 + normal(gk[0], (DEPTH, D_MODEL), 0.02),
        'mix_post_gain': 1.0 + normal(gk[1], (DEPTH, D_MODEL), 0.02),
        'mlp_pre_gain': 1.0 + normal(gk[2], (DEPTH, D_MODEL), 0.02),
        'mlp_post_gain': 1.0 + normal(gk[3], (DEPTH, D_MODEL), 0.02),
        'w_mlp_in': normal(ks[25], (DEPTH, D_MODEL, D_FF), D_MODEL ** -0.5),
        'w_mlp_out': normal(ks[26], (DEPTH, D_FF, D_MODEL), D_FF ** -0.5),
    }


def reference(x, w_in, s5_lam_re, s5_lam_im, s5_log_step, s5_b_re, s5_b_im, s5_c_re,
              s5_c_im, s5_d, s5_w_glu, s5_b_glu, gdn_conv, gdn_a_log, gdn_dt_bias,
              gdn_o_gain, swa_sink, t5_bias, mla_q_gain, mla_kv_gain, mla_w_uq, mla_w_ukv,
              w_branch, w_out, mix_pre_gain, mix_post_gain, mlp_pre_gain, mlp_post_gain,
              w_mlp_in, w_mlp_out):
    B_, L, _ = x.shape
    split_points = np.cumsum(IN_SPLITS)[:-1].tolist()
    for l in range(DEPTH):
        h = rms_norm(x, mix_pre_gain[l])
        (s5_u, gdn_qkv, gdn_z, gdn_beta, gdn_decay, swa_q, swa_kv,
         mla_cq, mla_ckv, mla_kr, gate_logits) = jnp.split(h @ w_in[l], split_points, axis=-1)
        y_a = s5_mixer(s5_u, s5_lam_re[l], s5_lam_im[l], s5_log_step[l], s5_b_re[l],
                       s5_b_im[l], s5_c_re[l], s5_c_im[l], s5_d[l], s5_w_glu[l], s5_b_glu[l])
        y_b = gdn_mixer(gdn_qkv, gdn_z, gdn_beta, gdn_decay, gdn_conv[l], gdn_a_log[l],
                        gdn_dt_bias[l], gdn_o_gain[l])
        y_c = swa_mixer(swa_q, swa_kv, swa_sink[l], t5_bias)
        y_d = mla_mixer(mla_cq, mla_ckv, mla_kr, mla_q_gain[l], mla_kv_gain[l],
                        mla_w_uq[l], mla_w_ukv[l])
        ys = jnp.stack([y_a, y_b, y_c, y_d], axis=2)
        branch = jnp.einsum('blnw,nwd->blnd', ys, w_branch[l])
        gates = jax.nn.sigmoid(gate_logits.reshape(B_, L, N_BRANCHES, D_MODEL))
        merged = jnp.sum(gates * branch, axis=2)
        x = x + rms_norm(merged @ w_out[l], mix_post_gain[l])
        h = rms_norm(x, mlp_pre_gain[l])
        f = jnp.square(jax.nn.relu(h @ w_mlp_in[l])) @ w_mlp_out[l]
        x = x + rms_norm(f, mlp_post_gain[l])
    return x
```

```python
import functools
import math

import numpy as np
import jax
import jax.numpy as jnp
from jax import lax
from jax.experimental import pallas as pl
from jax.experimental.pallas import tpu as pltpu

F32 = jnp.float32
BF16 = jnp.bfloat16

D_MODEL = 2048
DEPTH = 4
BW = 512
D_FF = 4 * D_MODEL
NORM_EPS = 1e-6

S5_GROUP = 16
S5_GROUPS = 32
S5_STATE = 64
S5_NSTATE = S5_GROUPS * S5_STATE
S5_HALF = S5_NSTATE // 2

GDN_HEADS = 4
GDN_DH = 128
GDN_CHUNK = 64

SWA_HEADS = 8
SWA_KV_HEADS = 2
SWA_DH = 64
SWA_BLOCK = 128
WINDOW = 128
T5_BUCKETS = 32
T5_MAX_DISTANCE = 128

MLA_HEADS = 4
MLA_Q_RANK = 384
MLA_KV_RANK = 512
MLA_NOPE = 128
MLA_ROPE = 64
MLA_V = 128
ROPE_THETA = 10000.0

_OFF_U = 0
_OFF_QKV = 512
_OFF_Z = 2048
_OFF_BETA = 2560
_OFF_DECAY = 2568
_OFF_SWAQ = 2576
_OFF_SWAKV = 3088
_OFF_CQ = 3344
_OFF_CKV = 3728
_OFF_KR = 4240
_OFF_GATE = 4304
D_IN = _OFF_GATE + 4 * D_MODEL

MIX_QKV = 0
MIX_U = 1536
MIX_Z = 2048
MIX_SWAQ = 2560
MIX_CKV = 3072
MIX_SWAKV = 3584
MIX_CQ = 3840
MIX_KR = 4224
MIX_KRROT = 4352
MIX_SMALL = 4480
N_MIX = 4608

VMEM_LIMIT = 56 * 1024 * 1024


def _cp(sem, limit=VMEM_LIMIT):
    return pltpu.CompilerParams(dimension_semantics=sem, vmem_limit_bytes=limit)


def _dot(a, b):
    return jnp.dot(a, b, preferred_element_type=F32)


def _dot_nt(a, b):
    return lax.dot_general(a, b, (((1,), (1,)), ((), ())), preferred_element_type=F32)


def _dot_tn(a, b):
    return lax.dot_general(a, b, (((0,), (0,)), ((), ())), preferred_element_type=F32)


def _sigmoid(x):
    return 1.0 / (1.0 + jnp.exp(-x))


def _softplus(x):
    return jnp.maximum(x, 0.0) + jnp.log(1.0 + jnp.exp(-jnp.abs(x)))


def _in_proj_kernel(x_ref, g_ref, w_ref, o_ref, h_ref, h_scr):
    @pl.when(pl.program_id(1) == 0)
    def _():
        x = x_ref[...]
        ms = jnp.mean(x * x, axis=-1, keepdims=True)
        h = (x * lax.rsqrt(ms + NORM_EPS) * g_ref[...]).astype(BF16)
        h_scr[...] = h
        h_ref[...] = h

    o_ref[...] = _dot(h_scr[...], w_ref[...])


def in_proj(x, gain, w_mix, tm=512, tn=512):
    L = x.shape[0]
    n = w_mix.shape[1]
    return pl.pallas_call(
        _in_proj_kernel,
        grid=(L // tm, n // tn),
        in_specs=[
            pl.BlockSpec((tm, D_MODEL), lambda i, j: (i, 0)),
            pl.BlockSpec((1, D_MODEL), lambda i, j: (0, 0)),
            pl.BlockSpec((D_MODEL, tn), lambda i, j: (0, j)),
        ],
        out_specs=[
            pl.BlockSpec((tm, tn), lambda i, j: (i, j)),
            pl.BlockSpec((tm, D_MODEL), lambda i, j: (i, 0)),
        ],
        out_shape=[
            jax.ShapeDtypeStruct((L, n), F32),
            jax.ShapeDtypeStruct((L, D_MODEL), BF16),
        ],
        scratch_shapes=[pltpu.VMEM((tm, D_MODEL), BF16)],
        compiler_params=_cp(("parallel", "arbitrary")),
        name="in_proj",
    )(x, gain.reshape(1, D_MODEL), w_mix)


def _s5_kernel(*refs, reverse, finalize, T):
    if finalize:
        (u_ref, bd_ref, cd_ref, hs_ref, p_ref, yb_ref, dsk_ref, wglu_ref, bglu_ref,
         o_ref, bur, bui, carry) = refs
    else:
        u_ref, bd_ref, cd_ref, hs_ref, p_ref, o_ref, bur, bui, carry = refs
    nt = T // 8

    @pl.when(pl.program_id(0) == 0)
    def _():
        carry[...] = jnp.zeros_like(carry)

    u = u_ref[...]
    ub = u.astype(BF16)
    for k in range(2):
        r = _dot(ub[:, 256 * k:256 * (k + 1)], bd_ref[k])
        bur[:, S5_HALF * k:S5_HALF * (k + 1)] = r[:, :S5_HALF]
        bui[:, S5_HALF * k:S5_HALF * (k + 1)] = r[:, S5_HALF:]

    def tile(kk, c):
        cr, ci = c
        idx = (nt - 1 - kk) if reverse else kk
        t0 = pl.multiple_of(idx * 8, 8)
        xr = bur[pl.ds(t0, 8), :]
        xi = bui[pl.ds(t0, 8), :]
        for s, d in enumerate((1, 2, 4)):
            sh = (8 - d) if reverse else d
            sr = pltpu.roll(xr, sh, 0)
            si = pltpu.roll(xi, sh, 0)
            ar = hs_ref[2 * s]
            ai = hs_ref[2 * s + 1]
            xr, xi = xr + ar * sr - ai * si, xi + ar * si + ai * sr
        pr = p_ref[0]
        pi = p_ref[1]
        xr, xi = xr + pr * cr - pi * ci, xi + pr * ci + pi * cr
        bur[pl.ds(t0, 8), :] = xr
        bui[pl.ds(t0, 8), :] = xi
        row = 0 if reverse else 7
        return (jnp.broadcast_to(xr[row:row + 1, :], (8, S5_NSTATE)),
                jnp.broadcast_to(xi[row:row + 1, :], (8, S5_NSTATE)))

    cr, ci = lax.fori_loop(0, nt, tile, (carry[0], carry[1]))
    carry[0] = cr
    carry[1] = ci

    ys = []
    for k in range(2):
        sr = bur[:, S5_HALF * k:S5_HALF * (k + 1)].astype(BF16)
        si = bui[:, S5_HALF * k:S5_HALF * (k + 1)].astype(BF16)
        ys.append(_dot(sr, cd_ref[k, 0]) + _dot(si, cd_ref[k, 1]))
    y = jnp.concatenate(ys, axis=1)
    if finalize:
        y = y + yb_ref[...] + dsk_ref[...] * u
        y = jax.nn.gelu(y)
        gate = _sigmoid(_dot(y.astype(BF16), wglu_ref[...]) + bglu_ref[...])
        o_ref[...] = (y * gate).astype(o_ref.dtype)
    else:
        o_ref[...] = y


def s5_direction_call(u, bd, cd, hs, pw, *, reverse, fin=None, T=256):
    L = u.shape[0]
    nb = L // T
    rmap = (lambda i: (nb - 1 - i, 0)) if reverse else (lambda i: (i, 0))
    c3 = lambda i: (0, 0, 0)
    in_specs = [
        pl.BlockSpec((T, BW), rmap),
        pl.BlockSpec((2, 256, S5_NSTATE), c3),
        pl.BlockSpec((2, 2, S5_HALF, 256), lambda i: (0, 0, 0, 0)),
        pl.BlockSpec((6, 8, S5_NSTATE), c3),
        pl.BlockSpec((2, 8, S5_NSTATE), c3),
    ]
    args = [u, bd, cd, hs, pw]
    if fin is not None:
        yb, dsk, wglu, bglu = fin
        in_specs += [
            pl.BlockSpec((T, BW), rmap),
            pl.BlockSpec((1, BW), lambda i: (0, 0)),
            pl.BlockSpec((BW, BW), lambda i: (0, 0)),
            pl.BlockSpec((1, BW), lambda i: (0, 0)),
        ]
        args += [yb, dsk.reshape(1, BW), wglu, bglu.reshape(1, BW)]
    return pl.pallas_call(
        functools.partial(_s5_kernel, reverse=reverse, finalize=fin is not None, T=T),
        grid=(nb,),
        in_specs=in_specs,
        out_specs=pl.BlockSpec((T, BW), rmap),
        out_shape=jax.ShapeDtypeStruct((L, BW), BF16 if fin is not None else F32),
        scratch_shapes=[
            pltpu.VMEM((T, S5_NSTATE), F32),
            pltpu.VMEM((T, S5_NSTATE), F32),
            pltpu.VMEM((2, 8, S5_NSTATE), F32),
        ],
        compiler_params=_cp(("arbitrary",)),
        name="s5_rev" if reverse else "s5_fwd",
    )(*args)


def _s5_prep(lam_re, lam_im, log_step, b_re, b_im, c_re, c_im, reverse):
    G, P, Hg = S5_GROUPS, S5_STATE, S5_GROUP
    lam_re = jnp.minimum(lam_re.astype(F32), -1e-4)
    lam_im = lam_im.astype(F32)
    dt = jnp.exp(log_step.astype(F32))[:, None]
    mag = jnp.exp(lam_re * dt)
    abar_r = mag * jnp.cos(lam_im * dt)
    abar_i = mag * jnp.sin(lam_im * dt)
    den = lam_re * lam_re + lam_im * lam_im
    xr = abar_r - 1.0
    xi = abar_i
    coef_r = (xr * lam_re + xi * lam_im) / den
    coef_i = (xi * lam_re - xr * lam_im) / den
    b_re = b_re.astype(F32)
    b_im = b_im.astype(F32)
    bbar_r = coef_r[..., None] * b_re - coef_i[..., None] * b_im
    bbar_i = coef_r[..., None] * b_im + coef_i[..., None] * b_re

    eye = jnp.eye(16, dtype=F32)

    def bdiag_in(bb):
        blk = bb.transpose(0, 2, 1).reshape(2, 16, Hg, P)
        return jnp.einsum('gq,kghp->kghqp', eye, blk).reshape(2, 16 * Hg, 16 * P)

    bd = jnp.concatenate([bdiag_in(bbar_r), bdiag_in(bbar_i)], axis=-1).astype(BF16)

    def bdiag_out(cc):
        blk = cc.astype(F32).transpose(0, 2, 1).reshape(2, 16, P, Hg)
        return jnp.einsum('gq,kgph->kgpqh', eye, blk).reshape(2, 16 * P, 16 * Hg)

    cd = jnp.stack([bdiag_out(c_re), -bdiag_out(c_im)], axis=1).astype(BF16)

    ar = abar_r.reshape(-1)
    ai = abar_i.reshape(-1)

    def cm(a, b):
        return a[0] * b[0] - a[1] * b[1], a[0] * b[1] + a[1] * b[0]

    a1 = (ar, ai)
    a2 = cm(a1, a1)
    a3 = cm(a2, a1)
    a4 = cm(a2, a2)
    a5 = cm(a4, a1)
    a6 = cm(a4, a2)
    a7 = cm(a4, a3)
    a8 = cm(a4, a4)
    pows = [a1, a2, a3, a4, a5, a6, a7, a8]
    rows = np.arange(8)[:, None]
    hs = []
    for d, ad in ((1, a1), (2, a2), (4, a4)):
        mask = jnp.asarray((rows <= 7 - d) if reverse else (rows >= d), F32)
        hs.append(mask * ad[0][None, :])
        hs.append(mask * ad[1][None, :])
    hs = jnp.stack(hs)
    order = list(range(7, -1, -1)) if reverse else list(range(8))
    pw = jnp.stack([jnp.stack([pows[t][0] for t in order]),
                    jnp.stack([pows[t][1] for t in order])])
    return bd, cd, hs, pw


def s5_mixer(u, lam_re, lam_im, log_step, b_re, b_im, c_re, c_im, d_skip, w_glu, b_glu, T=256):
    pf = _s5_prep(lam_re[0], lam_im[0], log_step[0], b_re[0], b_im[0], c_re[0], c_im[0], False)
    pb = _s5_prep(lam_re[1], lam_im[1], log_step[1], b_re[1], b_im[1], c_re[1], c_im[1], True)
    yb = s5_direction_call(u, *pb, reverse=True, T=T)
    return s5_direction_call(u, *pf, reverse=False,
                             fin=(yb, d_skip.astype(F32), w_glu.astype(BF16), b_glu.astype(F32)), T=T)


def _gdn_kernel(*refs, reverse, finalize, R, nb):
    if finalize:
        (qkv_ref, prev_ref, next_ref, cw_ref, sm_ref, smt_ref, gpr_ref, gpc_ref,
         ob_ref, z_ref, og_ref, o_ref, s_scr, q_scr, k_scr, v_scr, g_scr, b_scr, o_scr) = refs
    else:
        (qkv_ref, prev_ref, next_ref, cw_ref, sm_ref, smt_ref, gpr_ref, gpc_ref,
         o_ref, s_scr, q_scr, k_scr, v_scr, g_scr, b_scr, o_scr) = refs
    C = GDN_CHUNK
    nc = R // C
    i = pl.program_id(0)

    @pl.when(i == 0)
    def _():
        s_scr[...] = jnp.zeros_like(s_scr)

    blk = (nb - 1 - i) if reverse else i
    x = qkv_ref[...]
    pv = jnp.where(blk > 0, prev_ref[...], 0.0)
    nx = jnp.where(blk < nb - 1, next_ref[...], 0.0)
    xp = jnp.concatenate([pv, x, nx], axis=0)
    w = cw_ref[...]
    conv = (w[0:1] * xp[6:6 + R] + w[1:2] * xp[7:7 + R]
            + w[2:3] * xp[8:8 + R] + w[3:4] * xp[9:9 + R])
    act = conv * _sigmoid(conv)
    for h in range(GDN_HEADS):
        sl = slice(h * GDN_DH, (h + 1) * GDN_DH)
        qh = act[:, sl]
        q_scr[:, sl] = qh * lax.rsqrt(jnp.sum(qh * qh, axis=-1, keepdims=True) + 1e-6) * (GDN_DH ** -0.5)
        kh = act[:, BW + h * GDN_DH:BW + (h + 1) * GDN_DH]
        k_scr[:, sl] = kh * lax.rsqrt(jnp.sum(kh * kh, axis=-1, keepdims=True) + 1e-6)
    v_scr[...] = act[:, 2 * BW:3 * BW]
    sm = sm_ref[...]
    b_scr[...] = _sigmoid(sm[:, 0:4])
    g_scr[...] = gpr_ref[0:1, :] * _softplus(sm[:, 4:8] + gpr_ref[1:2, :])

    ii = lax.broadcasted_iota(jnp.int32, (C, C), 0)
    jj = lax.broadcasted_iota(jnp.int32, (C, C), 1)
    if reverse:
        incl = jj >= ii
        strict = jj > ii
    else:
        incl = jj <= ii
        strict = jj < ii
    tri_l = jnp.where(incl, 1.0, 0.0).astype(F32)
    tri_u = jnp.where((ii >= jj) if reverse else (ii <= jj), 1.0, 0.0).astype(F32)
    last = 0 if reverse else C - 1
    eye = jnp.where(ii == jj, 1.0, 0.0).astype(F32)
    pair_masks = []
    for lvl in range(6):
        bi = ii >> lvl
        bj = jj >> lvl
        lo_, hi_ = (bi, bj) if reverse else (bj, bi)
        pair_masks.append(((ii >> (lvl + 1)) == (jj >> (lvl + 1))) & ((hi_ & 1) == 1) & ((lo_ & 1) == 0))

    def chunk(cc, carry):
        cidx = (nc - 1 - cc) if reverse else cc
        r0 = pl.multiple_of(cidx * C, C)
        gch = g_scr[pl.ds(r0, C), :]
        bch = b_scr[pl.ds(r0, C), :]
        gt = gpc_ref[:, 0:1] * _softplus(smt_ref[cidx][4:8, :] + gpc_ref[:, 1:2])
        gc_col = jnp.dot(tri_l, gch, preferred_element_type=F32, precision=lax.Precision.HIGHEST)
        gc_row = jnp.dot(gt, tri_u, preferred_element_type=F32, precision=lax.Precision.HIGHEST)
        for h in range(GDN_HEADS):
            sl = slice(h * GDN_DH, (h + 1) * GDN_DH)
            gcol = gc_col[:, h:h + 1]
            grow = gc_row[h:h + 1, :]
            glast = gc_col[last:last + 1, h:h + 1]
            decay = jnp.exp(jnp.where(incl, gcol - grow, -jnp.inf))
            k = k_scr[pl.ds(r0, C), sl]
            q = q_scr[pl.ds(r0, C), sl]
            v = v_scr[pl.ds(r0, C), sl]
            beta = bch[:, h:h + 1]
            kb = k * beta
            kbf = k.astype(BF16)
            kk = _dot_nt(kb.astype(BF16), kbf)
            qk = _dot_nt(q.astype(BF16), kbf)
            a = jnp.where(strict, kk * decay, 0.0)
            attn = qk * decay
            xs = jnp.concatenate([v * beta, kb * jnp.exp(gcol)], axis=1)
            minv = eye - jnp.where(pair_masks[0], a, 0.0)
            for lvl in range(1, 6):
                mb = minv.astype(BF16)
                em = _dot(jnp.where(pair_masks[lvl], a, 0.0).astype(BF16), mb)
                minv = minv - _dot(mb, em.astype(BF16))
            xs = _dot(minv.astype(BF16), xs.astype(BF16))
            u_ = xs[:, :GDN_DH]
            w_ = xs[:, GDN_DH:]
            s_h = s_scr[h]
            sb = s_h.astype(BF16)
            v_new = u_ - _dot(w_.astype(BF16), sb)
            vnb = v_new.astype(BF16)
            o = _dot((q * jnp.exp(gcol)).astype(BF16), sb) + _dot(attn.astype(BF16), vnb)
            kdec = (k * jnp.exp(glast - gcol)).astype(BF16)
            s_scr[h] = s_h * jnp.exp(glast) + _dot_tn(kdec, vnb)
            o_scr[pl.ds(r0, C), sl] = o
        return carry

    lax.fori_loop(0, nc, chunk, 0)

    if finalize:
        o = o_scr[...] + ob_ref[...]
        z = z_ref[...]
        og = og_ref[...]
        for h in range(GDN_HEADS):
            sl = slice(h * GDN_DH, (h + 1) * GDN_DH)
            oh = o[:, sl]
            oh = oh * lax.rsqrt(jnp.mean(oh * oh, axis=-1, keepdims=True) + NORM_EPS) * og
            zh = z[:, sl]
            o_ref[:, sl] = (oh * (zh * _sigmoid(zh))).astype(o_ref.dtype)
    else:
        o_ref[...] = o_scr[...]


def gdn_direction_call(qkv, conv_w, smd, smdt, gpr, gpc, *, reverse, fin=None, R=256):
    L = qkv.shape[0]
    nb = L // R
    r8 = R // 8
    nc = R // GDN_CHUNK
    bmap = (lambda i: nb - 1 - i) if reverse else (lambda i: i)
    in_specs = [
        pl.BlockSpec((R, 3 * BW), lambda i: (bmap(i), 0)),
        pl.BlockSpec((8, 3 * BW), lambda i: (jnp.maximum(bmap(i) * r8 - 1, 0), 0)),
        pl.BlockSpec((8, 3 * BW), lambda i: (jnp.minimum((bmap(i) + 1) * r8, L // 8 - 1), 0)),
        pl.BlockSpec((4, 3 * BW), lambda i: (0, 0)),
        pl.BlockSpec((R, 8), lambda i: (bmap(i), 0)),
        pl.BlockSpec((nc, 8, GDN_CHUNK), lambda i: (bmap(i), 0, 0)),
        pl.BlockSpec((2, 4), lambda i: (0, 0)),
        pl.BlockSpec((4, 2), lambda i: (0, 0)),
    ]
    args = [qkv, qkv, qkv, conv_w, smd, smdt, gpr, gpc]
    if fin is not None:
        ob, z, og = fin
        in_specs += [
            pl.BlockSpec((R, BW), lambda i: (bmap(i), 0)),
            pl.BlockSpec((R, BW), lambda i: (bmap(i), 0)),
            pl.BlockSpec((1, GDN_DH), lambda i: (0, 0)),
        ]
        args += [ob, z, og.reshape(1, GDN_DH)]
    return pl.pallas_call(
        functools.partial(_gdn_kernel, reverse=reverse, finalize=fin is not None, R=R, nb=nb),
        grid=(nb,),
        in_specs=in_specs,
        out_specs=pl.BlockSpec((R, BW), lambda i: (bmap(i), 0)),
        out_shape=jax.ShapeDtypeStruct((L, BW), BF16 if fin is not None else F32),
        scratch_shapes=[
            pltpu.VMEM((GDN_HEADS, GDN_DH, GDN_DH), F32),
            pltpu.VMEM((R, BW), F32),
            pltpu.VMEM((R, BW), F32),
            pltpu.VMEM((R, BW), F32),
            pltpu.VMEM((R, 4), F32),
            pltpu.VMEM((R, 4), F32),
            pltpu.VMEM((R, BW), F32),
        ],
        compiler_params=_cp(("arbitrary",)),
        name="gdn_rev" if reverse else "gdn_fwd",
    )(*args)


def gdn_mixer(qkv, z, small, conv_w, a_log, dt_bias, o_gain, R=256):
    L = qkv.shape[0]
    coef = -jnp.exp(a_log.astype(F32))
    dtb = dt_bias.astype(F32)
    outs = []
    o_b = None
    for d in (1, 0):
        smd = jnp.concatenate([small[:, 4 * d:4 * d + 4], small[:, 8 + 4 * d:12 + 4 * d]], axis=1)
        smdt = smd.reshape(L // GDN_CHUNK, GDN_CHUNK, 8).transpose(0, 2, 1)
        gpr = jnp.stack([coef[d], dtb[d]])
        gpc = gpr.T
        if d == 1:
            o_b = gdn_direction_call(qkv, conv_w.astype(F32), smd, smdt, gpr, gpc, reverse=True, R=R)
        else:
            return gdn_direction_call(qkv, conv_w.astype(F32), smd, smdt, gpr, gpc, reverse=False,
                                      fin=(o_b, z, o_gain.astype(F32)), R=R)


def _t5_bucket(rel):
    nb = T5_BUCKETS // 2
    max_exact = nb // 2
    ret = jnp.where(rel > 0, nb, 0)
    n = jnp.abs(rel)
    nf = jnp.maximum(n, 1).astype(F32)
    large = max_exact + (jnp.log(nf / max_exact) / math.log(T5_MAX_DISTANCE / max_exact)
                         * (nb - max_exact)).astype(jnp.int32)
    large = jnp.minimum(large, nb - 1)
    return ret + jnp.where(n < max_exact, n, large)


def _swa_kernel(q_ref, kp_ref, kc_ref, kn_ref, vp_ref, vc_ref, vn_ref, bias_ref, sink_ref,
                o_ref, *, nb):
    i = pl.program_id(0)
    B = SWA_BLOCK
    G = SWA_HEADS // SWA_KV_HEADS
    qi = lax.broadcasted_iota(jnp.int32, (G * B, 3 * B), 0) % B
    sj = lax.broadcasted_iota(jnp.int32, (G * B, 3 * B), 1)
    rel = sj - B - qi
    lo = jnp.where(i == 0, B, 0)
    hi = jnp.where(i == nb - 1, 2 * B, 3 * B)
    valid = (jnp.abs(rel) <= WINDOW) & (sj >= lo) & (sj < hi)
    for g in range(SWA_KV_HEADS):
        kb = jnp.concatenate([kp_ref[g], kc_ref[g], kn_ref[g]], axis=0).astype(BF16)
        vb = jnp.concatenate([vp_ref[g], vc_ref[g], vn_ref[g]], axis=0).astype(BF16)
        q = q_ref[g * G:(g + 1) * G].reshape(G * B, SWA_DH).astype(BF16)
        bias = bias_ref[g * G:(g + 1) * G].reshape(G * B, 3 * B)
        s = _dot_nt(q, kb) * (SWA_DH ** -0.5) + bias
        s = jnp.where(valid, s, -1e30)
        sink = jnp.broadcast_to(sink_ref[g * G:(g + 1) * G], (G, B, 1)).reshape(G * B, 1)
        m = jnp.maximum(jnp.max(s, axis=-1, keepdims=True), sink)
        p = jnp.exp(s - m)
        den = jnp.sum(p, axis=-1, keepdims=True) + jnp.exp(sink - m)
        o = _dot(p.astype(BF16), vb) / den
        o_ref[g * G:(g + 1) * G] = o.reshape(G, B, SWA_DH).astype(o_ref.dtype)


def swa_mixer(q, kv, sink, t5_bias):
    L = q.shape[0]
    B = SWA_BLOCK
    nb = L // B
    qh = q.reshape(L, SWA_HEADS, SWA_DH).transpose(1, 0, 2)
    kh = kv[:, :SWA_KV_HEADS * SWA_DH].reshape(L, SWA_KV_HEADS, SWA_DH).transpose(1, 0, 2)
    vh = kv[:, SWA_KV_HEADS * SWA_DH:].reshape(L, SWA_KV_HEADS, SWA_DH).transpose(1, 0, 2)
    rel = jnp.arange(3 * B)[None, :] - B - jnp.arange(B)[:, None]
    bias = t5_bias.astype(F32)[_t5_bucket(rel)].transpose(2, 0, 1)
    kspec = lambda f: pl.BlockSpec((SWA_KV_HEADS, B, SWA_DH), lambda i: (0, f(i), 0))
    prev = lambda i: jnp.maximum(i - 1, 0)
    cur = lambda i: i
    nxt = lambda i: jnp.minimum(i + 1, nb - 1)
    out = pl.pallas_call(
        functools.partial(_swa_kernel, nb=nb),
        grid=(nb,),
        in_specs=[
            pl.BlockSpec((SWA_HEADS, B, SWA_DH), lambda i: (0, i, 0)),
            kspec(prev), kspec(cur), kspec(nxt),
            kspec(prev), kspec(cur), kspec(nxt),
            pl.BlockSpec((SWA_HEADS, B, 3 * B), lambda i: (0, 0, 0)),
            pl.BlockSpec((SWA_HEADS, 1, 1), lambda i: (0, 0, 0)),
        ],
        out_specs=pl.BlockSpec((SWA_HEADS, B, SWA_DH), lambda i: (0, i, 0)),
        out_shape=jax.ShapeDtypeStruct((SWA_HEADS, L, SWA_DH), BF16),
        compiler_params=_cp(("parallel",)),
        name="swa",
    )(qh, kh, kh, kh, vh, vh, vh, bias, sink.astype(F32).reshape(SWA_HEADS, 1, 1))
    return out.transpose(1, 0, 2).reshape(L, SWA_HEADS * SWA_DH)


def _mla_prep_kernel(cq_ref, ckv_ref, kr_ref, krr_ref, cos_ref, sin_ref, qg_ref, kvg_ref,
                     wqn_ref, wqp_ref, wqr_ref, wkv_ref, q_ref, k_ref, v_ref):
    cq = cq_ref[...]
    qn = (cq * lax.rsqrt(jnp.mean(cq * cq, axis=-1, keepdims=True) + NORM_EPS) * qg_ref[...]).astype(BF16)
    ckv = ckv_ref[...]
    kn = (ckv * lax.rsqrt(jnp.mean(ckv * ckv, axis=-1, keepdims=True) + NORM_EPS) * kvg_ref[...]).astype(BF16)
    cos = cos_ref[...]
    sin = sin_ref[...]
    q_nope = _dot(qn, wqn_ref[...])
    q_pe = _dot(qn, wqp_ref[...])
    q_rot = _dot(qn, wqr_ref[...])
    kv = _dot(kn, wkv_ref[...])
    k_pe = (kr_ref[...] * cos + krr_ref[...] * sin).astype(BF16)
    for h in range(MLA_HEADS):
        sl = slice(h * 128, (h + 1) * 128)
        q_ref[h, :, 0:128] = q_nope[:, sl].astype(BF16)
        q_ref[h, :, 128:256] = (q_pe[:, sl] * cos + q_rot[:, sl] * sin).astype(BF16)
        k_ref[h, :, 0:128] = kv[:, h * 256:h * 256 + 128].astype(BF16)
        k_ref[h, :, 128:256] = k_pe
        v_ref[h] = kv[:, h * 256 + 128:(h + 1) * 256].astype(BF16)


def _mla_attn_kernel(q_ref, k_ref, v_ref, o_ref, *, tk, nk):
    q = q_ref[0]
    tq = q.shape[0]
    scale = (MLA_NOPE + MLA_ROPE) ** -0.5

    def body(kc, carry):
        m, l, acc = carry
        k0 = pl.multiple_of(kc * tk, tk)
        k = k_ref[0, pl.ds(k0, tk), :]
        v = v_ref[0, pl.ds(k0, tk), :]
        s = _dot_nt(q, k) * scale
        m_new = jnp.maximum(m, jnp.max(s, axis=-1, keepdims=True))
        alpha = jnp.exp(m - m_new)
        p = jnp.exp(s - m_new)
        l = alpha * l + jnp.sum(p, axis=-1, keepdims=True)
        acc = alpha * acc + _dot(p.astype(BF16), v)
        return m_new, l, acc

    m0 = jnp.full((tq, 1), -jnp.inf, F32)
    l0 = jnp.zeros((tq, 1), F32)
    a0 = jnp.zeros((tq, MLA_V), F32)
    m, l, acc = lax.fori_loop(0, nk, body, (m0, l0, a0))
    o_ref[...] = (acc / l).astype(o_ref.dtype)


def mla_mixer(c_q, c_kv, kr_pad, krr_pad, q_gain, kv_gain, w_uq, w_ukv, R=512, tq=512, tk=512):
    L = c_q.shape[0]
    H = MLA_HEADS
    half = MLA_ROPE // 2
    pos = jnp.arange(L, dtype=F32)
    inv_freq = ROPE_THETA ** (-jnp.arange(0, MLA_ROPE, 2, dtype=F32) / MLA_ROPE)
    ang = pos[:, None] * inv_freq[None, :]
    cos, sin = jnp.cos(ang), jnp.sin(ang)
    zpad = jnp.zeros((L, 128 - MLA_ROPE), F32)
    cos_t = jnp.concatenate([cos, cos, zpad], axis=1)
    sin_t = jnp.concatenate([-sin, sin, zpad], axis=1)
    wq = w_uq.reshape(MLA_Q_RANK, H, MLA_NOPE + MLA_ROPE)
    wqn = wq[:, :, :MLA_NOPE].reshape(MLA_Q_RANK, H * MLA_NOPE).astype(BF16)
    wpe = wq[:, :, MLA_NOPE:]
    zw = jnp.zeros((MLA_Q_RANK, H, 128 - MLA_ROPE), w_uq.dtype)
    wqp = jnp.concatenate([wpe, zw], axis=2).reshape(MLA_Q_RANK, H * 128).astype(BF16)
    wrot = jnp.concatenate([wpe[:, :, half:], wpe[:, :, :half], zw], axis=2)
    wqr = wrot.reshape(MLA_Q_RANK, H * 128).astype(BF16)
    wkv = w_ukv.astype(BF16)
    row = lambda w: pl.BlockSpec((R, w), lambda i: (i, 0))
    full = lambda a, b: pl.BlockSpec((a, b), lambda i: (0, 0))
    q, k, v = pl.pallas_call(
        _mla_prep_kernel,
        grid=(L // R,),
        in_specs=[row(MLA_Q_RANK), row(MLA_KV_RANK), row(128), row(128), row(128), row(128),
                  full(1, MLA_Q_RANK), full(1, MLA_KV_RANK),
                  full(MLA_Q_RANK, H * 128), full(MLA_Q_RANK, H * 128), full(MLA_Q_RANK, H * 128),
                  full(MLA_KV_RANK, H * 256)],
        out_specs=[pl.BlockSpec((H, R, 256), lambda i: (0, i, 0)),
                   pl.BlockSpec((H, R, 256), lambda i: (0, i, 0)),
                   pl.BlockSpec((H, R, 128), lambda i: (0, i, 0))],
        out_shape=[jax.ShapeDtypeStruct((H, L, 256), BF16),
                   jax.ShapeDtypeStruct((H, L, 256), BF16),
                   jax.ShapeDtypeStruct((H, L, 128), BF16)],
        compiler_params=_cp(("parallel",)),
        name="mla_prep",
    )(c_q, c_kv, kr_pad, krr_pad, cos_t, sin_t,
      q_gain.astype(F32).reshape(1, -1), kv_gain.astype(F32).reshape(1, -1), wqn, wqp, wqr, wkv)
    return pl.pallas_call(
        functools.partial(_mla_attn_kernel, tk=tk, nk=L // tk),
        grid=(H, L // tq),
        in_specs=[pl.BlockSpec((1, tq, 256), lambda h, i: (h, i, 0)),
                  pl.BlockSpec((1, L, 256), lambda h, i: (h, 0, 0)),
                  pl.BlockSpec((1, L, 128), lambda h, i: (h, 0, 0))],
        out_specs=pl.BlockSpec((tq, MLA_V), lambda h, i: (i, h)),
        out_shape=jax.ShapeDtypeStruct((L, H * MLA_V), BF16),
        compiler_params=_cp(("parallel", "parallel")),
        name="mla_attn",
    )(q, k, v)


def _merge_kernel(h_ref, ya_ref, yb_ref, yc_ref, yd_ref, g0_ref, g1_ref, g2_ref, g3_ref, p_ref, o_ref):
    h = h_ref[...]
    acc = None
    for b, (y_ref, g_ref) in enumerate(((ya_ref, g0_ref), (yb_ref, g1_ref), (yc_ref, g2_ref), (yd_ref, g3_ref))):
        gate = _sigmoid(_dot(h, g_ref[...]))
        t = gate * _dot(y_ref[...], p_ref[b])
        acc = t if acc is None else acc + t
    o_ref[...] = acc.astype(o_ref.dtype)


def merge_branches(h, ys, w_gate, w_branch, tm=512, tn=256):
    L = h.shape[0]
    nj = D_MODEL // tn
    gspec = lambda b: pl.BlockSpec((D_MODEL, tn), lambda i, j: (0, b * nj + j))
    yspec = pl.BlockSpec((tm, BW), lambda i, j: (i, 0))
    return pl.pallas_call(
        _merge_kernel,
        grid=(L // tm, nj),
        in_specs=[pl.BlockSpec((tm, D_MODEL), lambda i, j: (i, 0)), yspec, yspec, yspec, yspec,
                  gspec(0), gspec(1), gspec(2), gspec(3),
                  pl.BlockSpec((4, BW, tn), lambda i, j: (0, 0, j))],
        out_specs=pl.BlockSpec((tm, tn), lambda i, j: (i, j)),
        out_shape=jax.ShapeDtypeStruct((L, D_MODEL), BF16),
        compiler_params=_cp(("parallel", "arbitrary")),
        name="merge",
    )(h, *ys, w_gate, w_gate, w_gate, w_gate, w_branch)


def _out_proj_kernel(x_ref, m_ref, w_ref, g_ref, o_ref):
    f = _dot(m_ref[...], w_ref[...])
    f = f * lax.rsqrt(jnp.mean(f * f, axis=-1, keepdims=True) + NORM_EPS) * g_ref[...]
    o_ref[...] = x_ref[...] + f


def out_proj(x, m, w_out, gain, tm=512):
    L = x.shape[0]
    return pl.pallas_call(
        _out_proj_kernel,
        grid=(L // tm,),
        in_specs=[pl.BlockSpec((tm, D_MODEL), lambda i: (i, 0)),
                  pl.BlockSpec((tm, D_MODEL), lambda i: (i, 0)),
                  pl.BlockSpec((D_MODEL, D_MODEL), lambda i: (0, 0)),
                  pl.BlockSpec((1, D_MODEL), lambda i: (0, 0))],
        out_specs=pl.BlockSpec((tm, D_MODEL), lambda i: (i, 0)),
        out_shape=jax.ShapeDtypeStruct((L, D_MODEL), F32),
        compiler_params=_cp(("parallel",)),
        name="out_proj",
    )(x, m, w_out, gain.reshape(1, D_MODEL))


def _mlp_kernel(x_ref, g1_ref, w1_ref, w2_ref, g2_ref, o_ref, h_scr, acc):
    j = pl.program_id(1)

    @pl.when(j == 0)
    def _():
        x = x_ref[...]
        ms = jnp.mean(x * x, axis=-1, keepdims=True)
        h_scr[...] = (x * lax.rsqrt(ms + NORM_EPS) * g1_ref[...]).astype(BF16)
        acc[...] = jnp.zeros_like(acc)

    a = _dot(h_scr[...], w1_ref[...])
    a = jnp.maximum(a, 0.0)
    a = (a * a).astype(BF16)
    acc[...] += _dot(a, w2_ref[...])

    @pl.when(j == pl.num_programs(1) - 1)
    def _():
        f = acc[...]
        f = f * lax.rsqrt(jnp.mean(f * f, axis=-1, keepdims=True) + NORM_EPS) * g2_ref[...]
        o_ref[...] = x_ref[...] + f


def mlp(x, g1, w1, w2, g2, tm=512, tf=512):
    L = x.shape[0]
    return pl.pallas_call(
        _mlp_kernel,
        grid=(L // tm, D_FF // tf),
        in_specs=[pl.BlockSpec((tm, D_MODEL), lambda i, j: (i, 0)),
                  pl.BlockSpec((1, D_MODEL), lambda i, j: (0, 0)),
                  pl.BlockSpec((D_MODEL, tf), lambda i, j: (0, j)),
                  pl.BlockSpec((tf, D_MODEL), lambda i, j: (j, 0)),
                  pl.BlockSpec((1, D_MODEL), lambda i, j: (0, 0))],
        out_specs=pl.BlockSpec((tm, D_MODEL), lambda i, j: (i, 0)),
        out_shape=jax.ShapeDtypeStruct((L, D_MODEL), F32),
        scratch_shapes=[pltpu.VMEM((tm, D_MODEL), BF16), pltpu.VMEM((tm, D_MODEL), F32)],
        compiler_params=_cp(("parallel", "arbitrary")),
        name="mlp",
    )(x, g1.reshape(1, D_MODEL), w1, w2, g2.reshape(1, D_MODEL))


def _mix_weight(w):
    half = MLA_ROPE // 2
    z = lambda n: jnp.zeros((D_MODEL, n), w.dtype)
    kr = w[:, _OFF_KR:_OFF_KR + MLA_ROPE]
    cols = [
        w[:, _OFF_QKV:_OFF_QKV + 3 * BW],
        w[:, _OFF_U:_OFF_U + BW],
        w[:, _OFF_Z:_OFF_Z + BW],
        w[:, _OFF_SWAQ:_OFF_SWAQ + BW],
        w[:, _OFF_CKV:_OFF_CKV + MLA_KV_RANK],
        w[:, _OFF_SWAKV:_OFF_SWAKV + 256],
        w[:, _OFF_CQ:_OFF_CQ + MLA_Q_RANK],
        kr, z(64),
        kr[:, half:], kr[:, :half], z(64),
        w[:, _OFF_BETA:_OFF_BETA + 16], z(112),
    ]
    return jnp.concatenate(cols, axis=1).astype(BF16)


def kernel(x, w_in, s5_lam_re, s5_lam_im, s5_log_step, s5_b_re, s5_b_im, s5_c_re, s5_c_im, s5_d, s5_w_glu, s5_b_glu, gdn_conv, gdn_a_log, gdn_dt_bias, gdn_o_gain, swa_sink, t5_bias, mla_q_gain, mla_kv_gain, mla_w_uq, mla_w_ukv, w_branch, w_out, mix_pre_gain, mix_post_gain, mlp_pre_gain, mlp_post_gain, w_mlp_in, w_mlp_out):
    B_, L, _ = x.shape
    outs = []
    for b in range(B_):
        xb = x[b].astype(F32)
        for l in range(DEPTH):
            w_mix = _mix_weight(w_in[l])
            w_gate = w_in[l][:, _OFF_GATE:].astype(BF16)
            proj, h = in_proj(xb, mix_pre_gain[l].astype(F32), w_mix)
            y_a = s5_mixer(proj[:, MIX_U:MIX_U + BW], s5_lam_re[l], s5_lam_im[l], s5_log_step[l],
                           s5_b_re[l], s5_b_im[l], s5_c_re[l], s5_c_im[l], s5_d[l], s5_w_glu[l],
                           s5_b_glu[l])
            y_b = gdn_mixer(proj[:, MIX_QKV:MIX_QKV + 3 * BW], proj[:, MIX_Z:MIX_Z + BW],
                            proj[:, MIX_SMALL:MIX_SMALL + 16], gdn_conv[l], gdn_a_log[l],
                            gdn_dt_bias[l], gdn_o_gain[l])
            y_c = swa_mixer(proj[:, MIX_SWAQ:MIX_SWAQ + BW], proj[:, MIX_SWAKV:MIX_SWAKV + 256],
                            swa_sink[l], t5_bias)
            y_d = mla_mixer(proj[:, MIX_CQ:MIX_CQ + MLA_Q_RANK], proj[:, MIX_CKV:MIX_CKV + MLA_KV_RANK],
                            proj[:, MIX_KR:MIX_KR + 128], proj[:, MIX_KRROT:MIX_KRROT + 128],
                            mla_q_gain[l], mla_kv_gain[l], mla_w_uq[l], mla_w_ukv[l])
            merged = merge_branches(h, (y_a, y_b, y_c, y_d), w_gate, w_branch[l].astype(BF16))
            xb = out_proj(xb, merged, w_out[l].astype(BF16), mix_post_gain[l].astype(F32))
            xb = mlp(xb, mlp_pre_gain[l].astype(F32), w_mlp_in[l].astype(BF16),
                     w_mlp_out[l].astype(BF16), mlp_post_gain[l].astype(F32))
        outs.append(xb)
    return jnp.stack(outs).astype(x.dtype)
```

```python
import functools
import math

import numpy as np
import jax
import jax.numpy as jnp
from jax import lax
from jax.experimental import pallas as pl
from jax.experimental.pallas import tpu as pltpu

F32 = jnp.float32
BF16 = jnp.bfloat16

D_MODEL = 2048
DEPTH = 4
BW = 512
D_FF = 4 * D_MODEL
NORM_EPS = 1e-6

S5_GROUP = 16
S5_GROUPS = 32
S5_STATE = 64
S5_NSTATE = S5_GROUPS * S5_STATE
S5_HALF = S5_NSTATE // 2

GDN_HEADS = 4
GDN_DH = 128
GDN_CHUNK = 64

SWA_HEADS = 8
SWA_KV_HEADS = 2
SWA_DH = 64
SWA_BLOCK = 128
WINDOW = 128
T5_BUCKETS = 32
T5_MAX_DISTANCE = 128

MLA_HEADS = 4
MLA_Q_RANK = 384
MLA_KV_RANK = 512
MLA_NOPE = 128
MLA_ROPE = 64
MLA_V = 128
ROPE_THETA = 10000.0

_OFF_U = 0
_OFF_QKV = 512
_OFF_Z = 2048
_OFF_BETA = 2560
_OFF_DECAY = 2568
_OFF_SWAQ = 2576
_OFF_SWAKV = 3088
_OFF_CQ = 3344
_OFF_CKV = 3728
_OFF_KR = 4240
_OFF_GATE = 4304
D_IN = _OFF_GATE + 4 * D_MODEL

MIX_QKV = 0
MIX_U = 1536
MIX_Z = 2048
MIX_SWAQ = 2560
MIX_CKV = 3072
MIX_SWAKV = 3584
MIX_CQ = 3840
MIX_KR = 4224
MIX_KRROT = 4352
MIX_SMALL = 4480
N_MIX = 4608

VMEM_LIMIT = 56 * 1024 * 1024


def _cp(sem, limit=VMEM_LIMIT):
    return pltpu.CompilerParams(dimension_semantics=sem, vmem_limit_bytes=limit)


def _dot(a, b):
    return jnp.dot(a, b, preferred_element_type=F32)


def _dot_nt(a, b):
    return lax.dot_general(a, b, (((1,), (1,)), ((), ())), preferred_element_type=F32)


def _dot_tn(a, b):
    return lax.dot_general(a, b, (((0,), (0,)), ((), ())), preferred_element_type=F32)


def _sigmoid(x):
    return 1.0 / (1.0 + jnp.exp(-x))


def _softplus(x):
    return jnp.maximum(x, 0.0) + jnp.log(1.0 + jnp.exp(-jnp.abs(x)))


def _in_proj_kernel(x_ref, g_ref, w_ref, o_ref, h_ref, h_scr):
    @pl.when(pl.program_id(1) == 0)
    def _():
        x = x_ref[...]
        ms = jnp.mean(x * x, axis=-1, keepdims=True)
        h = (x * lax.rsqrt(ms + NORM_EPS) * g_ref[...]).astype(BF16)
        h_scr[...] = h
        h_ref[...] = h

    o_ref[...] = _dot(h_scr[...], w_ref[...])


def in_proj(x, gain, w_mix, tm=512, tn=512):
    L = x.shape[0]
    n = w_mix.shape[1]
    return pl.pallas_call(
        _in_proj_kernel,
        grid=(L // tm, n // tn),
        in_specs=[
            pl.BlockSpec((tm, D_MODEL), lambda i, j: (i, 0)),
            pl.BlockSpec((1, D_MODEL), lambda i, j: (0, 0)),
            pl.BlockSpec((D_MODEL, tn), lambda i, j: (0, j)),
        ],
        out_specs=[
            pl.BlockSpec((tm, tn), lambda i, j: (i, j)),
            pl.BlockSpec((tm, D_MODEL), lambda i, j: (i, 0)),
        ],
        out_shape=[
            jax.ShapeDtypeStruct((L, n), F32),
            jax.ShapeDtypeStruct((L, D_MODEL), BF16),
        ],
        scratch_shapes=[pltpu.VMEM((tm, D_MODEL), BF16)],
        compiler_params=_cp(("parallel", "arbitrary")),
        name="in_proj",
    )(x, gain.reshape(1, D_MODEL), w_mix)


def _s5_kernel(*refs, reverse, finalize, T):
    if finalize:
        (u_ref, bd_ref, cd_ref, hs_ref, p_ref, yb_ref, dsk_ref, wglu_ref, bglu_ref,
         o_ref, bur, bui, carry) = refs
    else:
        u_ref, bd_ref, cd_ref, hs_ref, p_ref, o_ref, bur, bui, carry = refs
    nt = T // 8

    @pl.when(pl.program_id(0) == 0)
    def _():
        carry[...] = jnp.zeros_like(carry)

    u = u_ref[...]
    ub = u.astype(BF16)
    for k in range(2):
        r = _dot(ub[:, 256 * k:256 * (k + 1)], bd_ref[k])
        bur[:, S5_HALF * k:S5_HALF * (k + 1)] = r[:, :S5_HALF]
        bui[:, S5_HALF * k:S5_HALF * (k + 1)] = r[:, S5_HALF:]

    def tile(kk, c):
        cr, ci = c
        idx = (nt - 1 - kk) if reverse else kk
        t0 = pl.multiple_of(idx * 8, 8)
        xr = bur[pl.ds(t0, 8), :]
        xi = bui[pl.ds(t0, 8), :]
        for s, d in enumerate((1, 2, 4)):
            sh = (8 - d) if reverse else d
            sr = pltpu.roll(xr, sh, 0)
            si = pltpu.roll(xi, sh, 0)
            ar = hs_ref[2 * s]
            ai = hs_ref[2 * s + 1]
            xr, xi = xr + ar * sr - ai * si, xi + ar * si + ai * sr
        pr = p_ref[0]
        pi = p_ref[1]
        xr, xi = xr + pr * cr - pi * ci, xi + pr * ci + pi * cr
        bur[pl.ds(t0, 8), :] = xr
        bui[pl.ds(t0, 8), :] = xi
        row = 0 if reverse else 7
        return (jnp.broadcast_to(xr[row:row + 1, :], (8, S5_NSTATE)),
                jnp.broadcast_to(xi[row:row + 1, :], (8, S5_NSTATE)))

    cr, ci = lax.fori_loop(0, nt, tile, (carry[0], carry[1]))
    carry[0] = cr
    carry[1] = ci

    ys = []
    for k in range(2):
        sr = bur[:, S5_HALF * k:S5_HALF * (k + 1)].astype(BF16)
        si = bui[:, S5_HALF * k:S5_HALF * (k + 1)].astype(BF16)
        ys.append(_dot(sr, cd_ref[k, 0]) + _dot(si, cd_ref[k, 1]))
    y = jnp.concatenate(ys, axis=1)
    if finalize:
        y = y + yb_ref[...] + dsk_ref[...] * u
        y = jax.nn.gelu(y)
        gate = _sigmoid(_dot(y.astype(BF16), wglu_ref[...]) + bglu_ref[...])
        o_ref[...] = (y * gate).astype(o_ref.dtype)
    else:
        o_ref[...] = y


def s5_direction_call(u, bd, cd, hs, pw, *, reverse, fin=None, T=256):
    L = u.shape[0]
    nb = L // T
    rmap = (lambda i: (nb - 1 - i, 0)) if reverse else (lambda i: (i, 0))
    c3 = lambda i: (0, 0, 0)
    in_specs = [
        pl.BlockSpec((T, BW), rmap),
        pl.BlockSpec((2, 256, S5_NSTATE), c3),
        pl.BlockSpec((2, 2, S5_HALF, 256), lambda i: (0, 0, 0, 0)),
        pl.BlockSpec((6, 8, S5_NSTATE), c3),
        pl.BlockSpec((2, 8, S5_NSTATE), c3),
    ]
    args = [u, bd, cd, hs, pw]
    if fin is not None:
        yb, dsk, wglu, bglu = fin
        in_specs += [
            pl.BlockSpec((T, BW), rmap),
            pl.BlockSpec((1, BW), lambda i: (0, 0)),
            pl.BlockSpec((BW, BW), lambda i: (0, 0)),
            pl.BlockSpec((1, BW), lambda i: (0, 0)),
        ]
        args += [yb, dsk.reshape(1, BW), wglu, bglu.reshape(1, BW)]
    return pl.pallas_call(
        functools.partial(_s5_kernel, reverse=reverse, finalize=fin is not None, T=T),
        grid=(nb,),
        in_specs=in_specs,
        out_specs=pl.BlockSpec((T, BW), rmap),
        out_shape=jax.ShapeDtypeStruct((L, BW), BF16 if fin is not None else F32),
        scratch_shapes=[
            pltpu.VMEM((T, S5_NSTATE), F32),
            pltpu.VMEM((T, S5_NSTATE), F32),
            pltpu.VMEM((2, 8, S5_NSTATE), F32),
        ],
        compiler_params=_cp(("arbitrary",)),
        name="s5_rev" if reverse else "s5_fwd",
    )(*args)


def _s5_prep(lam_re, lam_im, log_step, b_re, b_im, c_re, c_im, reverse):
    G, P, Hg = S5_GROUPS, S5_STATE, S5_GROUP
    lam_re = jnp.minimum(lam_re.astype(F32), -1e-4)
    lam_im = lam_im.astype(F32)
    dt = jnp.exp(log_step.astype(F32))[:, None]
    mag = jnp.exp(lam_re * dt)
    abar_r = mag * jnp.cos(lam_im * dt)
    abar_i = mag * jnp.sin(lam_im * dt)
    den = lam_re * lam_re + lam_im * lam_im
    xr = abar_r - 1.0
    xi = abar_i
    coef_r = (xr * lam_re + xi * lam_im) / den
    coef_i = (xi * lam_re - xr * lam_im) / den
    b_re = b_re.astype(F32)
    b_im = b_im.astype(F32)
    bbar_r = coef_r[..., None] * b_re - coef_i[..., None] * b_im
    bbar_i = coef_r[..., None] * b_im + coef_i[..., None] * b_re

    eye = jnp.eye(16, dtype=F32)

    def bdiag_in(bb):
        blk = bb.transpose(0, 2, 1).reshape(2, 16, Hg, P)
        return jnp.einsum('gq,kghp->kghqp', eye, blk).reshape(2, 16 * Hg, 16 * P)

    bd = jnp.concatenate([bdiag_in(bbar_r), bdiag_in(bbar_i)], axis=-1).astype(BF16)

    def bdiag_out(cc):
        blk = cc.astype(F32).transpose(0, 2, 1).reshape(2, 16, P, Hg)
        return jnp.einsum('gq,kgph->kgpqh', eye, blk).reshape(2, 16 * P, 16 * Hg)

    cd = jnp.stack([bdiag_out(c_re), -bdiag_out(c_im)], axis=1).astype(BF16)

    ar = abar_r.reshape(-1)
    ai = abar_i.reshape(-1)

    def cm(a, b):
        return a[0] * b[0] - a[1] * b[1], a[0] * b[1] + a[1] * b[0]

    a1 = (ar, ai)
    a2 = cm(a1, a1)
    a3 = cm(a2, a1)
    a4 = cm(a2, a2)
    a5 = cm(a4, a1)
    a6 = cm(a4, a2)
    a7 = cm(a4, a3)
    a8 = cm(a4, a4)
    pows = [a1, a2, a3, a4, a5, a6, a7, a8]
    rows = np.arange(8)[:, None]
    hs = []
    for d, ad in ((1, a1), (2, a2), (4, a4)):
        mask = jnp.asarray((rows <= 7 - d) if reverse else (rows >= d), F32)
        hs.append(mask * ad[0][None, :])
        hs.append(mask * ad[1][None, :])
    hs = jnp.stack(hs)
    order = list(range(7, -1, -1)) if reverse else list(range(8))
    pw = jnp.stack([jnp.stack([pows[t][0] for t in order]),
                    jnp.stack([pows[t][1] for t in order])])
    return bd, cd, hs, pw


def s5_mixer(u, lam_re, lam_im, log_step, b_re, b_im, c_re, c_im, d_skip, w_glu, b_glu, T=256):
    pf = _s5_prep(lam_re[0], lam_im[0], log_step[0], b_re[0], b_im[0], c_re[0], c_im[0], False)
    pb = _s5_prep(lam_re[1], lam_im[1], log_step[1], b_re[1], b_im[1], c_re[1], c_im[1], True)
    yb = s5_direction_call(u, *pb, reverse=True, T=T)
    return s5_direction_call(u, *pf, reverse=False,
                             fin=(yb, d_skip.astype(F32), w_glu.astype(BF16), b_glu.astype(F32)), T=T)


def _gdn_kernel(*refs, reverse, finalize, R, nb):
    if finalize:
        (qkv_ref, prev_ref, next_ref, cw_ref, sm_ref, smt_ref, gpr_ref, gpc_ref,
         ob_ref, z_ref, og_ref, o_ref, s_scr, q_scr, k_scr, v_scr, g_scr, b_scr, o_scr) = refs
    else:
        (qkv_ref, prev_ref, next_ref, cw_ref, sm_ref, smt_ref, gpr_ref, gpc_ref,
         o_ref, s_scr, q_scr, k_scr, v_scr, g_scr, b_scr, o_scr) = refs
    C = GDN_CHUNK
    nc = R // C
    i = pl.program_id(0)

    @pl.when(i == 0)
    def _():
        s_scr[...] = jnp.zeros_like(s_scr)

    blk = (nb - 1 - i) if reverse else i
    x = qkv_ref[...]
    pv = jnp.where(blk > 0, prev_ref[...], 0.0)
    nx = jnp.where(blk < nb - 1, next_ref[...], 0.0)
    xp = jnp.concatenate([pv, x, nx], axis=0)
    w = cw_ref[...]
    conv = (w[0:1] * xp[6:6 + R] + w[1:2] * xp[7:7 + R]
            + w[2:3] * xp[8:8 + R] + w[3:4] * xp[9:9 + R])
    act = conv * _sigmoid(conv)
    for h in range(GDN_HEADS):
        sl = slice(h * GDN_DH, (h + 1) * GDN_DH)
        qh = act[:, sl]
        q_scr[:, sl] = qh * lax.rsqrt(jnp.sum(qh * qh, axis=-1, keepdims=True) + 1e-6) * (GDN_DH ** -0.5)
        kh = act[:, BW + h * GDN_DH:BW + (h + 1) * GDN_DH]
        k_scr[:, sl] = kh * lax.rsqrt(jnp.sum(kh * kh, axis=-1, keepdims=True) + 1e-6)
    v_scr[...] = act[:, 2 * BW:3 * BW]
    sm = sm_ref[...]
    b_scr[...] = _sigmoid(sm[:, 0:4])
    g_scr[...] = gpr_ref[0:1, :] * _softplus(sm[:, 4:8] + gpr_ref[1:2, :])

    ii = lax.broadcasted_iota(jnp.int32, (C, C), 0)
    jj = lax.broadcasted_iota(jnp.int32, (C, C), 1)
    if reverse:
        incl = jj >= ii
        strict = jj > ii
    else:
        incl = jj <= ii
        strict = jj < ii
    tri_l = jnp.where(incl, 1.0, 0.0).astype(F32)
    tri_u = jnp.where((ii >= jj) if reverse else (ii <= jj), 1.0, 0.0).astype(F32)
    last = 0 if reverse else C - 1
    eye = jnp.where(ii == jj, 1.0, 0.0).astype(F32)
    pair_masks = []
    for lvl in range(6):
        bi = ii >> lvl
        bj = jj >> lvl
        lo_, hi_ = (bi, bj) if reverse else (bj, bi)
        pair_masks.append(((ii >> (lvl + 1)) == (jj >> (lvl + 1))) & ((hi_ & 1) == 1) & ((lo_ & 1) == 0))

    def chunk(cc, carry):
        cidx = (nc - 1 - cc) if reverse else cc
        r0 = pl.multiple_of(cidx * C, C)
        gch = g_scr[pl.ds(r0, C), :]
        bch = b_scr[pl.ds(r0, C), :]
        gt = gpc_ref[:, 0:1] * _softplus(smt_ref[cidx][4:8, :] + gpc_ref[:, 1:2])
        gc_col = jnp.dot(tri_l, gch, preferred_element_type=F32, precision=lax.Precision.HIGHEST)
        gc_row = jnp.dot(gt, tri_u, preferred_element_type=F32, precision=lax.Precision.HIGHEST)
        for h in range(GDN_HEADS):
            sl = slice(h * GDN_DH, (h + 1) * GDN_DH)
            gcol = gc_col[:, h:h + 1]
            grow = gc_row[h:h + 1, :]
            glast = gc_col[last:last + 1, h:h + 1]
            decay = jnp.exp(jnp.where(incl, gcol - grow, -jnp.inf))
            k = k_scr[pl.ds(r0, C), sl]
            q = q_scr[pl.ds(r0, C), sl]
            v = v_scr[pl.ds(r0, C), sl]
            beta = bch[:, h:h + 1]
            kb = k * beta
            kbf = k.astype(BF16)
            kk = _dot_nt(kb.astype(BF16), kbf)
            qk = _dot_nt(q.astype(BF16), kbf)
            a = jnp.where(strict, kk * decay, 0.0)
            attn = qk * decay
            xs = jnp.concatenate([v * beta, kb * jnp.exp(gcol)], axis=1)
            minv = eye - jnp.where(pair_masks[0], a, 0.0)
            for lvl in range(1, 6):
                mb = minv.astype(BF16)
                em = _dot(jnp.where(pair_masks[lvl], a, 0.0).astype(BF16), mb)
                minv = minv - _dot(mb, em.astype(BF16))
            xs = _dot(minv.astype(BF16), xs.astype(BF16))
            u_ = xs[:, :GDN_DH]
            w_ = xs[:, GDN_DH:]
            s_h = s_scr[h]
            sb = s_h.astype(BF16)
            v_new = u_ - _dot(w_.astype(BF16), sb)
            vnb = v_new.astype(BF16)
            o = _dot((q * jnp.exp(gcol)).astype(BF16), sb) + _dot(attn.astype(BF16), vnb)
            kdec = (k * jnp.exp(glast - gcol)).astype(BF16)
            s_scr[h] = s_h * jnp.exp(glast) + _dot_tn(kdec, vnb)
            o_scr[pl.ds(r0, C), sl] = o
        return carry

    lax.fori_loop(0, nc, chunk, 0)

    if finalize:
        o = o_scr[...] + ob_ref[...]
        z = z_ref[...]
        og = og_ref[...]
        for h in range(GDN_HEADS):
            sl = slice(h * GDN_DH, (h + 1) * GDN_DH)
            oh = o[:, sl]
            oh = oh * lax.rsqrt(jnp.mean(oh * oh, axis=-1, keepdims=True) + NORM_EPS) * og
            zh = z[:, sl]
            o_ref[:, sl] = (oh * (zh * _sigmoid(zh))).astype(o_ref.dtype)
    else:
        o_ref[...] = o_scr[...]


def gdn_direction_call(qkv, conv_w, smd, smdt, gpr, gpc, *, reverse, fin=None, R=256):
    L = qkv.shape[0]
    nb = L // R
    r8 = R // 8
    nc = R // GDN_CHUNK
    bmap = (lambda i: nb - 1 - i) if reverse else (lambda i: i)
    in_specs = [
        pl.BlockSpec((R, 3 * BW), lambda i: (bmap(i), 0)),
        pl.BlockSpec((8, 3 * BW), lambda i: (jnp.maximum(bmap(i) * r8 - 1, 0), 0)),
        pl.BlockSpec((8, 3 * BW), lambda i: (jnp.minimum((bmap(i) + 1) * r8, L // 8 - 1), 0)),
        pl.BlockSpec((4, 3 * BW), lambda i: (0, 0)),
        pl.BlockSpec((R, 8), lambda i: (bmap(i), 0)),
        pl.BlockSpec((nc, 8, GDN_CHUNK), lambda i: (bmap(i), 0, 0)),
        pl.BlockSpec((2, 4), lambda i: (0, 0)),
        pl.BlockSpec((4, 2), lambda i: (0, 0)),
    ]
    args = [qkv, qkv, qkv, conv_w, smd, smdt, gpr, gpc]
    if fin is not None:
        ob, z, og = fin
        in_specs += [
            pl.BlockSpec((R, BW), lambda i: (bmap(i), 0)),
            pl.BlockSpec((R, BW), lambda i: (bmap(i), 0)),
            pl.BlockSpec((1, GDN_DH), lambda i: (0, 0)),
        ]
        args += [ob, z, og.reshape(1, GDN_DH)]
    return pl.pallas_call(
        functools.partial(_gdn_kernel, reverse=reverse, finalize=fin is not None, R=R, nb=nb),
        grid=(nb,),
        in_specs=in_specs,
        out_specs=pl.BlockSpec((R, BW), lambda i: (bmap(i), 0)),
        out_shape=jax.ShapeDtypeStruct((L, BW), BF16 if fin is not None else F32),
        scratch_shapes=[
            pltpu.VMEM((GDN_HEADS, GDN_DH, GDN_DH), F32),
            pltpu.VMEM((R, BW), F32),
            pltpu.VMEM((R, BW), F32),
            pltpu.VMEM((R, BW), F32),
            pltpu.VMEM((R, 4), F32),
            pltpu.VMEM((R, 4), F32),
            pltpu.VMEM((R, BW), F32),
        ],
        compiler_params=_cp(("arbitrary",)),
        name="gdn_rev" if reverse else "gdn_fwd",
    )(*args)


def gdn_mixer(qkv, z, small, conv_w, a_log, dt_bias, o_gain, R=256):
    L = qkv.shape[0]
    coef = -jnp.exp(a_log.astype(F32))
    dtb = dt_bias.astype(F32)
    outs = []
    o_b = None
    for d in (1, 0):
        smd = jnp.concatenate([small[:, 4 * d:4 * d + 4], small[:, 8 + 4 * d:12 + 4 * d]], axis=1)
        smdt = smd.reshape(L // GDN_CHUNK, GDN_CHUNK, 8).transpose(0, 2, 1)
        gpr = jnp.stack([coef[d], dtb[d]])
        gpc = gpr.T
        if d == 1:
            o_b = gdn_direction_call(qkv, conv_w.astype(F32), smd, smdt, gpr, gpc, reverse=True, R=R)
        else:
            return gdn_direction_call(qkv, conv_w.astype(F32), smd, smdt, gpr, gpc, reverse=False,
                                      fin=(o_b, z, o_gain.astype(F32)), R=R)


GDN_HC = GDN_HEADS * GDN_CHUNK
GDN_UNROLL = 2


def _stack_heads(ref, r0):
    return jnp.concatenate(
        [ref[pl.ds(r0, GDN_CHUNK), h * GDN_DH:(h + 1) * GDN_DH] for h in range(GDN_HEADS)], axis=0)


def _gdn_prologue(d, blk, nb, R, qkv_ref, prev_ref, next_ref, cw_ref, sm_ref, gt_ref, gpr_ref, gpc_ref,
                  q_scr, k_scr, v_scr, b_scr, gc_scr, gr_scr, tri):
    x = qkv_ref[...]
    pv = jnp.where(blk > 0, prev_ref[...], 0.0)
    nx = jnp.where(blk < nb - 1, next_ref[...], 0.0)
    xp = jnp.concatenate([pv, x, nx], axis=0)
    w = cw_ref[...]
    conv = (w[0:1] * xp[6:6 + R] + w[1:2] * xp[7:7 + R]
            + w[2:3] * xp[8:8 + R] + w[3:4] * xp[9:9 + R])
    act = conv * _sigmoid(conv)
    for h in range(GDN_HEADS):
        sl = slice(h * GDN_DH, (h + 1) * GDN_DH)
        qh = act[:, sl]
        q_scr[d, :, sl] = qh * lax.rsqrt(jnp.sum(qh * qh, axis=-1, keepdims=True) + 1e-6) * (GDN_DH ** -0.5)
        kh = act[:, BW + h * GDN_DH:BW + (h + 1) * GDN_DH]
        k_scr[d, :, sl] = kh * lax.rsqrt(jnp.sum(kh * kh, axis=-1, keepdims=True) + 1e-6)
    v_scr[d] = act[:, 2 * BW:3 * BW]
    sm = sm_ref[...]
    b_scr[d] = _sigmoid(sm[:, 0:4])
    g = gpr_ref[2 * d:2 * d + 1, :] * _softplus(sm[:, 4:8] + gpr_ref[2 * d + 1:2 * d + 2, :])
    gt = gpc_ref[:, 2 * d:2 * d + 1] * _softplus(gt_ref[...] + gpc_ref[:, 2 * d + 1:2 * d + 2])
    tri_l, tri_u = tri
    gc_scr[d] = jnp.dot(tri_l, g, preferred_element_type=F32, precision=lax.Precision.HIGHEST)
    gr = jnp.dot(gt, tri_u, preferred_element_type=F32, precision=lax.Precision.HIGHEST)
    for c in range(R // GDN_CHUNK):
        gr_scr[d, c] = gr[:, c * GDN_CHUNK:(c + 1) * GDN_CHUNK]


def _gdn_chunks(insts, m_scr, q_scr, k_scr, v_scr, b_scr, gc_scr, gr_scr, s_scr):
    C = GDN_CHUNK
    H = GDN_HEADS
    n = len(insts)
    rng = range(n)
    ds = [t[0] for t in insts]
    r0s = [pl.multiple_of(t[1] * C, C) for t in insts]
    lasts = [0 if t[2] else C - 1 for t in insts]
    ks = [_stack_heads(k_scr.at[ds[i]], r0s[i]) for i in rng]
    qs = [_stack_heads(q_scr.at[ds[i]], r0s[i]) for i in rng]
    vs = [_stack_heads(v_scr.at[ds[i]], r0s[i]) for i in rng]
    bch = [b_scr[ds[i], pl.ds(r0s[i], C), :] for i in rng]
    gch = [gc_scr[ds[i], pl.ds(r0s[i], C), :] for i in rng]
    grw = [gr_scr[ds[i], insts[i][1]] for i in rng]
    beta = [jnp.concatenate([b[:, h:h + 1] for h in range(H)], axis=0) for b in bch]
    gcol = [jnp.concatenate([g[:, h:h + 1] for h in range(H)], axis=0) for g in gch]
    grow = [jnp.concatenate([g[h:h + 1, :] for h in range(H)], axis=1) for g in grw]
    glast_r = [jnp.concatenate(
        [jnp.broadcast_to(grw[i][h:h + 1, lasts[i]:lasts[i] + 1], (1, C)) for h in range(H)], axis=1)
        for i in rng]
    e_cat = [jnp.concatenate(
        [jnp.broadcast_to(jnp.exp(gch[i][lasts[i]:lasts[i] + 1, h:h + 1]), (1, GDN_DH)) for h in range(H)],
        axis=1) for i in rng]

    kst = [k.T for k in ks]
    kstb = [k.astype(BF16) for k in kst]
    kbs = [ks[i] * beta[i] for i in rng]
    g2 = [_dot(jnp.concatenate([kbs[i], qs[i]], axis=0).astype(BF16), kstb[i]) for i in rng]
    decay = [jnp.exp((gcol[i] - grow[i]) + m_scr[ds[i], 0]) for i in rng]
    a = [g2[i][:GDN_HC] * decay[i] * m_scr[ds[i], 1] for i in rng]
    attn = [(g2[i][GDN_HC:] * decay[i]).astype(BF16) for i in rng]
    minv = [m_scr[ds[i], 8] - a[i] * m_scr[ds[i], 2] for i in rng]
    for lvl in range(1, 6):
        mb = [m.astype(BF16) for m in minv]
        em = [_dot((a[i] * m_scr[ds[i], 2 + lvl]).astype(BF16), mb[i]).astype(BF16) for i in rng]
        minv = [minv[i] - _dot(mb[i], em[i]) for i in rng]
    egc = [jnp.exp(g) for g in gcol]
    rhs = [jnp.concatenate([vs[i] * beta[i], kbs[i] * egc[i]], axis=1).astype(BF16) for i in rng]
    xs = [_dot(minv[i].astype(BF16), rhs[i]) for i in rng]
    qg = [qs[i] * egc[i] for i in rng]
    wq = [jnp.concatenate([xs[i][:, GDN_DH:], qg[i]], axis=0).astype(BF16) for i in rng]
    kdect = [(kst[i] * jnp.exp(glast_r[i] - grow[i])).astype(BF16) for i in rng]

    zero = jnp.zeros((C, GDN_DH), BF16)
    by_dir = {}
    for i in rng:
        by_dir.setdefault(ds[i], []).append(i)
    depth = max(len(v) for v in by_dir.values())
    s_cat = {d: s_scr[d] for d in by_dir}
    for step in range(depth):
        act = [v[step] for v in by_dir.values() if len(v) > step]
        p = [_dot(wq[i], s_cat[ds[i]].astype(BF16)) for i in act]
        vn = []
        qss = []
        for j, i in enumerate(act):
            ws = jnp.concatenate(
                [p[j][h * C:(h + 1) * C, h * GDN_DH:(h + 1) * GDN_DH] for h in range(H)], axis=0)
            qss.append(jnp.concatenate(
                [p[j][GDN_HC + h * C:GDN_HC + (h + 1) * C, h * GDN_DH:(h + 1) * GDN_DH] for h in range(H)],
                axis=0))
            vn.append((xs[i][:, :GDN_DH] - ws).astype(BF16))
        o = [qss[j] + _dot(attn[i], vn[j]) for j, i in enumerate(act)]
        for j, i in enumerate(act):
            vn_bd = jnp.concatenate(
                [jnp.concatenate([vn[j][h * C:(h + 1) * C] if hh == h else zero for hh in range(H)], axis=1)
                 for h in range(H)], axis=0)
            s_cat[ds[i]] = s_cat[ds[i]] * e_cat[i] + _dot(kdect[i], vn_bd)
            o_ref = insts[i][3]
            for h in range(H):
                o_ref[pl.ds(r0s[i], C), h * GDN_DH:(h + 1) * GDN_DH] = o[j][h * C:(h + 1) * C]
    for d in by_dir:
        s_scr[d] = s_cat[d]


def _gdn_bidir_kernel(qf_ref, pf_ref, nf_ref, qb_ref, pb_ref, nb_ref, cw_ref, smf_ref, smb_ref,
                      gtf_ref, gtb_ref, gpr_ref, gpc_ref, of_ref, ob_ref,
                      s_scr, q_scr, k_scr, v_scr, b_scr, gc_scr, gr_scr, m_scr, *, R, nb):
    C = GDN_CHUNK
    nc = R // C
    i = pl.program_id(0)

    @pl.when(i == 0)
    def _():
        s_scr[...] = jnp.zeros_like(s_scr)
        ii = lax.broadcasted_iota(jnp.int32, (GDN_HC, GDN_HC), 0)
        jj = lax.broadcasted_iota(jnp.int32, (GDN_HC, GDN_HC), 1)
        blockd = (ii // C) == (jj // C)
        for d, reverse in enumerate((False, True)):
            incl = blockd & ((jj >= ii) if reverse else (jj <= ii))
            strict = blockd & ((jj > ii) if reverse else (jj < ii))
            m_scr[d, 0] = jnp.where(incl, 0.0, -jnp.inf).astype(F32)
            m_scr[d, 1] = jnp.where(strict, 1.0, 0.0).astype(F32)
            for lvl in range(6):
                bi = ii >> lvl
                bj = jj >> lvl
                lo_, hi_ = (bi, bj) if reverse else (bj, bi)
                pm = ((ii >> (lvl + 1)) == (jj >> (lvl + 1))) & ((hi_ & 1) == 1) & ((lo_ & 1) == 0)
                m_scr[d, 2 + lvl] = jnp.where(pm, 1.0, 0.0).astype(F32)
            m_scr[d, 8] = jnp.where(ii == jj, 1.0, 0.0).astype(F32)

    ri = lax.broadcasted_iota(jnp.int32, (R, R), 0)
    rj = lax.broadcasted_iota(jnp.int32, (R, R), 1)
    same = (ri // C) == (rj // C)
    tris = []
    for reverse in (False, True):
        tl = jnp.where(same & ((rj >= ri) if reverse else (rj <= ri)), 1.0, 0.0).astype(F32)
        tu = jnp.where(same & ((ri >= rj) if reverse else (ri <= rj)), 1.0, 0.0).astype(F32)
        tris.append((tl, tu))
    _gdn_prologue(0, i, nb, R, qf_ref, pf_ref, nf_ref, cw_ref, smf_ref, gtf_ref, gpr_ref, gpc_ref,
                  q_scr, k_scr, v_scr, b_scr, gc_scr, gr_scr, tris[0])
    _gdn_prologue(1, nb - 1 - i, nb, R, qb_ref, pb_ref, nb_ref, cw_ref, smb_ref, gtb_ref, gpr_ref, gpc_ref,
                  q_scr, k_scr, v_scr, b_scr, gc_scr, gr_scr, tris[1])

    def chunk(cc, carry):
        insts = []
        for uu in range(GDN_UNROLL):
            c = cc * GDN_UNROLL + uu
            insts.append((0, c, False, of_ref))
            insts.append((1, nc - 1 - c, True, ob_ref))
        _gdn_chunks(insts, m_scr, q_scr, k_scr, v_scr, b_scr, gc_scr, gr_scr, s_scr)
        return carry

    lax.fori_loop(0, nc // GDN_UNROLL, chunk, 0)


def _gdn_final_kernel(of_ref, ob_ref, z_ref, og_ref, o_ref):
    o = of_ref[...] + ob_ref[...]
    z = z_ref[...]
    og = og_ref[...]
    for h in range(GDN_HEADS):
        sl = slice(h * GDN_DH, (h + 1) * GDN_DH)
        oh = o[:, sl]
        oh = oh * lax.rsqrt(jnp.mean(oh * oh, axis=-1, keepdims=True) + NORM_EPS) * og
        zh = z[:, sl]
        o_ref[:, sl] = (oh * (zh * _sigmoid(zh))).astype(o_ref.dtype)


def gdn_mixer_fused(qkv, z, small, conv_w, a_log, dt_bias, o_gain, R=256, RF=512):
    L = qkv.shape[0]
    nb = L // R
    r8 = R // 8
    coef = -jnp.exp(a_log.astype(F32))
    dtb = dt_bias.astype(F32)
    gpr = jnp.stack([coef[0], dtb[0], coef[1], dtb[1]])
    gpc = gpr.T
    smf = jnp.concatenate([small[:, 0:4], small[:, 8:12]], axis=1)
    smb = jnp.concatenate([small[:, 4:8], small[:, 12:16]], axis=1)
    gtf = small[:, 8:12].T
    gtb = small[:, 12:16].T
    fmap = lambda i: i
    bmap = lambda i: nb - 1 - i
    blk = lambda m: pl.BlockSpec((R, 3 * BW), lambda i: (m(i), 0))
    prv = lambda m: pl.BlockSpec((8, 3 * BW), lambda i: (jnp.maximum(m(i) * r8 - 1, 0), 0))
    nxt = lambda m: pl.BlockSpec((8, 3 * BW), lambda i: (jnp.minimum((m(i) + 1) * r8, L // 8 - 1), 0))
    full = lambda a, b: pl.BlockSpec((a, b), lambda i: (0, 0))
    o_f, o_b = pl.pallas_call(
        functools.partial(_gdn_bidir_kernel, R=R, nb=nb),
        grid=(nb,),
        in_specs=[blk(fmap), prv(fmap), nxt(fmap), blk(bmap), prv(bmap), nxt(bmap),
                  full(4, 3 * BW),
                  pl.BlockSpec((R, 8), lambda i: (i, 0)),
                  pl.BlockSpec((R, 8), lambda i: (nb - 1 - i, 0)),
                  pl.BlockSpec((4, R), lambda i: (0, i)),
                  pl.BlockSpec((4, R), lambda i: (0, nb - 1 - i)),
                  full(4, 4), full(4, 4)],
        out_specs=[pl.BlockSpec((R, BW), lambda i: (i, 0)),
                   pl.BlockSpec((R, BW), lambda i: (nb - 1 - i, 0))],
        out_shape=[jax.ShapeDtypeStruct((L, BW), F32), jax.ShapeDtypeStruct((L, BW), F32)],
        scratch_shapes=[
            pltpu.VMEM((2, GDN_DH, BW), F32),
            pltpu.VMEM((2, R, BW), F32),
            pltpu.VMEM((2, R, BW), F32),
            pltpu.VMEM((2, R, BW), F32),
            pltpu.VMEM((2, R, 4), F32),
            pltpu.VMEM((2, R, 4), F32),
            pltpu.VMEM((2, R // GDN_CHUNK, 4, GDN_CHUNK), F32),
            pltpu.VMEM((2, 9, GDN_HC, GDN_HC), F32),
        ],
        compiler_params=_cp(("arbitrary",)),
        name="gdn_bidir",
    )(qkv, qkv, qkv, qkv, qkv, qkv, conv_w.astype(F32), smf, smb, gtf, gtb, gpr, gpc)
    row = pl.BlockSpec((RF, BW), lambda i: (i, 0))
    return pl.pallas_call(
        _gdn_final_kernel,
        grid=(L // RF,),
        in_specs=[row, row, row, pl.BlockSpec((1, GDN_DH), lambda i: (0, 0))],
        out_specs=row,
        out_shape=jax.ShapeDtypeStruct((L, BW), BF16),
        compiler_params=_cp(("parallel",)),
        name="gdn_final",
    )(o_f, o_b, z, o_gain.astype(F32).reshape(1, GDN_DH))


def _t5_bucket(rel):
    nb = T5_BUCKETS // 2
    max_exact = nb // 2
    ret = jnp.where(rel > 0, nb, 0)
    n = jnp.abs(rel)
    nf = jnp.maximum(n, 1).astype(F32)
    large = max_exact + (jnp.log(nf / max_exact) / math.log(T5_MAX_DISTANCE / max_exact)
                         * (nb - max_exact)).astype(jnp.int32)
    large = jnp.minimum(large, nb - 1)
    return ret + jnp.where(n < max_exact, n, large)


def _swa_kernel(q_ref, kp_ref, kc_ref, kn_ref, vp_ref, vc_ref, vn_ref, bias_ref, sink_ref,
                o_ref, *, nb):
    i = pl.program_id(0)
    B = SWA_BLOCK
    G = SWA_HEADS // SWA_KV_HEADS
    qi = lax.broadcasted_iota(jnp.int32, (G * B, 3 * B), 0) % B
    sj = lax.broadcasted_iota(jnp.int32, (G * B, 3 * B), 1)
    rel = sj - B - qi
    lo = jnp.where(i == 0, B, 0)
    hi = jnp.where(i == nb - 1, 2 * B, 3 * B)
    valid = (jnp.abs(rel) <= WINDOW) & (sj >= lo) & (sj < hi)
    for g in range(SWA_KV_HEADS):
        kb = jnp.concatenate([kp_ref[g], kc_ref[g], kn_ref[g]], axis=0).astype(BF16)
        vb = jnp.concatenate([vp_ref[g], vc_ref[g], vn_ref[g]], axis=0).astype(BF16)
        q = q_ref[g * G:(g + 1) * G].reshape(G * B, SWA_DH).astype(BF16)
        bias = bias_ref[g * G:(g + 1) * G].reshape(G * B, 3 * B)
        s = _dot_nt(q, kb) * (SWA_DH ** -0.5) + bias
        s = jnp.where(valid, s, -1e30)
        sink = jnp.broadcast_to(sink_ref[g * G:(g + 1) * G], (G, B, 1)).reshape(G * B, 1)
        m = jnp.maximum(jnp.max(s, axis=-1, keepdims=True), sink)
        p = jnp.exp(s - m)
        den = jnp.sum(p, axis=-1, keepdims=True) + jnp.exp(sink - m)
        o = _dot(p.astype(BF16), vb) / den
        o_ref[g * G:(g + 1) * G] = o.reshape(G, B, SWA_DH).astype(o_ref.dtype)


def swa_mixer(q, kv, sink, t5_bias):
    L = q.shape[0]
    B = SWA_BLOCK
    nb = L // B
    qh = q.reshape(L, SWA_HEADS, SWA_DH).transpose(1, 0, 2)
    kh = kv[:, :SWA_KV_HEADS * SWA_DH].reshape(L, SWA_KV_HEADS, SWA_DH).transpose(1, 0, 2)
    vh = kv[:, SWA_KV_HEADS * SWA_DH:].reshape(L, SWA_KV_HEADS, SWA_DH).transpose(1, 0, 2)
    rel = jnp.arange(3 * B)[None, :] - B - jnp.arange(B)[:, None]
    bias = t5_bias.astype(F32)[_t5_bucket(rel)].transpose(2, 0, 1)
    kspec = lambda f: pl.BlockSpec((SWA_KV_HEADS, B, SWA_DH), lambda i: (0, f(i), 0))
    prev = lambda i: jnp.maximum(i - 1, 0)
    cur = lambda i: i
    nxt = lambda i: jnp.minimum(i + 1, nb - 1)
    out = pl.pallas_call(
        functools.partial(_swa_kernel, nb=nb),
        grid=(nb,),
        in_specs=[
            pl.BlockSpec((SWA_HEADS, B, SWA_DH), lambda i: (0, i, 0)),
            kspec(prev), kspec(cur), kspec(nxt),
            kspec(prev), kspec(cur), kspec(nxt),
            pl.BlockSpec((SWA_HEADS, B, 3 * B), lambda i: (0, 0, 0)),
            pl.BlockSpec((SWA_HEADS, 1, 1), lambda i: (0, 0, 0)),
        ],
        out_specs=pl.BlockSpec((SWA_HEADS, B, SWA_DH), lambda i: (0, i, 0)),
        out_shape=jax.ShapeDtypeStruct((SWA_HEADS, L, SWA_DH), BF16),
        compiler_params=_cp(("parallel",)),
        name="swa",
    )(qh, kh, kh, kh, vh, vh, vh, bias, sink.astype(F32).reshape(SWA_HEADS, 1, 1))
    return out.transpose(1, 0, 2).reshape(L, SWA_HEADS * SWA_DH)


def _mla_prep_kernel(cq_ref, ckv_ref, kr_ref, krr_ref, cos_ref, sin_ref, qg_ref, kvg_ref,
                     wqn_ref, wqp_ref, wqr_ref, wkv_ref, q_ref, k_ref, v_ref):
    cq = cq_ref[...]
    qn = (cq * lax.rsqrt(jnp.mean(cq * cq, axis=-1, keepdims=True) + NORM_EPS) * qg_ref[...]).astype(BF16)
    ckv = ckv_ref[...]
    kn = (ckv * lax.rsqrt(jnp.mean(ckv * ckv, axis=-1, keepdims=True) + NORM_EPS) * kvg_ref[...]).astype(BF16)
    cos = cos_ref[...]
    sin = sin_ref[...]
    q_nope = _dot(qn, wqn_ref[...])
    q_pe = _dot(qn, wqp_ref[...])
    q_rot = _dot(qn, wqr_ref[...])
    kv = _dot(kn, wkv_ref[...])
    k_pe = (kr_ref[...] * cos + krr_ref[...] * sin).astype(BF16)
    for h in range(MLA_HEADS):
        sl = slice(h * 128, (h + 1) * 128)
        q_ref[h, :, 0:128] = q_nope[:, sl].astype(BF16)
        q_ref[h, :, 128:256] = (q_pe[:, sl] * cos + q_rot[:, sl] * sin).astype(BF16)
        k_ref[h, :, 0:128] = kv[:, h * 256:h * 256 + 128].astype(BF16)
        k_ref[h, :, 128:256] = k_pe
        v_ref[h] = kv[:, h * 256 + 128:(h + 1) * 256].astype(BF16)


def _mla_attn_kernel(q_ref, k_ref, v_ref, o_ref, *, tk, nk):
    q = q_ref[0]
    tq = q.shape[0]
    scale = (MLA_NOPE + MLA_ROPE) ** -0.5

    def body(kc, carry):
        m, l, acc = carry
        k0 = pl.multiple_of(kc * tk, tk)
        k = k_ref[0, pl.ds(k0, tk), :]
        v = v_ref[0, pl.ds(k0, tk), :]
        s = _dot_nt(q, k) * scale
        m_new = jnp.maximum(m, jnp.max(s, axis=-1, keepdims=True))
        alpha = jnp.exp(m - m_new)
        p = jnp.exp(s - m_new)
        l = alpha * l + jnp.sum(p, axis=-1, keepdims=True)
        acc = alpha * acc + _dot(p.astype(BF16), v)
        return m_new, l, acc

    m0 = jnp.full((tq, 1), -jnp.inf, F32)
    l0 = jnp.zeros((tq, 1), F32)
    a0 = jnp.zeros((tq, MLA_V), F32)
    m, l, acc = lax.fori_loop(0, nk, body, (m0, l0, a0))
    o_ref[...] = (acc / l).astype(o_ref.dtype)


def mla_mixer(c_q, c_kv, kr_pad, krr_pad, q_gain, kv_gain, w_uq, w_ukv, R=512, tq=512, tk=512):
    L = c_q.shape[0]
    H = MLA_HEADS
    half = MLA_ROPE // 2
    pos = jnp.arange(L, dtype=F32)
    inv_freq = ROPE_THETA ** (-jnp.arange(0, MLA_ROPE, 2, dtype=F32) / MLA_ROPE)
    ang = pos[:, None] * inv_freq[None, :]
    cos, sin = jnp.cos(ang), jnp.sin(ang)
    zpad = jnp.zeros((L, 128 - MLA_ROPE), F32)
    cos_t = jnp.concatenate([cos, cos, zpad], axis=1)
    sin_t = jnp.concatenate([-sin, sin, zpad], axis=1)
    wq = w_uq.reshape(MLA_Q_RANK, H, MLA_NOPE + MLA_ROPE)
    wqn = wq[:, :, :MLA_NOPE].reshape(MLA_Q_RANK, H * MLA_NOPE).astype(BF16)
    wpe = wq[:, :, MLA_NOPE:]
    zw = jnp.zeros((MLA_Q_RANK, H, 128 - MLA_ROPE), w_uq.dtype)
    wqp = jnp.concatenate([wpe, zw], axis=2).reshape(MLA_Q_RANK, H * 128).astype(BF16)
    wrot = jnp.concatenate([wpe[:, :, half:], wpe[:, :, :half], zw], axis=2)
    wqr = wrot.reshape(MLA_Q_RANK, H * 128).astype(BF16)
    wkv = w_ukv.astype(BF16)
    row = lambda w: pl.BlockSpec((R, w), lambda i: (i, 0))
    full = lambda a, b: pl.BlockSpec((a, b), lambda i: (0, 0))
    q, k, v = pl.pallas_call(
        _mla_prep_kernel,
        grid=(L // R,),
        in_specs=[row(MLA_Q_RANK), row(MLA_KV_RANK), row(128), row(128), row(128), row(128),
                  full(1, MLA_Q_RANK), full(1, MLA_KV_RANK),
                  full(MLA_Q_RANK, H * 128), full(MLA_Q_RANK, H * 128), full(MLA_Q_RANK, H * 128),
                  full(MLA_KV_RANK, H * 256)],
        out_specs=[pl.BlockSpec((H, R, 256), lambda i: (0, i, 0)),
                   pl.BlockSpec((H, R, 256), lambda i: (0, i, 0)),
                   pl.BlockSpec((H, R, 128), lambda i: (0, i, 0))],
        out_shape=[jax.ShapeDtypeStruct((H, L, 256), BF16),
                   jax.ShapeDtypeStruct((H, L, 256), BF16),
                   jax.ShapeDtypeStruct((H, L, 128), BF16)],
        compiler_params=_cp(("parallel",)),
        name="mla_prep",
    )(c_q, c_kv, kr_pad, krr_pad, cos_t, sin_t,
      q_gain.astype(F32).reshape(1, -1), kv_gain.astype(F32).reshape(1, -1), wqn, wqp, wqr, wkv)
    return pl.pallas_call(
        functools.partial(_mla_attn_kernel, tk=tk, nk=L // tk),
        grid=(H, L // tq),
        in_specs=[pl.BlockSpec((1, tq, 256), lambda h, i: (h, i, 0)),
                  pl.BlockSpec((1, L, 256), lambda h, i: (h, 0, 0)),
                  pl.BlockSpec((1, L, 128), lambda h, i: (h, 0, 0))],
        out_specs=pl.BlockSpec((tq, MLA_V), lambda h, i: (i, h)),
        out_shape=jax.ShapeDtypeStruct((L, H * MLA_V), BF16),
        compiler_params=_cp(("parallel", "parallel")),
        name="mla_attn",
    )(q, k, v)


MLA_VT_ROWS = MLA_V + 16
LOG2E = 1.4426950408889634


def _mla_prep_t_kernel(cq_ref, ckv_ref, kr_ref, krr_ref, cos_ref, sin_ref, cost_ref, sint_ref,
                       qg_ref, kvg_ref, wqn_ref, wqp_ref, wqr_ref, wk_ref, wv_ref,
                       qt_ref, k_ref, vt_ref):
    cq = cq_ref[...]
    qn = (cq * lax.rsqrt(jnp.mean(cq * cq, axis=-1, keepdims=True) + NORM_EPS) * qg_ref[...]).astype(BF16)
    ckv = ckv_ref[...]
    kn = (ckv * lax.rsqrt(jnp.mean(ckv * ckv, axis=-1, keepdims=True) + NORM_EPS) * kvg_ref[...]).astype(BF16)
    c = (MLA_NOPE + MLA_ROPE) ** -0.5 * LOG2E
    qt_nope = _dot_nt(wqn_ref[...], qn)
    qt_pe = _dot_nt(wqp_ref[...], qn)
    qt_rot = _dot_nt(wqr_ref[...], qn)
    cost = cost_ref[...]
    sint = sint_ref[...]
    k_nope = _dot(kn, wk_ref[...])
    vt = _dot_nt(wv_ref[...], kn)
    k_pe = (kr_ref[...] * cos_ref[...] + krr_ref[...] * sin_ref[...]).astype(BF16)
    ones = jnp.ones((MLA_VT_ROWS - MLA_V, vt.shape[1]), BF16)
    for h in range(MLA_HEADS):
        sl = slice(h * 128, (h + 1) * 128)
        qt_ref[h, 0:128, :] = (qt_nope[sl] * c).astype(BF16)
        qt_ref[h, 128:256, :] = ((qt_pe[sl] * cost + qt_rot[sl] * sint) * c).astype(BF16)
        k_ref[h, :, 0:128] = k_nope[:, sl].astype(BF16)
        k_ref[h, :, 128:256] = k_pe
        vt_ref[h, 0:MLA_V, :] = vt[sl].astype(BF16)
        vt_ref[h, MLA_V:MLA_VT_ROWS, :] = ones


def _mla_attn_t_kernel(qt_ref, k_ref, vt_ref, o_ref, s_scr, *, tk, nk, tsub):
    tq = qt_ref.shape[2]
    nsub = tq // tsub
    subs = range(nsub)
    qts = [qt_ref[0, :, j * tsub:(j + 1) * tsub] for j in subs]

    def put_scores(kc, slot):
        k0 = pl.multiple_of(kc * tk, tk)
        kblk = k_ref[0, pl.ds(k0, tk), :]
        for j in subs:
            s_scr[slot, j] = _dot(kblk, qts[j])

    def consume(kc, slot, ms, accs):
        k0 = pl.multiple_of(kc * tk, tk)
        vt = vt_ref[0, :, pl.ds(k0, tk)]
        ss = [s_scr[slot, j] for j in subs]
        m_new = [jnp.maximum(ms[j], jnp.max(ss[j], axis=0, keepdims=True)) for j in subs]
        ps = [jnp.exp2(ss[j] - m_new[j]).astype(BF16) for j in subs]
        pv = [_dot(vt, ps[j]) for j in subs]
        accs = [jnp.exp2(ms[j] - m_new[j]) * accs[j] + pv[j] for j in subs]
        return m_new, accs

    def body(kk, carry):
        ms, accs = carry
        kc = 2 * kk
        put_scores(kc + 1, 1)
        ms, accs = consume(kc, 0, list(ms), list(accs))
        put_scores(jnp.minimum(kc + 2, nk - 1), 0)
        ms, accs = consume(kc + 1, 1, ms, accs)
        return tuple(ms), tuple(accs)

    m0 = tuple(jnp.full((1, tsub), -jnp.inf, F32) for _ in subs)
    a0 = tuple(jnp.zeros((MLA_VT_ROWS, tsub), F32) for _ in subs)
    put_scores(0, 0)
    ms, accs = lax.fori_loop(0, nk // 2, body, (m0, a0))
    for j in subs:
        o = accs[j][0:MLA_V] / accs[j][MLA_V:MLA_V + 1]
        o_ref[j * tsub:(j + 1) * tsub, :] = o.T.astype(o_ref.dtype)


def mla_mixer_t(c_q, c_kv, kr_pad, krr_pad, q_gain, kv_gain, w_uq, w_ukv, R=512, tq=512, tk=512, tsub=256):
    L = c_q.shape[0]
    H = MLA_HEADS
    half = MLA_ROPE // 2
    pos = jnp.arange(L, dtype=F32)
    inv_freq = ROPE_THETA ** (-jnp.arange(0, MLA_ROPE, 2, dtype=F32) / MLA_ROPE)
    ang = pos[:, None] * inv_freq[None, :]
    cos, sin = jnp.cos(ang), jnp.sin(ang)
    zpad = jnp.zeros((L, 128 - MLA_ROPE), F32)
    cos_t = jnp.concatenate([cos, cos, zpad], axis=1)
    sin_t = jnp.concatenate([-sin, sin, zpad], axis=1)
    wq = w_uq.reshape(MLA_Q_RANK, H, MLA_NOPE + MLA_ROPE)
    wqn = wq[:, :, :MLA_NOPE].reshape(MLA_Q_RANK, H * MLA_NOPE).T.astype(BF16)
    wpe = wq[:, :, MLA_NOPE:]
    zw = jnp.zeros((MLA_Q_RANK, H, 128 - MLA_ROPE), w_uq.dtype)
    wqp = jnp.concatenate([wpe, zw], axis=2).reshape(MLA_Q_RANK, H * 128).T.astype(BF16)
    wrot = jnp.concatenate([wpe[:, :, half:], wpe[:, :, :half], zw], axis=2)
    wqr = wrot.reshape(MLA_Q_RANK, H * 128).T.astype(BF16)
    wkv = w_ukv.reshape(MLA_KV_RANK, H, MLA_NOPE + MLA_V)
    wk = wkv[:, :, :MLA_NOPE].reshape(MLA_KV_RANK, H * MLA_NOPE).astype(BF16)
    wv = wkv[:, :, MLA_NOPE:].reshape(MLA_KV_RANK, H * MLA_V).T.astype(BF16)
    row = lambda w: pl.BlockSpec((R, w), lambda i: (i, 0))
    col = lambda w: pl.BlockSpec((w, R), lambda i: (0, i))
    full = lambda a, b: pl.BlockSpec((a, b), lambda i: (0, 0))
    qt, k, vt = pl.pallas_call(
        _mla_prep_t_kernel,
        grid=(L // R,),
        in_specs=[row(MLA_Q_RANK), row(MLA_KV_RANK), row(128), row(128), row(128), row(128),
                  col(128), col(128),
                  full(1, MLA_Q_RANK), full(1, MLA_KV_RANK),
                  full(H * 128, MLA_Q_RANK), full(H * 128, MLA_Q_RANK), full(H * 128, MLA_Q_RANK),
                  full(MLA_KV_RANK, H * 128), full(H * 128, MLA_KV_RANK)],
        out_specs=[pl.BlockSpec((H, 256, R), lambda i: (0, 0, i)),
                   pl.BlockSpec((H, R, 256), lambda i: (0, i, 0)),
                   pl.BlockSpec((H, MLA_VT_ROWS, R), lambda i: (0, 0, i))],
        out_shape=[jax.ShapeDtypeStruct((H, 256, L), BF16),
                   jax.ShapeDtypeStruct((H, L, 256), BF16),
                   jax.ShapeDtypeStruct((H, MLA_VT_ROWS, L), BF16)],
        compiler_params=_cp(("parallel",)),
        name="mla_prep",
    )(c_q, c_kv, kr_pad, krr_pad, cos_t, sin_t, cos_t.T, sin_t.T,
      q_gain.astype(F32).reshape(1, -1), kv_gain.astype(F32).reshape(1, -1), wqn, wqp, wqr, wk, wv)
    return pl.pallas_call(
        functools.partial(_mla_attn_t_kernel, tk=tk, nk=L // tk, tsub=tsub),
        grid=(H, L // tq),
        in_specs=[pl.BlockSpec((1, 256, tq), lambda h, i: (h, 0, i)),
                  pl.BlockSpec((1, L, 256), lambda h, i: (h, 0, 0)),
                  pl.BlockSpec((1, MLA_VT_ROWS, L), lambda h, i: (h, 0, 0))],
        out_specs=pl.BlockSpec((tq, MLA_V), lambda h, i: (i, h)),
        out_shape=jax.ShapeDtypeStruct((L, H * MLA_V), BF16),
        scratch_shapes=[pltpu.VMEM((2, tq // tsub, tk, tsub), F32)],
        compiler_params=_cp(("parallel", "parallel")),
        name="mla_attn",
    )(qt, k, vt)


def _merge_kernel(h_ref, ya_ref, yb_ref, yc_ref, yd_ref, g0_ref, g1_ref, g2_ref, g3_ref, p_ref, o_ref):
    h = h_ref[...]
    acc = None
    for b, (y_ref, g_ref) in enumerate(((ya_ref, g0_ref), (yb_ref, g1_ref), (yc_ref, g2_ref), (yd_ref, g3_ref))):
        gate = _sigmoid(_dot(h, g_ref[...]))
        t = gate * _dot(y_ref[...], p_ref[b])
        acc = t if acc is None else acc + t
    o_ref[...] = acc.astype(o_ref.dtype)


def merge_branches(h, ys, w_gate, w_branch, tm=512, tn=256):
    L = h.shape[0]
    nj = D_MODEL // tn
    gspec = lambda b: pl.BlockSpec((D_MODEL, tn), lambda i, j: (0, b * nj + j))
    yspec = pl.BlockSpec((tm, BW), lambda i, j: (i, 0))
    return pl.pallas_call(
        _merge_kernel,
        grid=(L // tm, nj),
        in_specs=[pl.BlockSpec((tm, D_MODEL), lambda i, j: (i, 0)), yspec, yspec, yspec, yspec,
                  gspec(0), gspec(1), gspec(2), gspec(3),
                  pl.BlockSpec((4, BW, tn), lambda i, j: (0, 0, j))],
        out_specs=pl.BlockSpec((tm, tn), lambda i, j: (i, j)),
        out_shape=jax.ShapeDtypeStruct((L, D_MODEL), BF16),
        compiler_params=_cp(("parallel", "arbitrary")),
        name="merge",
    )(h, *ys, w_gate, w_gate, w_gate, w_gate, w_branch)


def _out_proj_kernel(x_ref, m_ref, w_ref, g_ref, o_ref):
    f = _dot(m_ref[...], w_ref[...])
    f = f * lax.rsqrt(jnp.mean(f * f, axis=-1, keepdims=True) + NORM_EPS) * g_ref[...]
    o_ref[...] = x_ref[...] + f


def out_proj(x, m, w_out, gain, tm=512):
    L = x.shape[0]
    return pl.pallas_call(
        _out_proj_kernel,
        grid=(L // tm,),
        in_specs=[pl.BlockSpec((tm, D_MODEL), lambda i: (i, 0)),
                  pl.BlockSpec((tm, D_MODEL), lambda i: (i, 0)),
                  pl.BlockSpec((D_MODEL, D_MODEL), lambda i: (0, 0)),
                  pl.BlockSpec((1, D_MODEL), lambda i: (0, 0))],
        out_specs=pl.BlockSpec((tm, D_MODEL), lambda i: (i, 0)),
        out_shape=jax.ShapeDtypeStruct((L, D_MODEL), F32),
        compiler_params=_cp(("parallel",)),
        name="out_proj",
    )(x, m, w_out, gain.reshape(1, D_MODEL))


def _mlp_kernel(x_ref, g1_ref, w1_ref, w2_ref, g2_ref, o_ref, h_scr, acc):
    j = pl.program_id(1)

    @pl.when(j == 0)
    def _():
        x = x_ref[...]
        ms = jnp.mean(x * x, axis=-1, keepdims=True)
        h_scr[...] = (x * lax.rsqrt(ms + NORM_EPS) * g1_ref[...]).astype(BF16)
        acc[...] = jnp.zeros_like(acc)

    a = _dot(h_scr[...], w1_ref[...])
    a = jnp.maximum(a, 0.0)
    a = (a * a).astype(BF16)
    acc[...] += _dot(a, w2_ref[...])

    @pl.when(j == pl.num_programs(1) - 1)
    def _():
        f = acc[...]
        f = f * lax.rsqrt(jnp.mean(f * f, axis=-1, keepdims=True) + NORM_EPS) * g2_ref[...]
        o_ref[...] = x_ref[...] + f


def mlp(x, g1, w1, w2, g2, tm=512, tf=512):
    L = x.shape[0]
    return pl.pallas_call(
        _mlp_kernel,
        grid=(L // tm, D_FF // tf),
        in_specs=[pl.BlockSpec((tm, D_MODEL), lambda i, j: (i, 0)),
                  pl.BlockSpec((1, D_MODEL), lambda i, j: (0, 0)),
                  pl.BlockSpec((D_MODEL, tf), lambda i, j: (0, j)),
                  pl.BlockSpec((tf, D_MODEL), lambda i, j: (j, 0)),
                  pl.BlockSpec((1, D_MODEL), lambda i, j: (0, 0))],
        out_specs=pl.BlockSpec((tm, D_MODEL), lambda i, j: (i, 0)),
        out_shape=jax.ShapeDtypeStruct((L, D_MODEL), F32),
        scratch_shapes=[pltpu.VMEM((tm, D_MODEL), BF16), pltpu.VMEM((tm, D_MODEL), F32)],
        compiler_params=_cp(("parallel", "arbitrary")),
        name="mlp",
    )(x, g1.reshape(1, D_MODEL), w1, w2, g2.reshape(1, D_MODEL))


def _mix_weight(w):
    half = MLA_ROPE // 2
    z = lambda n: jnp.zeros((D_MODEL, n), w.dtype)
    kr = w[:, _OFF_KR:_OFF_KR + MLA_ROPE]
    cols = [
        w[:, _OFF_QKV:_OFF_QKV + 3 * BW],
        w[:, _OFF_U:_OFF_U + BW],
        w[:, _OFF_Z:_OFF_Z + BW],
        w[:, _OFF_SWAQ:_OFF_SWAQ + BW],
        w[:, _OFF_CKV:_OFF_CKV + MLA_KV_RANK],
        w[:, _OFF_SWAKV:_OFF_SWAKV + 256],
        w[:, _OFF_CQ:_OFF_CQ + MLA_Q_RANK],
        kr, z(64),
        kr[:, half:], kr[:, :half], z(64),
        w[:, _OFF_BETA:_OFF_BETA + 16], z(112),
    ]
    return jnp.concatenate(cols, axis=1).astype(BF16)


def kernel(x, w_in, s5_lam_re, s5_lam_im, s5_log_step, s5_b_re, s5_b_im, s5_c_re, s5_c_im, s5_d, s5_w_glu, s5_b_glu, gdn_conv, gdn_a_log, gdn_dt_bias, gdn_o_gain, swa_sink, t5_bias, mla_q_gain, mla_kv_gain, mla_w_uq, mla_w_ukv, w_branch, w_out, mix_pre_gain, mix_post_gain, mlp_pre_gain, mlp_post_gain, w_mlp_in, w_mlp_out):
    B_, L, _ = x.shape
    outs = []
    for b in range(B_):
        xb = x[b].astype(F32)
        for l in range(DEPTH):
            w_mix = _mix_weight(w_in[l])
            w_gate = w_in[l][:, _OFF_GATE:].astype(BF16)
            proj, h = in_proj(xb, mix_pre_gain[l].astype(F32), w_mix)
            y_a = s5_mixer(proj[:, MIX_U:MIX_U + BW], s5_lam_re[l], s5_lam_im[l], s5_log_step[l],
                           s5_b_re[l], s5_b_im[l], s5_c_re[l], s5_c_im[l], s5_d[l], s5_w_glu[l],
                           s5_b_glu[l])
            y_b = gdn_mixer_fused(proj[:, MIX_QKV:MIX_QKV + 3 * BW], proj[:, MIX_Z:MIX_Z + BW],
                                  proj[:, MIX_SMALL:MIX_SMALL + 16], gdn_conv[l], gdn_a_log[l],
                                  gdn_dt_bias[l], gdn_o_gain[l])
            y_c = swa_mixer(proj[:, MIX_SWAQ:MIX_SWAQ + BW], proj[:, MIX_SWAKV:MIX_SWAKV + 256],
                            swa_sink[l], t5_bias)
            y_d = mla_mixer_t(proj[:, MIX_CQ:MIX_CQ + MLA_Q_RANK], proj[:, MIX_CKV:MIX_CKV + MLA_KV_RANK],
                              proj[:, MIX_KR:MIX_KR + 128], proj[:, MIX_KRROT:MIX_KRROT + 128],
                              mla_q_gain[l], mla_kv_gain[l], mla_w_uq[l], mla_w_ukv[l])
            merged = merge_branches(h, (y_a, y_b, y_c, y_d), w_gate, w_branch[l].astype(BF16))
            xb = out_proj(xb, merged, w_out[l].astype(BF16), mix_post_gain[l].astype(F32))
            xb = mlp(xb, mlp_pre_gain[l].astype(F32), w_mlp_in[l].astype(BF16),
                     w_mlp_out[l].astype(BF16), mlp_post_gain[l].astype(F32))
        outs.append(xb)
    return jnp.stack(outs).astype(x.dtype)
```

```python
import functools
import math

import numpy as np
import jax
import jax.numpy as jnp
from jax import lax
from jax.experimental import pallas as pl
from jax.experimental.pallas import tpu as pltpu

F32 = jnp.float32
BF16 = jnp.bfloat16

D_MODEL = 2048
DEPTH = 4
BW = 512
D_FF = 4 * D_MODEL
NORM_EPS = 1e-6

S5_GROUP = 16
S5_GROUPS = 32
S5_STATE = 64
S5_NSTATE = S5_GROUPS * S5_STATE
S5_HALF = S5_NSTATE // 2

GDN_HEADS = 4
GDN_DH = 128
GDN_CHUNK = 64

SWA_HEADS = 8
SWA_KV_HEADS = 2
SWA_DH = 64
SWA_BLOCK = 128
WINDOW = 128
T5_BUCKETS = 32
T5_MAX_DISTANCE = 128

MLA_HEADS = 4
MLA_Q_RANK = 384
MLA_KV_RANK = 512
MLA_NOPE = 128
MLA_ROPE = 64
MLA_V = 128
ROPE_THETA = 10000.0

_OFF_U = 0
_OFF_QKV = 512
_OFF_Z = 2048
_OFF_BETA = 2560
_OFF_DECAY = 2568
_OFF_SWAQ = 2576
_OFF_SWAKV = 3088
_OFF_CQ = 3344
_OFF_CKV = 3728
_OFF_KR = 4240
_OFF_GATE = 4304
D_IN = _OFF_GATE + 4 * D_MODEL

MIX_QKV = 0
MIX_U = 1536
MIX_Z = 2048
MIX_SWAQ = 2560
MIX_CKV = 3072
MIX_SWAKV = 3584
MIX_CQ = 3840
MIX_KR = 4224
MIX_KRROT = 4352
MIX_SMALL = 4480
N_MIX = 4608

VMEM_LIMIT = 56 * 1024 * 1024


def _cp(sem, limit=VMEM_LIMIT):
    return pltpu.CompilerParams(dimension_semantics=sem, vmem_limit_bytes=limit)


def _dot(a, b):
    return jnp.dot(a, b, preferred_element_type=F32)


def _dot_nt(a, b):
    return lax.dot_general(a, b, (((1,), (1,)), ((), ())), preferred_element_type=F32)


def _dot_tn(a, b):
    return lax.dot_general(a, b, (((0,), (0,)), ((), ())), preferred_element_type=F32)


def _sigmoid(x):
    return 1.0 / (1.0 + jnp.exp(-x))


def _softplus(x):
    return jnp.maximum(x, 0.0) + jnp.log(1.0 + jnp.exp(-jnp.abs(x)))


def _in_proj_kernel(x_ref, g_ref, w_ref, o_ref, h_ref, h_scr):
    @pl.when(pl.program_id(1) == 0)
    def _():
        x = x_ref[...]
        ms = jnp.mean(x * x, axis=-1, keepdims=True)
        h = (x * lax.rsqrt(ms + NORM_EPS) * g_ref[...]).astype(BF16)
        h_scr[...] = h
        h_ref[...] = h

    o_ref[...] = _dot(h_scr[...], w_ref[...])


def in_proj(x, gain, w_mix, tm=512, tn=512):
    L = x.shape[0]
    n = w_mix.shape[1]
    return pl.pallas_call(
        _in_proj_kernel,
        grid=(L // tm, n // tn),
        in_specs=[
            pl.BlockSpec((tm, D_MODEL), lambda i, j: (i, 0)),
            pl.BlockSpec((1, D_MODEL), lambda i, j: (0, 0)),
            pl.BlockSpec((D_MODEL, tn), lambda i, j: (0, j)),
        ],
        out_specs=[
            pl.BlockSpec((tm, tn), lambda i, j: (i, j)),
            pl.BlockSpec((tm, D_MODEL), lambda i, j: (i, 0)),
        ],
        out_shape=[
            jax.ShapeDtypeStruct((L, n), F32),
            jax.ShapeDtypeStruct((L, D_MODEL), BF16),
        ],
        scratch_shapes=[pltpu.VMEM((tm, D_MODEL), BF16)],
        compiler_params=_cp(("parallel", "arbitrary")),
        name="in_proj",
    )(x, gain.reshape(1, D_MODEL), w_mix)


def _s5_kernel(*refs, reverse, finalize, T):
    if finalize:
        (u_ref, bd_ref, cd_ref, hs_ref, p_ref, yb_ref, dsk_ref, wglu_ref, bglu_ref,
         o_ref, bur, bui, carry) = refs
    else:
        u_ref, bd_ref, cd_ref, hs_ref, p_ref, o_ref, bur, bui, carry = refs
    nt = T // 8

    @pl.when(pl.program_id(0) == 0)
    def _():
        carry[...] = jnp.zeros_like(carry)

    u = u_ref[...]
    ub = u.astype(BF16)
    for k in range(2):
        r = _dot(ub[:, 256 * k:256 * (k + 1)], bd_ref[k])
        bur[:, S5_HALF * k:S5_HALF * (k + 1)] = r[:, :S5_HALF]
        bui[:, S5_HALF * k:S5_HALF * (k + 1)] = r[:, S5_HALF:]

    def tile(kk, c):
        cr, ci = c
        idx = (nt - 1 - kk) if reverse else kk
        t0 = pl.multiple_of(idx * 8, 8)
        xr = bur[pl.ds(t0, 8), :]
        xi = bui[pl.ds(t0, 8), :]
        for s, d in enumerate((1, 2, 4)):
            sh = (8 - d) if reverse else d
            sr = pltpu.roll(xr, sh, 0)
            si = pltpu.roll(xi, sh, 0)
            ar = hs_ref[2 * s]
            ai = hs_ref[2 * s + 1]
            xr, xi = xr + ar * sr - ai * si, xi + ar * si + ai * sr
        pr = p_ref[0]
        pi = p_ref[1]
        xr, xi = xr + pr * cr - pi * ci, xi + pr * ci + pi * cr
        bur[pl.ds(t0, 8), :] = xr
        bui[pl.ds(t0, 8), :] = xi
        row = 0 if reverse else 7
        return (jnp.broadcast_to(xr[row:row + 1, :], (8, S5_NSTATE)),
                jnp.broadcast_to(xi[row:row + 1, :], (8, S5_NSTATE)))

    cr, ci = lax.fori_loop(0, nt, tile, (carry[0], carry[1]))
    carry[0] = cr
    carry[1] = ci

    ys = []
    for k in range(2):
        sr = bur[:, S5_HALF * k:S5_HALF * (k + 1)].astype(BF16)
        si = bui[:, S5_HALF * k:S5_HALF * (k + 1)].astype(BF16)
        ys.append(_dot(sr, cd_ref[k, 0]) + _dot(si, cd_ref[k, 1]))
    y = jnp.concatenate(ys, axis=1)
    if finalize:
        y = y + yb_ref[...] + dsk_ref[...] * u
        y = jax.nn.gelu(y)
        gate = _sigmoid(_dot(y.astype(BF16), wglu_ref[...]) + bglu_ref[...])
        o_ref[...] = (y * gate).astype(o_ref.dtype)
    else:
        o_ref[...] = y


def s5_direction_call(u, bd, cd, hs, pw, *, reverse, fin=None, T=256, ucol=0):
    L = u.shape[0]
    nb = L // T
    rmap = (lambda i: (nb - 1 - i, 0)) if reverse else (lambda i: (i, 0))
    umap = (lambda i: (nb - 1 - i, ucol)) if reverse else (lambda i: (i, ucol))
    c3 = lambda i: (0, 0, 0)
    in_specs = [
        pl.BlockSpec((T, BW), umap),
        pl.BlockSpec((2, 256, S5_NSTATE), c3),
        pl.BlockSpec((2, 2, S5_HALF, 256), lambda i: (0, 0, 0, 0)),
        pl.BlockSpec((6, 8, S5_NSTATE), c3),
        pl.BlockSpec((2, 8, S5_NSTATE), c3),
    ]
    args = [u, bd, cd, hs, pw]
    if fin is not None:
        yb, dsk, wglu, bglu = fin
        in_specs += [
            pl.BlockSpec((T, BW), rmap),
            pl.BlockSpec((1, BW), lambda i: (0, 0)),
            pl.BlockSpec((BW, BW), lambda i: (0, 0)),
            pl.BlockSpec((1, BW), lambda i: (0, 0)),
        ]
        args += [yb, dsk.reshape(1, BW), wglu, bglu.reshape(1, BW)]
    return pl.pallas_call(
        functools.partial(_s5_kernel, reverse=reverse, finalize=fin is not None, T=T),
        grid=(nb,),
        in_specs=in_specs,
        out_specs=pl.BlockSpec((T, BW), rmap),
        out_shape=jax.ShapeDtypeStruct((L, BW), BF16 if fin is not None else F32),
        scratch_shapes=[
            pltpu.VMEM((T, S5_NSTATE), F32),
            pltpu.VMEM((T, S5_NSTATE), F32),
            pltpu.VMEM((2, 8, S5_NSTATE), F32),
        ],
        compiler_params=_cp(("arbitrary",)),
        name="s5_rev" if reverse else "s5_fwd",
    )(*args)


def _s5_prep(lam_re, lam_im, log_step, b_re, b_im, c_re, c_im, reverse):
    G, P, Hg = S5_GROUPS, S5_STATE, S5_GROUP
    lam_re = jnp.minimum(lam_re.astype(F32), -1e-4)
    lam_im = lam_im.astype(F32)
    dt = jnp.exp(log_step.astype(F32))[:, None]
    mag = jnp.exp(lam_re * dt)
    abar_r = mag * jnp.cos(lam_im * dt)
    abar_i = mag * jnp.sin(lam_im * dt)
    den = lam_re * lam_re + lam_im * lam_im
    xr = abar_r - 1.0
    xi = abar_i
    coef_r = (xr * lam_re + xi * lam_im) / den
    coef_i = (xi * lam_re - xr * lam_im) / den
    b_re = b_re.astype(F32)
    b_im = b_im.astype(F32)
    bbar_r = coef_r[..., None] * b_re - coef_i[..., None] * b_im
    bbar_i = coef_r[..., None] * b_im + coef_i[..., None] * b_re

    eye = jnp.eye(16, dtype=F32)

    def bdiag_in(bb):
        blk = bb.transpose(0, 2, 1).reshape(2, 16, Hg, P)
        return jnp.einsum('gq,kghp->kghqp', eye, blk).reshape(2, 16 * Hg, 16 * P)

    bd = jnp.concatenate([bdiag_in(bbar_r), bdiag_in(bbar_i)], axis=-1).astype(BF16)

    def bdiag_out(cc):
        blk = cc.astype(F32).transpose(0, 2, 1).reshape(2, 16, P, Hg)
        return jnp.einsum('gq,kgph->kgpqh', eye, blk).reshape(2, 16 * P, 16 * Hg)

    cd = jnp.stack([bdiag_out(c_re), -bdiag_out(c_im)], axis=1).astype(BF16)

    ar = abar_r.reshape(-1)
    ai = abar_i.reshape(-1)

    def cm(a, b):
        return a[0] * b[0] - a[1] * b[1], a[0] * b[1] + a[1] * b[0]

    a1 = (ar, ai)
    a2 = cm(a1, a1)
    a3 = cm(a2, a1)
    a4 = cm(a2, a2)
    a5 = cm(a4, a1)
    a6 = cm(a4, a2)
    a7 = cm(a4, a3)
    a8 = cm(a4, a4)
    pows = [a1, a2, a3, a4, a5, a6, a7, a8]
    rows = np.arange(8)[:, None]
    hs = []
    for d, ad in ((1, a1), (2, a2), (4, a4)):
        mask = jnp.asarray((rows <= 7 - d) if reverse else (rows >= d), F32)
        hs.append(mask * ad[0][None, :])
        hs.append(mask * ad[1][None, :])
    hs = jnp.stack(hs)
    order = list(range(7, -1, -1)) if reverse else list(range(8))
    pw = jnp.stack([jnp.stack([pows[t][0] for t in order]),
                    jnp.stack([pows[t][1] for t in order])])
    return bd, cd, hs, pw


def s5_mixer(u, lam_re, lam_im, log_step, b_re, b_im, c_re, c_im, d_skip, w_glu, b_glu, T=256, ucol=0):
    pf = _s5_prep(lam_re[0], lam_im[0], log_step[0], b_re[0], b_im[0], c_re[0], c_im[0], False)
    pb = _s5_prep(lam_re[1], lam_im[1], log_step[1], b_re[1], b_im[1], c_re[1], c_im[1], True)
    yb = s5_direction_call(u, *pb, reverse=True, T=T, ucol=ucol)
    return s5_direction_call(u, *pf, reverse=False, T=T, ucol=ucol,
                             fin=(yb, d_skip.astype(F32), w_glu.astype(BF16), b_glu.astype(F32)))


def _gdn_kernel(*refs, reverse, finalize, R, nb):
    if finalize:
        (qkv_ref, prev_ref, next_ref, cw_ref, sm_ref, smt_ref, gpr_ref, gpc_ref,
         ob_ref, z_ref, og_ref, o_ref, s_scr, q_scr, k_scr, v_scr, g_scr, b_scr, o_scr) = refs
    else:
        (qkv_ref, prev_ref, next_ref, cw_ref, sm_ref, smt_ref, gpr_ref, gpc_ref,
         o_ref, s_scr, q_scr, k_scr, v_scr, g_scr, b_scr, o_scr) = refs
    C = GDN_CHUNK
    nc = R // C
    i = pl.program_id(0)

    @pl.when(i == 0)
    def _():
        s_scr[...] = jnp.zeros_like(s_scr)

    blk = (nb - 1 - i) if reverse else i
    x = qkv_ref[...]
    pv = jnp.where(blk > 0, prev_ref[...], 0.0)
    nx = jnp.where(blk < nb - 1, next_ref[...], 0.0)
    xp = jnp.concatenate([pv, x, nx], axis=0)
    w = cw_ref[...]
    conv = (w[0:1] * xp[6:6 + R] + w[1:2] * xp[7:7 + R]
            + w[2:3] * xp[8:8 + R] + w[3:4] * xp[9:9 + R])
    act = conv * _sigmoid(conv)
    for h in range(GDN_HEADS):
        sl = slice(h * GDN_DH, (h + 1) * GDN_DH)
        qh = act[:, sl]
        q_scr[:, sl] = qh * lax.rsqrt(jnp.sum(qh * qh, axis=-1, keepdims=True) + 1e-6) * (GDN_DH ** -0.5)
        kh = act[:, BW + h * GDN_DH:BW + (h + 1) * GDN_DH]
        k_scr[:, sl] = kh * lax.rsqrt(jnp.sum(kh * kh, axis=-1, keepdims=True) + 1e-6)
    v_scr[...] = act[:, 2 * BW:3 * BW]
    sm = sm_ref[...]
    b_scr[...] = _sigmoid(sm[:, 0:4])
    g_scr[...] = gpr_ref[0:1, :] * _softplus(sm[:, 4:8] + gpr_ref[1:2, :])

    ii = lax.broadcasted_iota(jnp.int32, (C, C), 0)
    jj = lax.broadcasted_iota(jnp.int32, (C, C), 1)
    if reverse:
        incl = jj >= ii
        strict = jj > ii
    else:
        incl = jj <= ii
        strict = jj < ii
    tri_l = jnp.where(incl, 1.0, 0.0).astype(F32)
    tri_u = jnp.where((ii >= jj) if reverse else (ii <= jj), 1.0, 0.0).astype(F32)
    last = 0 if reverse else C - 1
    eye = jnp.where(ii == jj, 1.0, 0.0).astype(F32)
    pair_masks = []
    for lvl in range(6):
        bi = ii >> lvl
        bj = jj >> lvl
        lo_, hi_ = (bi, bj) if reverse else (bj, bi)
        pair_masks.append(((ii >> (lvl + 1)) == (jj >> (lvl + 1))) & ((hi_ & 1) == 1) & ((lo_ & 1) == 0))

    def chunk(cc, carry):
        cidx = (nc - 1 - cc) if reverse else cc
        r0 = pl.multiple_of(cidx * C, C)
        gch = g_scr[pl.ds(r0, C), :]
        bch = b_scr[pl.ds(r0, C), :]
        gt = gpc_ref[:, 0:1] * _softplus(smt_ref[cidx][4:8, :] + gpc_ref[:, 1:2])
        gc_col = jnp.dot(tri_l, gch, preferred_element_type=F32, precision=lax.Precision.HIGHEST)
        gc_row = jnp.dot(gt, tri_u, preferred_element_type=F32, precision=lax.Precision.HIGHEST)
        for h in range(GDN_HEADS):
            sl = slice(h * GDN_DH, (h + 1) * GDN_DH)
            gcol = gc_col[:, h:h + 1]
            grow = gc_row[h:h + 1, :]
            glast = gc_col[last:last + 1, h:h + 1]
            decay = jnp.exp(jnp.where(incl, gcol - grow, -jnp.inf))
            k = k_scr[pl.ds(r0, C), sl]
            q = q_scr[pl.ds(r0, C), sl]
            v = v_scr[pl.ds(r0, C), sl]
            beta = bch[:, h:h + 1]
            kb = k * beta
            kbf = k.astype(BF16)
            kk = _dot_nt(kb.astype(BF16), kbf)
            qk = _dot_nt(q.astype(BF16), kbf)
            a = jnp.where(strict, kk * decay, 0.0)
            attn = qk * decay
            xs = jnp.concatenate([v * beta, kb * jnp.exp(gcol)], axis=1)
            minv = eye - jnp.where(pair_masks[0], a, 0.0)
            for lvl in range(1, 6):
                mb = minv.astype(BF16)
                em = _dot(jnp.where(pair_masks[lvl], a, 0.0).astype(BF16), mb)
                minv = minv - _dot(mb, em.astype(BF16))
            xs = _dot(minv.astype(BF16), xs.astype(BF16))
            u_ = xs[:, :GDN_DH]
            w_ = xs[:, GDN_DH:]
            s_h = s_scr[h]
            sb = s_h.astype(BF16)
            v_new = u_ - _dot(w_.astype(BF16), sb)
            vnb = v_new.astype(BF16)
            o = _dot((q * jnp.exp(gcol)).astype(BF16), sb) + _dot(attn.astype(BF16), vnb)
            kdec = (k * jnp.exp(glast - gcol)).astype(BF16)
            s_scr[h] = s_h * jnp.exp(glast) + _dot_tn(kdec, vnb)
            o_scr[pl.ds(r0, C), sl] = o
        return carry

    lax.fori_loop(0, nc, chunk, 0)

    if finalize:
        o = o_scr[...] + ob_ref[...]
        z = z_ref[...]
        og = og_ref[...]
        for h in range(GDN_HEADS):
            sl = slice(h * GDN_DH, (h + 1) * GDN_DH)
            oh = o[:, sl]
            oh = oh * lax.rsqrt(jnp.mean(oh * oh, axis=-1, keepdims=True) + NORM_EPS) * og
            zh = z[:, sl]
            o_ref[:, sl] = (oh * (zh * _sigmoid(zh))).astype(o_ref.dtype)
    else:
        o_ref[...] = o_scr[...]


def gdn_direction_call(qkv, conv_w, smd, smdt, gpr, gpc, *, reverse, fin=None, R=256):
    L = qkv.shape[0]
    nb = L // R
    r8 = R // 8
    nc = R // GDN_CHUNK
    bmap = (lambda i: nb - 1 - i) if reverse else (lambda i: i)
    in_specs = [
        pl.BlockSpec((R, 3 * BW), lambda i: (bmap(i), 0)),
        pl.BlockSpec((8, 3 * BW), lambda i: (jnp.maximum(bmap(i) * r8 - 1, 0), 0)),
        pl.BlockSpec((8, 3 * BW), lambda i: (jnp.minimum((bmap(i) + 1) * r8, L // 8 - 1), 0)),
        pl.BlockSpec((4, 3 * BW), lambda i: (0, 0)),
        pl.BlockSpec((R, 8), lambda i: (bmap(i), 0)),
        pl.BlockSpec((nc, 8, GDN_CHUNK), lambda i: (bmap(i), 0, 0)),
        pl.BlockSpec((2, 4), lambda i: (0, 0)),
        pl.BlockSpec((4, 2), lambda i: (0, 0)),
    ]
    args = [qkv, qkv, qkv, conv_w, smd, smdt, gpr, gpc]
    if fin is not None:
        ob, z, og = fin
        in_specs += [
            pl.BlockSpec((R, BW), lambda i: (bmap(i), 0)),
            pl.BlockSpec((R, BW), lambda i: (bmap(i), 0)),
            pl.BlockSpec((1, GDN_DH), lambda i: (0, 0)),
        ]
        args += [ob, z, og.reshape(1, GDN_DH)]
    return pl.pallas_call(
        functools.partial(_gdn_kernel, reverse=reverse, finalize=fin is not None, R=R, nb=nb),
        grid=(nb,),
        in_specs=in_specs,
        out_specs=pl.BlockSpec((R, BW), lambda i: (bmap(i), 0)),
        out_shape=jax.ShapeDtypeStruct((L, BW), BF16 if fin is not None else F32),
        scratch_shapes=[
            pltpu.VMEM((GDN_HEADS, GDN_DH, GDN_DH), F32),
            pltpu.VMEM((R, BW), F32),
            pltpu.VMEM((R, BW), F32),
            pltpu.VMEM((R, BW), F32),
            pltpu.VMEM((R, 4), F32),
            pltpu.VMEM((R, 4), F32),
            pltpu.VMEM((R, BW), F32),
        ],
        compiler_params=_cp(("arbitrary",)),
        name="gdn_rev" if reverse else "gdn_fwd",
    )(*args)


def gdn_mixer(qkv, z, small, conv_w, a_log, dt_bias, o_gain, R=256):
    L = qkv.shape[0]
    coef = -jnp.exp(a_log.astype(F32))
    dtb = dt_bias.astype(F32)
    outs = []
    o_b = None
    for d in (1, 0):
        smd = jnp.concatenate([small[:, 4 * d:4 * d + 4], small[:, 8 + 4 * d:12 + 4 * d]], axis=1)
        smdt = smd.reshape(L // GDN_CHUNK, GDN_CHUNK, 8).transpose(0, 2, 1)
        gpr = jnp.stack([coef[d], dtb[d]])
        gpc = gpr.T
        if d == 1:
            o_b = gdn_direction_call(qkv, conv_w.astype(F32), smd, smdt, gpr, gpc, reverse=True, R=R)
        else:
            return gdn_direction_call(qkv, conv_w.astype(F32), smd, smdt, gpr, gpc, reverse=False,
                                      fin=(o_b, z, o_gain.astype(F32)), R=R)


GDN_HC = GDN_HEADS * GDN_CHUNK
GDN_UNROLL = 2


def _stack_heads(ref, r0):
    return jnp.concatenate(
        [ref[pl.ds(r0, GDN_CHUNK), h * GDN_DH:(h + 1) * GDN_DH] for h in range(GDN_HEADS)], axis=0)


def _gdn_prologue(d, blk, nb, R, qkv_ref, prev_ref, next_ref, cw_ref, sm_ref, gt_ref, gpr_ref, gpc_ref,
                  q_scr, k_scr, v_scr, b_scr, gc_scr, gr_scr, tri):
    x = qkv_ref[...]
    pv = jnp.where(blk > 0, prev_ref[...], 0.0)
    nx = jnp.where(blk < nb - 1, next_ref[...], 0.0)
    xp = jnp.concatenate([pv, x, nx], axis=0)
    w = cw_ref[...]
    conv = (w[0:1] * xp[6:6 + R] + w[1:2] * xp[7:7 + R]
            + w[2:3] * xp[8:8 + R] + w[3:4] * xp[9:9 + R])
    act = conv * _sigmoid(conv)
    for h in range(GDN_HEADS):
        sl = slice(h * GDN_DH, (h + 1) * GDN_DH)
        qh = act[:, sl]
        q_scr[d, :, sl] = qh * lax.rsqrt(jnp.sum(qh * qh, axis=-1, keepdims=True) + 1e-6) * (GDN_DH ** -0.5)
        kh = act[:, BW + h * GDN_DH:BW + (h + 1) * GDN_DH]
        k_scr[d, :, sl] = kh * lax.rsqrt(jnp.sum(kh * kh, axis=-1, keepdims=True) + 1e-6)
    v_scr[d] = act[:, 2 * BW:3 * BW]
    sm = sm_ref[...]
    b_scr[d] = _sigmoid(sm[:, 0:4])
    g = gpr_ref[2 * d:2 * d + 1, :] * _softplus(sm[:, 4:8] + gpr_ref[2 * d + 1:2 * d + 2, :])
    gt = gpc_ref[:, 2 * d:2 * d + 1] * _softplus(gt_ref[...] + gpc_ref[:, 2 * d + 1:2 * d + 2])
    tri_l, tri_u = tri
    gc_scr[d] = jnp.dot(tri_l, g, preferred_element_type=F32, precision=lax.Precision.HIGHEST)
    gr = jnp.dot(gt, tri_u, preferred_element_type=F32, precision=lax.Precision.HIGHEST)
    for c in range(R // GDN_CHUNK):
        gr_scr[d, c] = gr[:, c * GDN_CHUNK:(c + 1) * GDN_CHUNK]


def _gdn_chunks(insts, m_scr, q_scr, k_scr, v_scr, b_scr, gc_scr, gr_scr, s_scr):
    C = GDN_CHUNK
    H = GDN_HEADS
    n = len(insts)
    rng = range(n)
    ds = [t[0] for t in insts]
    r0s = [pl.multiple_of(t[1] * C, C) for t in insts]
    lasts = [0 if t[2] else C - 1 for t in insts]
    ks = [_stack_heads(k_scr.at[ds[i]], r0s[i]) for i in rng]
    qs = [_stack_heads(q_scr.at[ds[i]], r0s[i]) for i in rng]
    vs = [_stack_heads(v_scr.at[ds[i]], r0s[i]) for i in rng]
    bch = [b_scr[ds[i], pl.ds(r0s[i], C), :] for i in rng]
    gch = [gc_scr[ds[i], pl.ds(r0s[i], C), :] for i in rng]
    grw = [gr_scr[ds[i], insts[i][1]] for i in rng]
    beta = [jnp.concatenate([b[:, h:h + 1] for h in range(H)], axis=0) for b in bch]
    gcol = [jnp.concatenate([g[:, h:h + 1] for h in range(H)], axis=0) for g in gch]
    grow = [jnp.concatenate([g[h:h + 1, :] for h in range(H)], axis=1) for g in grw]
    glast_r = [jnp.concatenate(
        [jnp.broadcast_to(grw[i][h:h + 1, lasts[i]:lasts[i] + 1], (1, C)) for h in range(H)], axis=1)
        for i in rng]
    e_cat = [jnp.concatenate(
        [jnp.broadcast_to(jnp.exp(gch[i][lasts[i]:lasts[i] + 1, h:h + 1]), (1, GDN_DH)) for h in range(H)],
        axis=1) for i in rng]

    kst = [k.T for k in ks]
    kstb = [k.astype(BF16) for k in kst]
    kbs = [ks[i] * beta[i] for i in rng]
    g2 = [_dot(jnp.concatenate([kbs[i], qs[i]], axis=0).astype(BF16), kstb[i]) for i in rng]
    decay = [jnp.exp((gcol[i] - grow[i]) + m_scr[ds[i], 0]) for i in rng]
    a = [g2[i][:GDN_HC] * decay[i] * m_scr[ds[i], 1] for i in rng]
    attn = [(g2[i][GDN_HC:] * decay[i]).astype(BF16) for i in rng]
    minv = [m_scr[ds[i], 8] - a[i] * m_scr[ds[i], 2] for i in rng]
    for lvl in range(1, 6):
        mb = [m.astype(BF16) for m in minv]
        em = [_dot((a[i] * m_scr[ds[i], 2 + lvl]).astype(BF16), mb[i]).astype(BF16) for i in rng]
        minv = [minv[i] - _dot(mb[i], em[i]) for i in rng]
    egc = [jnp.exp(g) for g in gcol]
    rhs = [jnp.concatenate([vs[i] * beta[i], kbs[i] * egc[i]], axis=1).astype(BF16) for i in rng]
    xs = [_dot(minv[i].astype(BF16), rhs[i]) for i in rng]
    qg = [qs[i] * egc[i] for i in rng]
    wq = [jnp.concatenate([xs[i][:, GDN_DH:], qg[i]], axis=0).astype(BF16) for i in rng]
    kdect = [(kst[i] * jnp.exp(glast_r[i] - grow[i])).astype(BF16) for i in rng]

    zero = jnp.zeros((C, GDN_DH), BF16)
    by_dir = {}
    for i in rng:
        by_dir.setdefault(ds[i], []).append(i)
    depth = max(len(v) for v in by_dir.values())
    s_cat = {d: s_scr[d] for d in by_dir}
    for step in range(depth):
        act = [v[step] for v in by_dir.values() if len(v) > step]
        p = [_dot(wq[i], s_cat[ds[i]].astype(BF16)) for i in act]
        vn = []
        qss = []
        for j, i in enumerate(act):
            ws = jnp.concatenate(
                [p[j][h * C:(h + 1) * C, h * GDN_DH:(h + 1) * GDN_DH] for h in range(H)], axis=0)
            qss.append(jnp.concatenate(
                [p[j][GDN_HC + h * C:GDN_HC + (h + 1) * C, h * GDN_DH:(h + 1) * GDN_DH] for h in range(H)],
                axis=0))
            vn.append((xs[i][:, :GDN_DH] - ws).astype(BF16))
        o = [qss[j] + _dot(attn[i], vn[j]) for j, i in enumerate(act)]
        for j, i in enumerate(act):
            vn_bd = jnp.concatenate(
                [jnp.concatenate([vn[j][h * C:(h + 1) * C] if hh == h else zero for hh in range(H)], axis=1)
                 for h in range(H)], axis=0)
            s_cat[ds[i]] = s_cat[ds[i]] * e_cat[i] + _dot(kdect[i], vn_bd)
            o_ref = insts[i][3]
            for h in range(H):
                o_ref[pl.ds(r0s[i], C), h * GDN_DH:(h + 1) * GDN_DH] = o[j][h * C:(h + 1) * C]
    for d in by_dir:
        s_scr[d] = s_cat[d]


def _gdn_bidir_kernel(qf_ref, pf_ref, nf_ref, qb_ref, pb_ref, nb_ref, cw_ref, smf_ref, smb_ref,
                      gtf_ref, gtb_ref, gpr_ref, gpc_ref, of_ref, ob_ref,
                      s_scr, q_scr, k_scr, v_scr, b_scr, gc_scr, gr_scr, m_scr, *, R, nb):
    C = GDN_CHUNK
    nc = R // C
    i = pl.program_id(0)

    @pl.when(i == 0)
    def _():
        s_scr[...] = jnp.zeros_like(s_scr)
        ii = lax.broadcasted_iota(jnp.int32, (GDN_HC, GDN_HC), 0)
        jj = lax.broadcasted_iota(jnp.int32, (GDN_HC, GDN_HC), 1)
        blockd = (ii // C) == (jj // C)
        for d, reverse in enumerate((False, True)):
            incl = blockd & ((jj >= ii) if reverse else (jj <= ii))
            strict = blockd & ((jj > ii) if reverse else (jj < ii))
            m_scr[d, 0] = jnp.where(incl, 0.0, -jnp.inf).astype(F32)
            m_scr[d, 1] = jnp.where(strict, 1.0, 0.0).astype(F32)
            for lvl in range(6):
                bi = ii >> lvl
                bj = jj >> lvl
                lo_, hi_ = (bi, bj) if reverse else (bj, bi)
                pm = ((ii >> (lvl + 1)) == (jj >> (lvl + 1))) & ((hi_ & 1) == 1) & ((lo_ & 1) == 0)
                m_scr[d, 2 + lvl] = jnp.where(pm, 1.0, 0.0).astype(F32)
            m_scr[d, 8] = jnp.where(ii == jj, 1.0, 0.0).astype(F32)

    ri = lax.broadcasted_iota(jnp.int32, (R, R), 0)
    rj = lax.broadcasted_iota(jnp.int32, (R, R), 1)
    same = (ri // C) == (rj // C)
    tris = []
    for reverse in (False, True):
        tl = jnp.where(same & ((rj >= ri) if reverse else (rj <= ri)), 1.0, 0.0).astype(F32)
        tu = jnp.where(same & ((ri >= rj) if reverse else (ri <= rj)), 1.0, 0.0).astype(F32)
        tris.append((tl, tu))
    _gdn_prologue(0, i, nb, R, qf_ref, pf_ref, nf_ref, cw_ref, smf_ref, gtf_ref, gpr_ref, gpc_ref,
                  q_scr, k_scr, v_scr, b_scr, gc_scr, gr_scr, tris[0])
    _gdn_prologue(1, nb - 1 - i, nb, R, qb_ref, pb_ref, nb_ref, cw_ref, smb_ref, gtb_ref, gpr_ref, gpc_ref,
                  q_scr, k_scr, v_scr, b_scr, gc_scr, gr_scr, tris[1])

    def chunk(cc, carry):
        insts = []
        for uu in range(GDN_UNROLL):
            c = cc * GDN_UNROLL + uu
            insts.append((0, c, False, of_ref))
            insts.append((1, nc - 1 - c, True, ob_ref))
        _gdn_chunks(insts, m_scr, q_scr, k_scr, v_scr, b_scr, gc_scr, gr_scr, s_scr)
        return carry

    lax.fori_loop(0, nc // GDN_UNROLL, chunk, 0)


def _gdn_final_kernel(of_ref, ob_ref, z_ref, og_ref, o_ref):
    o = of_ref[...] + ob_ref[...]
    z = z_ref[...]
    og = og_ref[...]
    for h in range(GDN_HEADS):
        sl = slice(h * GDN_DH, (h + 1) * GDN_DH)
        oh = o[:, sl]
        oh = oh * lax.rsqrt(jnp.mean(oh * oh, axis=-1, keepdims=True) + NORM_EPS) * og
        zh = z[:, sl]
        o_ref[:, sl] = (oh * (zh * _sigmoid(zh))).astype(o_ref.dtype)


def gdn_mixer_fused(qkv, z, small, conv_w, a_log, dt_bias, o_gain, R=256, RF=512, zcol=0):
    L = qkv.shape[0]
    nb = L // R
    r8 = R // 8
    coef = -jnp.exp(a_log.astype(F32))
    dtb = dt_bias.astype(F32)
    gpr = jnp.stack([coef[0], dtb[0], coef[1], dtb[1]])
    gpc = gpr.T
    smf = jnp.concatenate([small[:, 0:4], small[:, 8:12]], axis=1)
    smb = jnp.concatenate([small[:, 4:8], small[:, 12:16]], axis=1)
    gtf = small[:, 8:12].T
    gtb = small[:, 12:16].T
    fmap = lambda i: i
    bmap = lambda i: nb - 1 - i
    blk = lambda m: pl.BlockSpec((R, 3 * BW), lambda i: (m(i), 0))
    prv = lambda m: pl.BlockSpec((8, 3 * BW), lambda i: (jnp.maximum(m(i) * r8 - 1, 0), 0))
    nxt = lambda m: pl.BlockSpec((8, 3 * BW), lambda i: (jnp.minimum((m(i) + 1) * r8, L // 8 - 1), 0))
    full = lambda a, b: pl.BlockSpec((a, b), lambda i: (0, 0))
    o_f, o_b = pl.pallas_call(
        functools.partial(_gdn_bidir_kernel, R=R, nb=nb),
        grid=(nb,),
        in_specs=[blk(fmap), prv(fmap), nxt(fmap), blk(bmap), prv(bmap), nxt(bmap),
                  full(4, 3 * BW),
                  pl.BlockSpec((R, 8), lambda i: (i, 0)),
                  pl.BlockSpec((R, 8), lambda i: (nb - 1 - i, 0)),
                  pl.BlockSpec((4, R), lambda i: (0, i)),
                  pl.BlockSpec((4, R), lambda i: (0, nb - 1 - i)),
                  full(4, 4), full(4, 4)],
        out_specs=[pl.BlockSpec((R, BW), lambda i: (i, 0)),
                   pl.BlockSpec((R, BW), lambda i: (nb - 1 - i, 0))],
        out_shape=[jax.ShapeDtypeStruct((L, BW), F32), jax.ShapeDtypeStruct((L, BW), F32)],
        scratch_shapes=[
            pltpu.VMEM((2, GDN_DH, BW), F32),
            pltpu.VMEM((2, R, BW), F32),
            pltpu.VMEM((2, R, BW), F32),
            pltpu.VMEM((2, R, BW), F32),
            pltpu.VMEM((2, R, 4), F32),
            pltpu.VMEM((2, R, 4), F32),
            pltpu.VMEM((2, R // GDN_CHUNK, 4, GDN_CHUNK), F32),
            pltpu.VMEM((2, 9, GDN_HC, GDN_HC), F32),
        ],
        compiler_params=_cp(("arbitrary",)),
        name="gdn_bidir",
    )(qkv, qkv, qkv, qkv, qkv, qkv, conv_w.astype(F32), smf, smb, gtf, gtb, gpr, gpc)
    row = pl.BlockSpec((RF, BW), lambda i: (i, 0))
    zrow = pl.BlockSpec((RF, BW), lambda i: (i, zcol))
    return pl.pallas_call(
        _gdn_final_kernel,
        grid=(L // RF,),
        in_specs=[row, row, zrow, pl.BlockSpec((1, GDN_DH), lambda i: (0, 0))],
        out_specs=row,
        out_shape=jax.ShapeDtypeStruct((L, BW), BF16),
        compiler_params=_cp(("parallel",)),
        name="gdn_final",
    )(o_f, o_b, z, o_gain.astype(F32).reshape(1, GDN_DH))


def _t5_bucket(rel):
    nb = T5_BUCKETS // 2
    max_exact = nb // 2
    ret = jnp.where(rel > 0, nb, 0)
    n = jnp.abs(rel)
    nf = jnp.maximum(n, 1).astype(F32)
    large = max_exact + (jnp.log(nf / max_exact) / math.log(T5_MAX_DISTANCE / max_exact)
                         * (nb - max_exact)).astype(jnp.int32)
    large = jnp.minimum(large, nb - 1)
    return ret + jnp.where(n < max_exact, n, large)


def _swa_kernel(q_ref, kp_ref, kc_ref, kn_ref, vp_ref, vc_ref, vn_ref, bias_ref, sink_ref,
                o_ref, *, nb):
    i = pl.program_id(0)
    B = SWA_BLOCK
    G = SWA_HEADS // SWA_KV_HEADS
    qi = lax.broadcasted_iota(jnp.int32, (G * B, 3 * B), 0) % B
    sj = lax.broadcasted_iota(jnp.int32, (G * B, 3 * B), 1)
    rel = sj - B - qi
    lo = jnp.where(i == 0, B, 0)
    hi = jnp.where(i == nb - 1, 2 * B, 3 * B)
    valid = (jnp.abs(rel) <= WINDOW) & (sj >= lo) & (sj < hi)
    for g in range(SWA_KV_HEADS):
        kb = jnp.concatenate([kp_ref[g], kc_ref[g], kn_ref[g]], axis=0).astype(BF16)
        vb = jnp.concatenate([vp_ref[g], vc_ref[g], vn_ref[g]], axis=0).astype(BF16)
        q = q_ref[g * G:(g + 1) * G].reshape(G * B, SWA_DH).astype(BF16)
        bias = bias_ref[g * G:(g + 1) * G].reshape(G * B, 3 * B)
        s = _dot_nt(q, kb) * (SWA_DH ** -0.5) + bias
        s = jnp.where(valid, s, -1e30)
        sink = jnp.broadcast_to(sink_ref[g * G:(g + 1) * G], (G, B, 1)).reshape(G * B, 1)
        m = jnp.maximum(jnp.max(s, axis=-1, keepdims=True), sink)
        p = jnp.exp(s - m)
        den = jnp.sum(p, axis=-1, keepdims=True) + jnp.exp(sink - m)
        o = _dot(p.astype(BF16), vb) / den
        o_ref[g * G:(g + 1) * G] = o.reshape(G, B, SWA_DH).astype(o_ref.dtype)


def swa_mixer(q, kv, sink, t5_bias):
    L = q.shape[0]
    B = SWA_BLOCK
    nb = L // B
    qh = q.reshape(L, SWA_HEADS, SWA_DH).transpose(1, 0, 2)
    kh = kv[:, :SWA_KV_HEADS * SWA_DH].reshape(L, SWA_KV_HEADS, SWA_DH).transpose(1, 0, 2)
    vh = kv[:, SWA_KV_HEADS * SWA_DH:].reshape(L, SWA_KV_HEADS, SWA_DH).transpose(1, 0, 2)
    rel = jnp.arange(3 * B)[None, :] - B - jnp.arange(B)[:, None]
    bias = t5_bias.astype(F32)[_t5_bucket(rel)].transpose(2, 0, 1)
    kspec = lambda f: pl.BlockSpec((SWA_KV_HEADS, B, SWA_DH), lambda i: (0, f(i), 0))
    prev = lambda i: jnp.maximum(i - 1, 0)
    cur = lambda i: i
    nxt = lambda i: jnp.minimum(i + 1, nb - 1)
    out = pl.pallas_call(
        functools.partial(_swa_kernel, nb=nb),
        grid=(nb,),
        in_specs=[
            pl.BlockSpec((SWA_HEADS, B, SWA_DH), lambda i: (0, i, 0)),
            kspec(prev), kspec(cur), kspec(nxt),
            kspec(prev), kspec(cur), kspec(nxt),
            pl.BlockSpec((SWA_HEADS, B, 3 * B), lambda i: (0, 0, 0)),
            pl.BlockSpec((SWA_HEADS, 1, 1), lambda i: (0, 0, 0)),
        ],
        out_specs=pl.BlockSpec((SWA_HEADS, B, SWA_DH), lambda i: (0, i, 0)),
        out_shape=jax.ShapeDtypeStruct((SWA_HEADS, L, SWA_DH), BF16),
        compiler_params=_cp(("parallel",)),
        name="swa",
    )(qh, kh, kh, kh, vh, vh, vh, bias, sink.astype(F32).reshape(SWA_HEADS, 1, 1))
    return out.transpose(1, 0, 2).reshape(L, SWA_HEADS * SWA_DH)


SWA_G = SWA_HEADS // SWA_KV_HEADS
SWA_VT_ROWS = SWA_DH + 16


def _swa_t_kernel(q_ref, kvp_ref, kvc_ref, kvn_ref, bias_ref, sink_ref, o_ref, *, nb):
    i = pl.program_id(0)
    B = SWA_BLOCK
    KD = SWA_KV_HEADS * SWA_DH
    sj = lax.broadcasted_iota(jnp.int32, (3 * B, SWA_G * B), 0)
    qi = lax.broadcasted_iota(jnp.int32, (3 * B, SWA_G * B), 1) % B
    rel = sj - B - qi
    lo = jnp.where(i == 0, B, 0)
    hi = jnp.where(i == nb - 1, 2 * B, 3 * B)
    valid = (jnp.abs(rel) <= WINDOW) & (sj >= lo) & (sj < hi)
    qt = (q_ref[...] * (SWA_DH ** -0.5)).T
    kb = jnp.concatenate([kvp_ref[:, 0:KD], kvc_ref[:, 0:KD], kvn_ref[:, 0:KD]], axis=0).astype(BF16)
    vt = jnp.concatenate([kvp_ref[:, KD:2 * KD].T, kvc_ref[:, KD:2 * KD].T, kvn_ref[:, KD:2 * KD].T],
                         axis=1).astype(BF16)
    zq = jnp.zeros((SWA_DH, SWA_G * B), F32)
    ones = jnp.ones((SWA_VT_ROWS - SWA_DH, 3 * B), BF16)
    outs = []
    for g in range(SWA_KV_HEADS):
        qg = jnp.concatenate([qt[(g * SWA_G + hh) * SWA_DH:(g * SWA_G + hh + 1) * SWA_DH, :]
                              for hh in range(SWA_G)], axis=1)
        qpad = jnp.concatenate([qg if gg == g else zq for gg in range(SWA_KV_HEADS)], axis=0).astype(BF16)
        s = _dot(kb, qpad) + bias_ref[g]
        s = jnp.where(valid, s, -1e30)
        sink = sink_ref[g]
        m = jnp.maximum(jnp.max(s, axis=0, keepdims=True), sink)
        p = jnp.exp(s - m).astype(BF16)
        vg = jnp.concatenate([vt[g * SWA_DH:(g + 1) * SWA_DH, :], ones], axis=0)
        pv = _dot(vg, p)
        den = pv[SWA_DH:SWA_DH + 1] + jnp.exp(sink - m)
        o = pv[0:SWA_DH] / den
        outs += [o[:, hh * B:(hh + 1) * B] for hh in range(SWA_G)]
    ot = jnp.concatenate(outs, axis=0)
    o_ref[...] = ot.T.astype(o_ref.dtype)


def swa_mixer_t(q, kv, sink, t5_bias, qcol=0, kvcol=0):
    L = q.shape[0]
    B = SWA_BLOCK
    nb = L // B
    rel = jnp.arange(3 * B)[None, :] - B - jnp.arange(B)[:, None]
    onehot = (_t5_bucket(rel)[..., None] == jnp.arange(T5_BUCKETS)).astype(F32)
    bias = jnp.einsum('qsb,bh->hqs', onehot, t5_bias.astype(F32), precision=lax.Precision.HIGHEST)
    bias_t = bias.reshape(SWA_KV_HEADS, SWA_G, B, 3 * B).transpose(0, 3, 1, 2).reshape(
        SWA_KV_HEADS, 3 * B, SWA_G * B)
    sink_r = jnp.broadcast_to(sink.astype(F32).reshape(SWA_KV_HEADS, 1, SWA_G, 1),
                              (SWA_KV_HEADS, 1, SWA_G, B)).reshape(SWA_KV_HEADS, 1, SWA_G * B)
    kvspec = lambda f: pl.BlockSpec((B, 2 * SWA_KV_HEADS * SWA_DH), lambda i: (f(i), kvcol))
    return pl.pallas_call(
        functools.partial(_swa_t_kernel, nb=nb),
        grid=(nb,),
        in_specs=[
            pl.BlockSpec((B, SWA_HEADS * SWA_DH), lambda i: (i, qcol)),
            kvspec(lambda i: jnp.maximum(i - 1, 0)), kvspec(lambda i: i),
            kvspec(lambda i: jnp.minimum(i + 1, nb - 1)),
            pl.BlockSpec((SWA_KV_HEADS, 3 * B, SWA_G * B), lambda i: (0, 0, 0)),
            pl.BlockSpec((SWA_KV_HEADS, 1, SWA_G * B), lambda i: (0, 0, 0)),
        ],
        out_specs=pl.BlockSpec((B, SWA_HEADS * SWA_DH), lambda i: (i, 0)),
        out_shape=jax.ShapeDtypeStruct((L, SWA_HEADS * SWA_DH), BF16),
        compiler_params=_cp(("parallel",)),
        name="swa",
    )(q, kv, kv, kv, bias_t, sink_r)


def _mla_prep_kernel(cq_ref, ckv_ref, kr_ref, krr_ref, cos_ref, sin_ref, qg_ref, kvg_ref,
                     wqn_ref, wqp_ref, wqr_ref, wkv_ref, q_ref, k_ref, v_ref):
    cq = cq_ref[...]
    qn = (cq * lax.rsqrt(jnp.mean(cq * cq, axis=-1, keepdims=True) + NORM_EPS) * qg_ref[...]).astype(BF16)
    ckv = ckv_ref[...]
    kn = (ckv * lax.rsqrt(jnp.mean(ckv * ckv, axis=-1, keepdims=True) + NORM_EPS) * kvg_ref[...]).astype(BF16)
    cos = cos_ref[...]
    sin = sin_ref[...]
    q_nope = _dot(qn, wqn_ref[...])
    q_pe = _dot(qn, wqp_ref[...])
    q_rot = _dot(qn, wqr_ref[...])
    kv = _dot(kn, wkv_ref[...])
    k_pe = (kr_ref[...] * cos + krr_ref[...] * sin).astype(BF16)
    for h in range(MLA_HEADS):
        sl = slice(h * 128, (h + 1) * 128)
        q_ref[h, :, 0:128] = q_nope[:, sl].astype(BF16)
        q_ref[h, :, 128:256] = (q_pe[:, sl] * cos + q_rot[:, sl] * sin).astype(BF16)
        k_ref[h, :, 0:128] = kv[:, h * 256:h * 256 + 128].astype(BF16)
        k_ref[h, :, 128:256] = k_pe
        v_ref[h] = kv[:, h * 256 + 128:(h + 1) * 256].astype(BF16)


def _mla_attn_kernel(q_ref, k_ref, v_ref, o_ref, *, tk, nk):
    q = q_ref[0]
    tq = q.shape[0]
    scale = (MLA_NOPE + MLA_ROPE) ** -0.5

    def body(kc, carry):
        m, l, acc = carry
        k0 = pl.multiple_of(kc * tk, tk)
        k = k_ref[0, pl.ds(k0, tk), :]
        v = v_ref[0, pl.ds(k0, tk), :]
        s = _dot_nt(q, k) * scale
        m_new = jnp.maximum(m, jnp.max(s, axis=-1, keepdims=True))
        alpha = jnp.exp(m - m_new)
        p = jnp.exp(s - m_new)
        l = alpha * l + jnp.sum(p, axis=-1, keepdims=True)
        acc = alpha * acc + _dot(p.astype(BF16), v)
        return m_new, l, acc

    m0 = jnp.full((tq, 1), -jnp.inf, F32)
    l0 = jnp.zeros((tq, 1), F32)
    a0 = jnp.zeros((tq, MLA_V), F32)
    m, l, acc = lax.fori_loop(0, nk, body, (m0, l0, a0))
    o_ref[...] = (acc / l).astype(o_ref.dtype)


def mla_mixer(c_q, c_kv, kr_pad, krr_pad, q_gain, kv_gain, w_uq, w_ukv, R=512, tq=512, tk=512):
    L = c_q.shape[0]
    H = MLA_HEADS
    half = MLA_ROPE // 2
    pos = jnp.arange(L, dtype=F32)
    inv_freq = ROPE_THETA ** (-jnp.arange(0, MLA_ROPE, 2, dtype=F32) / MLA_ROPE)
    ang = pos[:, None] * inv_freq[None, :]
    cos, sin = jnp.cos(ang), jnp.sin(ang)
    zpad = jnp.zeros((L, 128 - MLA_ROPE), F32)
    cos_t = jnp.concatenate([cos, cos, zpad], axis=1)
    sin_t = jnp.concatenate([-sin, sin, zpad], axis=1)
    wq = w_uq.reshape(MLA_Q_RANK, H, MLA_NOPE + MLA_ROPE)
    wqn = wq[:, :, :MLA_NOPE].reshape(MLA_Q_RANK, H * MLA_NOPE).astype(BF16)
    wpe = wq[:, :, MLA_NOPE:]
    zw = jnp.zeros((MLA_Q_RANK, H, 128 - MLA_ROPE), w_uq.dtype)
    wqp = jnp.concatenate([wpe, zw], axis=2).reshape(MLA_Q_RANK, H * 128).astype(BF16)
    wrot = jnp.concatenate([wpe[:, :, half:], wpe[:, :, :half], zw], axis=2)
    wqr = wrot.reshape(MLA_Q_RANK, H * 128).astype(BF16)
    wkv = w_ukv.astype(BF16)
    row = lambda w: pl.BlockSpec((R, w), lambda i: (i, 0))
    full = lambda a, b: pl.BlockSpec((a, b), lambda i: (0, 0))
    q, k, v = pl.pallas_call(
        _mla_prep_kernel,
        grid=(L // R,),
        in_specs=[row(MLA_Q_RANK), row(MLA_KV_RANK), row(128), row(128), row(128), row(128),
                  full(1, MLA_Q_RANK), full(1, MLA_KV_RANK),
                  full(MLA_Q_RANK, H * 128), full(MLA_Q_RANK, H * 128), full(MLA_Q_RANK, H * 128),
                  full(MLA_KV_RANK, H * 256)],
        out_specs=[pl.BlockSpec((H, R, 256), lambda i: (0, i, 0)),
                   pl.BlockSpec((H, R, 256), lambda i: (0, i, 0)),
                   pl.BlockSpec((H, R, 128), lambda i: (0, i, 0))],
        out_shape=[jax.ShapeDtypeStruct((H, L, 256), BF16),
                   jax.ShapeDtypeStruct((H, L, 256), BF16),
                   jax.ShapeDtypeStruct((H, L, 128), BF16)],
        compiler_params=_cp(("parallel",)),
        name="mla_prep",
    )(c_q, c_kv, kr_pad, krr_pad, cos_t, sin_t,
      q_gain.astype(F32).reshape(1, -1), kv_gain.astype(F32).reshape(1, -1), wqn, wqp, wqr, wkv)
    return pl.pallas_call(
        functools.partial(_mla_attn_kernel, tk=tk, nk=L // tk),
        grid=(H, L // tq),
        in_specs=[pl.BlockSpec((1, tq, 256), lambda h, i: (h, i, 0)),
                  pl.BlockSpec((1, L, 256), lambda h, i: (h, 0, 0)),
                  pl.BlockSpec((1, L, 128), lambda h, i: (h, 0, 0))],
        out_specs=pl.BlockSpec((tq, MLA_V), lambda h, i: (i, h)),
        out_shape=jax.ShapeDtypeStruct((L, H * MLA_V), BF16),
        compiler_params=_cp(("parallel", "parallel")),
        name="mla_attn",
    )(q, k, v)


MLA_VT_ROWS = MLA_V + 16
LOG2E = 1.4426950408889634


def _mla_prep_t_kernel(cq_ref, ckv_ref, kr_ref, krr_ref, cos_ref, sin_ref, cost_ref, sint_ref,
                       qg_ref, kvg_ref, wqn_ref, wqp_ref, wqr_ref, wk_ref, wv_ref,
                       qt_ref, k_ref, vt_ref):
    cq = cq_ref[...]
    qn = (cq * lax.rsqrt(jnp.mean(cq * cq, axis=-1, keepdims=True) + NORM_EPS) * qg_ref[...]).astype(BF16)
    ckv = ckv_ref[...]
    kn = (ckv * lax.rsqrt(jnp.mean(ckv * ckv, axis=-1, keepdims=True) + NORM_EPS) * kvg_ref[...]).astype(BF16)
    c = (MLA_NOPE + MLA_ROPE) ** -0.5 * LOG2E
    qt_nope = _dot_nt(wqn_ref[...], qn)
    qt_pe = _dot_nt(wqp_ref[...], qn)
    qt_rot = _dot_nt(wqr_ref[...], qn)
    cost = cost_ref[...]
    sint = sint_ref[...]
    k_nope = _dot(kn, wk_ref[...])
    vt = _dot_nt(wv_ref[...], kn)
    k_pe = (kr_ref[...] * cos_ref[...] + krr_ref[...] * sin_ref[...]).astype(BF16)
    ones = jnp.ones((MLA_VT_ROWS - MLA_V, vt.shape[1]), BF16)
    for h in range(MLA_HEADS):
        sl = slice(h * 128, (h + 1) * 128)
        qt_ref[h, 0:128, :] = (qt_nope[sl] * c).astype(BF16)
        qt_ref[h, 128:256, :] = ((qt_pe[sl] * cost + qt_rot[sl] * sint) * c).astype(BF16)
        k_ref[h, :, 0:128] = k_nope[:, sl].astype(BF16)
        k_ref[h, :, 128:256] = k_pe
        vt_ref[h, 0:MLA_V, :] = vt[sl].astype(BF16)
        vt_ref[h, MLA_V:MLA_VT_ROWS, :] = ones


def _mla_attn_t_kernel(qt_ref, k_ref, vt_ref, o_ref, s_scr, *, tk, nk, tsub):
    tq = qt_ref.shape[2]
    nsub = tq // tsub
    subs = range(nsub)
    qts = [qt_ref[0, :, j * tsub:(j + 1) * tsub] for j in subs]

    def put_scores(kc, slot):
        k0 = pl.multiple_of(kc * tk, tk)
        kblk = k_ref[0, pl.ds(k0, tk), :]
        for j in subs:
            s_scr[slot, j] = _dot(kblk, qts[j])

    def consume(kc, slot, ms, accs):
        k0 = pl.multiple_of(kc * tk, tk)
        vt = vt_ref[0, :, pl.ds(k0, tk)]
        ss = [s_scr[slot, j] for j in subs]
        m_new = [jnp.maximum(ms[j], jnp.max(ss[j], axis=0, keepdims=True)) for j in subs]
        ps = [jnp.exp2(ss[j] - m_new[j]).astype(BF16) for j in subs]
        pv = [_dot(vt, ps[j]) for j in subs]
        accs = [jnp.exp2(ms[j] - m_new[j]) * accs[j] + pv[j] for j in subs]
        return m_new, accs

    def body(kk, carry):
        ms, accs = carry
        kc = 2 * kk
        put_scores(kc + 1, 1)
        ms, accs = consume(kc, 0, list(ms), list(accs))
        put_scores(jnp.minimum(kc + 2, nk - 1), 0)
        ms, accs = consume(kc + 1, 1, ms, accs)
        return tuple(ms), tuple(accs)

    m0 = tuple(jnp.full((1, tsub), -jnp.inf, F32) for _ in subs)
    a0 = tuple(jnp.zeros((MLA_VT_ROWS, tsub), F32) for _ in subs)
    put_scores(0, 0)
    ms, accs = lax.fori_loop(0, nk // 2, body, (m0, a0))
    for j in subs:
        o = accs[j][0:MLA_V] / accs[j][MLA_V:MLA_V + 1]
        o_ref[j * tsub:(j + 1) * tsub, :] = o.T.astype(o_ref.dtype)


def _rope_tables(L):
    pos = np.arange(L, dtype=np.float32)
    inv_freq = np.float32(ROPE_THETA) ** (-np.arange(0, MLA_ROPE, 2, dtype=np.float32) / np.float32(MLA_ROPE))
    ang = pos[:, None] * inv_freq.astype(np.float32)[None, :]
    return jnp.asarray(np.cos(ang), F32), jnp.asarray(np.sin(ang), F32)


def mla_mixer_t(c_q, c_kv, kr_pad, krr_pad, q_gain, kv_gain, w_uq, w_ukv, R=512, tq=512, tk=512, tsub=256,
                cols=(0, 0, 0, 0)):
    L = c_q.shape[0]
    H = MLA_HEADS
    half = MLA_ROPE // 2
    cos, sin = _rope_tables(L)
    zpad = jnp.zeros((L, 128 - MLA_ROPE), F32)
    cos_t = jnp.concatenate([cos, cos, zpad], axis=1)
    sin_t = jnp.concatenate([-sin, sin, zpad], axis=1)
    wq = w_uq.reshape(MLA_Q_RANK, H, MLA_NOPE + MLA_ROPE)
    wqn = wq[:, :, :MLA_NOPE].reshape(MLA_Q_RANK, H * MLA_NOPE).T.astype(BF16)
    wpe = wq[:, :, MLA_NOPE:]
    zw = jnp.zeros((MLA_Q_RANK, H, 128 - MLA_ROPE), w_uq.dtype)
    wqp = jnp.concatenate([wpe, zw], axis=2).reshape(MLA_Q_RANK, H * 128).T.astype(BF16)
    wrot = jnp.concatenate([wpe[:, :, half:], wpe[:, :, :half], zw], axis=2)
    wqr = wrot.reshape(MLA_Q_RANK, H * 128).T.astype(BF16)
    wkv = w_ukv.reshape(MLA_KV_RANK, H, MLA_NOPE + MLA_V)
    wk = wkv[:, :, :MLA_NOPE].reshape(MLA_KV_RANK, H * MLA_NOPE).astype(BF16)
    wv = wkv[:, :, MLA_NOPE:].reshape(MLA_KV_RANK, H * MLA_V).T.astype(BF16)
    row = lambda w, c=0: pl.BlockSpec((R, w), lambda i: (i, c))
    col = lambda w: pl.BlockSpec((w, R), lambda i: (0, i))
    full = lambda a, b: pl.BlockSpec((a, b), lambda i: (0, 0))
    qt, k, vt = pl.pallas_call(
        _mla_prep_t_kernel,
        grid=(L // R,),
        in_specs=[row(MLA_Q_RANK, cols[0]), row(MLA_KV_RANK, cols[1]), row(128, cols[2]), row(128, cols[3]),
                  row(128), row(128), col(128), col(128),
                  full(1, MLA_Q_RANK), full(1, MLA_KV_RANK),
                  full(H * 128, MLA_Q_RANK), full(H * 128, MLA_Q_RANK), full(H * 128, MLA_Q_RANK),
                  full(MLA_KV_RANK, H * 128), full(H * 128, MLA_KV_RANK)],
        out_specs=[pl.BlockSpec((H, 256, R), lambda i: (0, 0, i)),
                   pl.BlockSpec((H, R, 256), lambda i: (0, i, 0)),
                   pl.BlockSpec((H, MLA_VT_ROWS, R), lambda i: (0, 0, i))],
        out_shape=[jax.ShapeDtypeStruct((H, 256, L), BF16),
                   jax.ShapeDtypeStruct((H, L, 256), BF16),
                   jax.ShapeDtypeStruct((H, MLA_VT_ROWS, L), BF16)],
        compiler_params=_cp(("parallel",)),
        name="mla_prep",
    )(c_q, c_kv, kr_pad, krr_pad, cos_t, sin_t, cos_t.T, sin_t.T,
      q_gain.astype(F32).reshape(1, -1), kv_gain.astype(F32).reshape(1, -1), wqn, wqp, wqr, wk, wv)
    return pl.pallas_call(
        functools.partial(_mla_attn_t_kernel, tk=tk, nk=L // tk, tsub=tsub),
        grid=(H, L // tq),
        in_specs=[pl.BlockSpec((1, 256, tq), lambda h, i: (h, 0, i)),
                  pl.BlockSpec((1, L, 256), lambda h, i: (h, 0, 0)),
                  pl.BlockSpec((1, MLA_VT_ROWS, L), lambda h, i: (h, 0, 0))],
        out_specs=pl.BlockSpec((tq, MLA_V), lambda h, i: (i, h)),
        out_shape=jax.ShapeDtypeStruct((L, H * MLA_V), BF16),
        scratch_shapes=[pltpu.VMEM((2, tq // tsub, tk, tsub), F32)],
        compiler_params=_cp(("parallel", "parallel")),
        name="mla_attn",
    )(qt, k, vt)


def _merge_kernel(h_ref, ya_ref, yb_ref, yc_ref, yd_ref, g0_ref, g1_ref, g2_ref, g3_ref, p_ref, o_ref):
    h = h_ref[...]
    acc = None
    for b, (y_ref, g_ref) in enumerate(((ya_ref, g0_ref), (yb_ref, g1_ref), (yc_ref, g2_ref), (yd_ref, g3_ref))):
        gate = _sigmoid(_dot(h, g_ref[...]))
        t = gate * _dot(y_ref[...], p_ref[b])
        acc = t if acc is None else acc + t
    o_ref[...] = acc.astype(o_ref.dtype)


def merge_branches(h, ys, w_gate, w_branch, tm=512, tn=256):
    L = h.shape[0]
    nj = D_MODEL // tn
    gspec = lambda b: pl.BlockSpec((D_MODEL, tn), lambda i, j: (0, b * nj + j))
    yspec = pl.BlockSpec((tm, BW), lambda i, j: (i, 0))
    return pl.pallas_call(
        _merge_kernel,
        grid=(L // tm, nj),
        in_specs=[pl.BlockSpec((tm, D_MODEL), lambda i, j: (i, 0)), yspec, yspec, yspec, yspec,
                  gspec(0), gspec(1), gspec(2), gspec(3),
                  pl.BlockSpec((4, BW, tn), lambda i, j: (0, 0, j))],
        out_specs=pl.BlockSpec((tm, tn), lambda i, j: (i, j)),
        out_shape=jax.ShapeDtypeStruct((L, D_MODEL), BF16),
        compiler_params=_cp(("parallel", "arbitrary")),
        name="merge",
    )(h, *ys, w_gate, w_gate, w_gate, w_gate, w_branch)


def _out_proj_kernel(x_ref, m_ref, w_ref, g_ref, o_ref):
    f = _dot(m_ref[...], w_ref[...])
    f = f * lax.rsqrt(jnp.mean(f * f, axis=-1, keepdims=True) + NORM_EPS) * g_ref[...]
    o_ref[...] = x_ref[...] + f


def out_proj(x, m, w_out, gain, tm=512):
    L = x.shape[0]
    return pl.pallas_call(
        _out_proj_kernel,
        grid=(L // tm,),
        in_specs=[pl.BlockSpec((tm, D_MODEL), lambda i: (i, 0)),
                  pl.BlockSpec((tm, D_MODEL), lambda i: (i, 0)),
                  pl.BlockSpec((D_MODEL, D_MODEL), lambda i: (0, 0)),
                  pl.BlockSpec((1, D_MODEL), lambda i: (0, 0))],
        out_specs=pl.BlockSpec((tm, D_MODEL), lambda i: (i, 0)),
        out_shape=jax.ShapeDtypeStruct((L, D_MODEL), F32),
        compiler_params=_cp(("parallel",)),
        name="out_proj",
    )(x, m, w_out, gain.reshape(1, D_MODEL))


def _mlp_kernel(x_ref, g1_ref, w1_ref, w2_ref, g2_ref, o_ref, h_scr, acc):
    j = pl.program_id(1)

    @pl.when(j == 0)
    def _():
        x = x_ref[...]
        ms = jnp.mean(x * x, axis=-1, keepdims=True)
        h_scr[...] = (x * lax.rsqrt(ms + NORM_EPS) * g1_ref[...]).astype(BF16)
        acc[...] = jnp.zeros_like(acc)

    a = _dot(h_scr[...], w1_ref[...])
    a = jnp.maximum(a, 0.0)
    a = (a * a).astype(BF16)
    acc[...] += _dot(a, w2_ref[...])

    @pl.when(j == pl.num_programs(1) - 1)
    def _():
        f = acc[...]
        f = f * lax.rsqrt(jnp.mean(f * f, axis=-1, keepdims=True) + NORM_EPS) * g2_ref[...]
        o_ref[...] = x_ref[...] + f


def mlp(x, g1, w1, w2, g2, tm=512, tf=512):
    L = x.shape[0]
    return pl.pallas_call(
        _mlp_kernel,
        grid=(L // tm, D_FF // tf),
        in_specs=[pl.BlockSpec((tm, D_MODEL), lambda i, j: (i, 0)),
                  pl.BlockSpec((1, D_MODEL), lambda i, j: (0, 0)),
                  pl.BlockSpec((D_MODEL, tf), lambda i, j: (0, j)),
                  pl.BlockSpec((tf, D_MODEL), lambda i, j: (j, 0)),
                  pl.BlockSpec((1, D_MODEL), lambda i, j: (0, 0))],
        out_specs=pl.BlockSpec((tm, D_MODEL), lambda i, j: (i, 0)),
        out_shape=jax.ShapeDtypeStruct((L, D_MODEL), F32),
        scratch_shapes=[pltpu.VMEM((tm, D_MODEL), BF16), pltpu.VMEM((tm, D_MODEL), F32)],
        compiler_params=_cp(("parallel", "arbitrary")),
        name="mlp",
    )(x, g1.reshape(1, D_MODEL), w1, w2, g2.reshape(1, D_MODEL))


def _mix_weight(w):
    half = MLA_ROPE // 2
    z = lambda n: jnp.zeros((D_MODEL, n), w.dtype)
    kr = w[:, _OFF_KR:_OFF_KR + MLA_ROPE]
    cols = [
        w[:, _OFF_QKV:_OFF_QKV + 3 * BW],
        w[:, _OFF_U:_OFF_U + BW],
        w[:, _OFF_Z:_OFF_Z + BW],
        w[:, _OFF_SWAQ:_OFF_SWAQ + BW],
        w[:, _OFF_CKV:_OFF_CKV + MLA_KV_RANK],
        w[:, _OFF_SWAKV:_OFF_SWAKV + 256],
        w[:, _OFF_CQ:_OFF_CQ + MLA_Q_RANK],
        kr, z(64),
        kr[:, half:], kr[:, :half], z(64),
        w[:, _OFF_BETA:_OFF_BETA + 16], z(112),
    ]
    return jnp.concatenate(cols, axis=1).astype(BF16)


def kernel(x, w_in, s5_lam_re, s5_lam_im, s5_log_step, s5_b_re, s5_b_im, s5_c_re, s5_c_im, s5_d, s5_w_glu, s5_b_glu, gdn_conv, gdn_a_log, gdn_dt_bias, gdn_o_gain, swa_sink, t5_bias, mla_q_gain, mla_kv_gain, mla_w_uq, mla_w_ukv, w_branch, w_out, mix_pre_gain, mix_post_gain, mlp_pre_gain, mlp_post_gain, w_mlp_in, w_mlp_out):
    B_, L, _ = x.shape
    outs = []
    for b in range(B_):
        xb = x[b].astype(F32)
        for l in range(DEPTH):
            w_mix = _mix_weight(w_in[l])
            w_gate = w_in[l][:, _OFF_GATE:].astype(BF16)
            proj, h = in_proj(xb, mix_pre_gain[l].astype(F32), w_mix, tm=1024)
            y_a = s5_mixer(proj, s5_lam_re[l], s5_lam_im[l], s5_log_step[l],
                           s5_b_re[l], s5_b_im[l], s5_c_re[l], s5_c_im[l], s5_d[l], s5_w_glu[l],
                           s5_b_glu[l], ucol=MIX_U // BW)
            y_b = gdn_mixer_fused(proj, proj, proj[:, MIX_SMALL:MIX_SMALL + 16], gdn_conv[l], gdn_a_log[l],
                                  gdn_dt_bias[l], gdn_o_gain[l], zcol=MIX_Z // BW)
            y_c = swa_mixer_t(proj, proj, swa_sink[l], t5_bias, qcol=MIX_SWAQ // BW, kvcol=MIX_SWAKV // 256)
            y_d = mla_mixer_t(proj, proj, proj, proj, mla_q_gain[l], mla_kv_gain[l], mla_w_uq[l], mla_w_ukv[l],
                              cols=(MIX_CQ // MLA_Q_RANK, MIX_CKV // MLA_KV_RANK, MIX_KR // 128,
                                    MIX_KRROT // 128))
            merged = merge_branches(h, (y_a, y_b, y_c, y_d), w_gate, w_branch[l].astype(BF16))
            xb = out_proj(xb, merged, w_out[l].astype(BF16), mix_post_gain[l].astype(F32))
            xb = mlp(xb, mlp_pre_gain[l].astype(F32), w_mlp_in[l].astype(BF16),
                     w_mlp_out[l].astype(BF16), mlp_post_gain[l].astype(F32))
        outs.append(xb)
    return jnp.stack(outs).astype(x.dtype)
```

```python
import functools
import math

import numpy as np
import jax
import jax.numpy as jnp
from jax import lax
from jax.experimental import pallas as pl
from jax.experimental.pallas import tpu as pltpu

F32 = jnp.float32
BF16 = jnp.bfloat16

D_MODEL = 2048
DEPTH = 4
BW = 512
D_FF = 4 * D_MODEL
NORM_EPS = 1e-6

S5_GROUP = 16
S5_GROUPS = 32
S5_STATE = 64
S5_NSTATE = S5_GROUPS * S5_STATE
S5_HALF = S5_NSTATE // 2

GDN_HEADS = 4
GDN_DH = 128
GDN_CHUNK = 64

SWA_HEADS = 8
SWA_KV_HEADS = 2
SWA_DH = 64
SWA_BLOCK = 128
WINDOW = 128
T5_BUCKETS = 32
T5_MAX_DISTANCE = 128

MLA_HEADS = 4
MLA_Q_RANK = 384
MLA_KV_RANK = 512
MLA_NOPE = 128
MLA_ROPE = 64
MLA_V = 128
ROPE_THETA = 10000.0

_OFF_U = 0
_OFF_QKV = 512
_OFF_Z = 2048
_OFF_BETA = 2560
_OFF_DECAY = 2568
_OFF_SWAQ = 2576
_OFF_SWAKV = 3088
_OFF_CQ = 3344
_OFF_CKV = 3728
_OFF_KR = 4240
_OFF_GATE = 4304
D_IN = _OFF_GATE + 4 * D_MODEL

MIX_QKV = 0
MIX_U = 1536
MIX_Z = 2048
MIX_SWAQ = 2560
MIX_CKV = 3072
MIX_SWAKV = 3584
MIX_CQ = 3840
MIX_KR = 4224
MIX_KRROT = 4352
MIX_SMALL = 4480
N_MIX = 4608

VMEM_LIMIT = 56 * 1024 * 1024


def _cp(sem, limit=VMEM_LIMIT):
    return pltpu.CompilerParams(dimension_semantics=sem, vmem_limit_bytes=limit)


def _dot(a, b):
    return jnp.dot(a, b, preferred_element_type=F32)


def _dot_nt(a, b):
    return lax.dot_general(a, b, (((1,), (1,)), ((), ())), preferred_element_type=F32)


def _dot_tn(a, b):
    return lax.dot_general(a, b, (((0,), (0,)), ((), ())), preferred_element_type=F32)


def _sigmoid(x):
    return 1.0 / (1.0 + jnp.exp(-x))


def _softplus(x):
    return jnp.maximum(x, 0.0) + jnp.log(1.0 + jnp.exp(-jnp.abs(x)))


def _in_proj_kernel(x_ref, g_ref, w_ref, o_ref, h_ref, h_scr):
    @pl.when(pl.program_id(1) == 0)
    def _():
        x = x_ref[...]
        ms = jnp.mean(x * x, axis=-1, keepdims=True)
        h = (x * lax.rsqrt(ms + NORM_EPS) * g_ref[...]).astype(BF16)
        h_scr[...] = h
        h_ref[...] = h

    o_ref[...] = _dot(h_scr[...], w_ref[...])


def in_proj(x, gain, w_mix, tm=512, tn=512):
    L = x.shape[0]
    n = w_mix.shape[1]
    return pl.pallas_call(
        _in_proj_kernel,
        grid=(L // tm, n // tn),
        in_specs=[
            pl.BlockSpec((tm, D_MODEL), lambda i, j: (i, 0)),
            pl.BlockSpec((1, D_MODEL), lambda i, j: (0, 0)),
            pl.BlockSpec((D_MODEL, tn), lambda i, j: (0, j)),
        ],
        out_specs=[
            pl.BlockSpec((tm, tn), lambda i, j: (i, j)),
            pl.BlockSpec((tm, D_MODEL), lambda i, j: (i, 0)),
        ],
        out_shape=[
            jax.ShapeDtypeStruct((L, n), F32),
            jax.ShapeDtypeStruct((L, D_MODEL), BF16),
        ],
        scratch_shapes=[pltpu.VMEM((tm, D_MODEL), BF16)],
        compiler_params=_cp(("parallel", "arbitrary")),
        name="in_proj",
    )(x, gain.reshape(1, D_MODEL), w_mix)


def _s5_kernel(*refs, reverse, finalize, T):
    if finalize:
        (u_ref, bd_ref, cd_ref, hs_ref, p_ref, yb_ref, dsk_ref, wglu_ref, bglu_ref,
         o_ref, bur, bui, carry) = refs
    else:
        u_ref, bd_ref, cd_ref, hs_ref, p_ref, o_ref, bur, bui, carry = refs
    nt = T // 8

    @pl.when(pl.program_id(0) == 0)
    def _():
        carry[...] = jnp.zeros_like(carry)

    u = u_ref[...]
    ub = u.astype(BF16)
    for k in range(2):
        r = _dot(ub[:, 256 * k:256 * (k + 1)], bd_ref[k])
        bur[:, S5_HALF * k:S5_HALF * (k + 1)] = r[:, :S5_HALF]
        bui[:, S5_HALF * k:S5_HALF * (k + 1)] = r[:, S5_HALF:]

    def tile(kk, c):
        cr, ci = c
        idx = (nt - 1 - kk) if reverse else kk
        t0 = pl.multiple_of(idx * 8, 8)
        xr = bur[pl.ds(t0, 8), :]
        xi = bui[pl.ds(t0, 8), :]
        for s, d in enumerate((1, 2, 4)):
            sh = (8 - d) if reverse else d
            sr = pltpu.roll(xr, sh, 0)
            si = pltpu.roll(xi, sh, 0)
            ar = hs_ref[2 * s]
            ai = hs_ref[2 * s + 1]
            xr, xi = xr + ar * sr - ai * si, xi + ar * si + ai * sr
        pr = p_ref[0]
        pi = p_ref[1]
        xr, xi = xr + pr * cr - pi * ci, xi + pr * ci + pi * cr
        bur[pl.ds(t0, 8), :] = xr
        bui[pl.ds(t0, 8), :] = xi
        row = 0 if reverse else 7
        return (jnp.broadcast_to(xr[row:row + 1, :], (8, S5_NSTATE)),
                jnp.broadcast_to(xi[row:row + 1, :], (8, S5_NSTATE)))

    cr, ci = lax.fori_loop(0, nt, tile, (carry[0], carry[1]))
    carry[0] = cr
    carry[1] = ci

    ys = []
    for k in range(2):
        sr = bur[:, S5_HALF * k:S5_HALF * (k + 1)].astype(BF16)
        si = bui[:, S5_HALF * k:S5_HALF * (k + 1)].astype(BF16)
        ys.append(_dot(sr, cd_ref[k, 0]) + _dot(si, cd_ref[k, 1]))
    y = jnp.concatenate(ys, axis=1)
    if finalize:
        y = y + yb_ref[...] + dsk_ref[...] * u
        y = jax.nn.gelu(y)
        gate = _sigmoid(_dot(y.astype(BF16), wglu_ref[...]) + bglu_ref[...])
        o_ref[...] = (y * gate).astype(o_ref.dtype)
    else:
        o_ref[...] = y


def s5_direction_call(u, bd, cd, hs, pw, *, reverse, fin=None, T=256, ucol=0):
    L = u.shape[0]
    nb = L // T
    rmap = (lambda i: (nb - 1 - i, 0)) if reverse else (lambda i: (i, 0))
    umap = (lambda i: (nb - 1 - i, ucol)) if reverse else (lambda i: (i, ucol))
    c3 = lambda i: (0, 0, 0)
    in_specs = [
        pl.BlockSpec((T, BW), umap),
        pl.BlockSpec((2, 256, S5_NSTATE), c3),
        pl.BlockSpec((2, 2, S5_HALF, 256), lambda i: (0, 0, 0, 0)),
        pl.BlockSpec((6, 8, S5_NSTATE), c3),
        pl.BlockSpec((2, 8, S5_NSTATE), c3),
    ]
    args = [u, bd, cd, hs, pw]
    if fin is not None:
        yb, dsk, wglu, bglu = fin
        in_specs += [
            pl.BlockSpec((T, BW), rmap),
            pl.BlockSpec((1, BW), lambda i: (0, 0)),
            pl.BlockSpec((BW, BW), lambda i: (0, 0)),
            pl.BlockSpec((1, BW), lambda i: (0, 0)),
        ]
        args += [yb, dsk.reshape(1, BW), wglu, bglu.reshape(1, BW)]
    return pl.pallas_call(
        functools.partial(_s5_kernel, reverse=reverse, finalize=fin is not None, T=T),
        grid=(nb,),
        in_specs=in_specs,
        out_specs=pl.BlockSpec((T, BW), rmap),
        out_shape=jax.ShapeDtypeStruct((L, BW), BF16 if fin is not None else F32),
        scratch_shapes=[
            pltpu.VMEM((T, S5_NSTATE), F32),
            pltpu.VMEM((T, S5_NSTATE), F32),
            pltpu.VMEM((2, 8, S5_NSTATE), F32),
        ],
        compiler_params=_cp(("arbitrary",)),
        name="s5_rev" if reverse else "s5_fwd",
    )(*args)


def _s5_prep(lam_re, lam_im, log_step, b_re, b_im, c_re, c_im, reverse):
    G, P, Hg = S5_GROUPS, S5_STATE, S5_GROUP
    lam_re = jnp.minimum(lam_re.astype(F32), -1e-4)
    lam_im = lam_im.astype(F32)
    dt = jnp.exp(log_step.astype(F32))[:, None]
    mag = jnp.exp(lam_re * dt)
    abar_r = mag * jnp.cos(lam_im * dt)
    abar_i = mag * jnp.sin(lam_im * dt)
    den = lam_re * lam_re + lam_im * lam_im
    xr = abar_r - 1.0
    xi = abar_i
    coef_r = (xr * lam_re + xi * lam_im) / den
    coef_i = (xi * lam_re - xr * lam_im) / den
    b_re = b_re.astype(F32)
    b_im = b_im.astype(F32)
    bbar_r = coef_r[..., None] * b_re - coef_i[..., None] * b_im
    bbar_i = coef_r[..., None] * b_im + coef_i[..., None] * b_re

    eye = jnp.eye(16, dtype=F32)

    def bdiag_in(bb):
        blk = bb.transpose(0, 2, 1).reshape(2, 16, Hg, P)
        return jnp.einsum('gq,kghp->kghqp', eye, blk).reshape(2, 16 * Hg, 16 * P)

    bd = jnp.concatenate([bdiag_in(bbar_r), bdiag_in(bbar_i)], axis=-1).astype(BF16)

    def bdiag_out(cc):
        blk = cc.astype(F32).transpose(0, 2, 1).reshape(2, 16, P, Hg)
        return jnp.einsum('gq,kgph->kgpqh', eye, blk).reshape(2, 16 * P, 16 * Hg)

    cd = jnp.stack([bdiag_out(c_re), -bdiag_out(c_im)], axis=1).astype(BF16)

    ar = abar_r.reshape(-1)
    ai = abar_i.reshape(-1)

    def cm(a, b):
        return a[0] * b[0] - a[1] * b[1], a[0] * b[1] + a[1] * b[0]

    a1 = (ar, ai)
    a2 = cm(a1, a1)
    a3 = cm(a2, a1)
    a4 = cm(a2, a2)
    a5 = cm(a4, a1)
    a6 = cm(a4, a2)
    a7 = cm(a4, a3)
    a8 = cm(a4, a4)
    pows = [a1, a2, a3, a4, a5, a6, a7, a8]
    rows = np.arange(8)[:, None]
    hs = []
    for d, ad in ((1, a1), (2, a2), (4, a4)):
        mask = jnp.asarray((rows <= 7 - d) if reverse else (rows >= d), F32)
        hs.append(mask * ad[0][None, :])
        hs.append(mask * ad[1][None, :])
    hs = jnp.stack(hs)
    order = list(range(7, -1, -1)) if reverse else list(range(8))
    pw = jnp.stack([jnp.stack([pows[t][0] for t in order]),
                    jnp.stack([pows[t][1] for t in order])])
    return bd, cd, hs, pw


def s5_mixer(u, lam_re, lam_im, log_step, b_re, b_im, c_re, c_im, d_skip, w_glu, b_glu, T=256, ucol=0):
    pf = _s5_prep(lam_re[0], lam_im[0], log_step[0], b_re[0], b_im[0], c_re[0], c_im[0], False)
    pb = _s5_prep(lam_re[1], lam_im[1], log_step[1], b_re[1], b_im[1], c_re[1], c_im[1], True)
    yb = s5_direction_call(u, *pb, reverse=True, T=T, ucol=ucol)
    return s5_direction_call(u, *pf, reverse=False, T=T, ucol=ucol,
                             fin=(yb, d_skip.astype(F32), w_glu.astype(BF16), b_glu.astype(F32)))


S5_TT = 16
S5_NPAIR = S5_GROUPS // 2
S5_PW = 2 * S5_TT * S5_GROUP


def _hs_params(ar, ai, reverse):
    def cm(a, b):
        return a[0] * b[0] - a[1] * b[1], a[0] * b[1] + a[1] * b[0]

    a1 = (ar, ai)
    a2 = cm(a1, a1)
    a3 = cm(a2, a1)
    a4 = cm(a2, a2)
    pows = [a1, a2, a3, a4, cm(a4, a1), cm(a4, a2), cm(a4, a3), cm(a4, a4)]
    rows = np.arange(8)[:, None]
    hs = []
    for d, ad in ((1, a1), (2, a2), (4, a4)):
        mask = jnp.asarray((rows <= 7 - d) if reverse else (rows >= d), F32)
        hs.append(mask * ad[0][None, :])
        hs.append(mask * ad[1][None, :])
    order = list(range(7, -1, -1)) if reverse else list(range(8))
    pw = jnp.stack([jnp.stack([pows[t][0] for t in order]),
                    jnp.stack([pows[t][1] for t in order])])
    return jnp.stack(hs), pw


def _s5_tile_params(lam_re, lam_im, log_step, b_re, b_im, c_re, c_im, reverse):
    G, P, Hg, TT = S5_GROUPS, S5_STATE, S5_GROUP, S5_TT
    hp = lax.Precision.HIGHEST
    lam_re = jnp.minimum(lam_re.astype(F32), -1e-4)
    lam_im = lam_im.astype(F32)
    dt = jnp.exp(log_step.astype(F32))[:, None]
    mag = jnp.exp(lam_re * dt)
    abar_r = mag * jnp.cos(lam_im * dt)
    abar_i = mag * jnp.sin(lam_im * dt)
    den = lam_re * lam_re + lam_im * lam_im
    xr = abar_r - 1.0
    xi = abar_i
    coef_r = (xr * lam_re + xi * lam_im) / den
    coef_i = (xi * lam_re - xr * lam_im) / den
    b_re = b_re.astype(F32)
    b_im = b_im.astype(F32)
    bb_r = coef_r[..., None] * b_re - coef_i[..., None] * b_im
    bb_i = coef_r[..., None] * b_im + coef_i[..., None] * b_re
    cr = c_re.astype(F32)
    ci = c_im.astype(F32)
    pr = [jnp.ones_like(abar_r)]
    pi = [jnp.zeros_like(abar_i)]
    for _ in range(TT):
        pr.append(pr[-1] * abar_r - pi[-1] * abar_i)
        pi.append(pr[-2] * abar_i + pi[-1] * abar_r)
    pr = jnp.stack(pr)
    pi = jnp.stack(pi)
    car = cr[None] * pr[:TT, :, None, :] - ci[None] * pi[:TT, :, None, :]
    cai = cr[None] * pi[:TT, :, None, :] + ci[None] * pr[:TT, :, None, :]
    kd = (jnp.einsum('dgkp,gph->dgkh', car, bb_r, precision=hp)
          - jnp.einsum('dgkp,gph->dgkh', cai, bb_i, precision=hp))
    ii = np.arange(TT)[:, None]
    jj = np.arange(TT)[None, :]
    lag = (ii - jj) if reverse else (jj - ii)
    shift = jnp.asarray((lag[None] == np.arange(TT)[:, None, None]), F32)
    toep = jnp.einsum('dij,dgkh->gihjk', shift, kd, precision=hp).reshape(G, TT * Hg, TT * Hg)
    eye2 = jnp.eye(2, dtype=F32)
    e_idx = np.arange(TT) if reverse else (TT - 1 - np.arange(TT))
    er = pr[e_idx]
    ei = pi[e_idx]
    wf_r = (er[..., None] * bb_r[None] - ei[..., None] * bb_i[None]).transpose(1, 0, 3, 2)
    wf_i = (er[..., None] * bb_i[None] + ei[..., None] * bb_r[None]).transpose(1, 0, 3, 2)

    def pair_rows(w):
        return jnp.einsum('ab,kaxp->kaxbp', eye2, w.reshape(S5_NPAIR, 2, TT * Hg, P)).reshape(
            S5_NPAIR, S5_PW, 2 * P)

    wf = jnp.concatenate([pair_rows(wf_r.reshape(G, TT * Hg, P)),
                          pair_rows(wf_i.reshape(G, TT * Hg, P))], axis=-1).astype(BF16)
    f_idx = (TT - np.arange(TT)) if reverse else (np.arange(TT) + 1)
    fr = pr[f_idx]
    fi = pi[f_idx]
    mr = (cr[None] * fr[:, :, None, :] - ci[None] * fi[:, :, None, :]).transpose(1, 3, 0, 2)
    mi = (cr[None] * fi[:, :, None, :] + ci[None] * fr[:, :, None, :]).transpose(1, 3, 0, 2)

    def pair_cols(w):
        return jnp.einsum('ab,kapx->kapbx', eye2, w.reshape(S5_NPAIR, 2, P, TT * Hg)).reshape(
            S5_NPAIR, 2 * P, S5_PW)

    wc = jnp.concatenate([pair_cols(mr.reshape(G, P, TT * Hg)),
                          pair_cols(-mi.reshape(G, P, TT * Hg))], axis=1).astype(BF16)
    hs, pw = _hs_params(pr[TT].reshape(-1), pi[TT].reshape(-1), reverse)
    return toep, wf, wc, hs, pw


def _s5_state_kernel(u_ref, wff_ref, wfb_ref, f_ref):
    u = u_ref[...]
    for d, w_ref in enumerate((wff_ref, wfb_ref)):
        f = _dot(u, w_ref[...])
        f_ref[d, 0] = f[:, :2 * S5_STATE]
        f_ref[d, 1] = f[:, 2 * S5_STATE:]


def _s5_scan_kernel(f_ref, hs_ref, p_ref, o_ref, pad_r, pad_i, *, reverse, N):
    nt = N // 8
    z8 = jnp.zeros((8, S5_NSTATE), F32)
    if reverse:
        pad_r[0:N] = f_ref[0]
        pad_i[0:N] = f_ref[1]
        pad_r[N:N + 8] = z8
        pad_i[N:N + 8] = z8
        o_ref[0] = pad_r[1:N + 1]
        o_ref[1] = pad_i[1:N + 1]
    else:
        pad_r[0:8] = z8
        pad_i[0:8] = z8
        pad_r[8:N + 8] = f_ref[0]
        pad_i[8:N + 8] = f_ref[1]
        o_ref[0] = pad_r[7:N + 7]
        o_ref[1] = pad_i[7:N + 7]

    def tile(kk, c):
        cr, ci = c
        idx = (nt - 1 - kk) if reverse else kk
        t0 = pl.multiple_of(idx * 8, 8)
        xr = o_ref[0, pl.ds(t0, 8), :]
        xi = o_ref[1, pl.ds(t0, 8), :]
        for s, d in enumerate((1, 2, 4)):
            sh = (8 - d) if reverse else d
            sr = pltpu.roll(xr, sh, 0)
            si = pltpu.roll(xi, sh, 0)
            ar = hs_ref[2 * s]
            ai = hs_ref[2 * s + 1]
            xr, xi = xr + ar * sr - ai * si, xi + ar * si + ai * sr
        pr = p_ref[0]
        pi = p_ref[1]
        xr, xi = xr + pr * cr - pi * ci, xi + pr * ci + pi * cr
        o_ref[0, pl.ds(t0, 8), :] = xr
        o_ref[1, pl.ds(t0, 8), :] = xi
        row = 0 if reverse else 7
        return (jnp.broadcast_to(xr[row:row + 1, :], (8, S5_NSTATE)),
                jnp.broadcast_to(xi[row:row + 1, :], (8, S5_NSTATE)))

    lax.fori_loop(0, nt, tile, (z8, z8))


def _s5_out_kernel(u_ref, toep_ref, cf_ref, cb_ref, wc_ref, y_ref):
    u = u_ref[...]
    st = jnp.concatenate([cf_ref[0], cf_ref[1], cb_ref[0], cb_ref[1]], axis=1).astype(BF16)
    y_ref[...] = _dot(u, toep_ref[...]) + _dot(st, wc_ref[...])


def _s5_final_kernel(y_ref, u_ref, dsk_ref, wglu_ref, bglu_ref, o_ref):
    y = y_ref[...] + dsk_ref[...] * u_ref[...]
    y = jax.nn.gelu(y)
    gate = _sigmoid(_dot(y.astype(BF16), wglu_ref[...]) + bglu_ref[...])
    o_ref[...] = (y * gate).astype(o_ref.dtype)


def s5_mixer_tiled(u, lam_re, lam_im, log_step, b_re, b_im, c_re, c_im, d_skip, w_glu, b_glu, ucol=0, RF=512):
    L = u.shape[0]
    TT, Hg = S5_TT, S5_GROUP
    N = L // TT
    NP = S5_NPAIR
    tf, wff, wcf, hsf, pwf = _s5_tile_params(lam_re[0], lam_im[0], log_step[0], b_re[0], b_im[0],
                                             c_re[0], c_im[0], False)
    tb, wfb, wcb, hsb, pwb = _s5_tile_params(lam_re[1], lam_im[1], log_step[1], b_re[1], b_im[1],
                                             c_re[1], c_im[1], True)
    eye2 = jnp.eye(2, dtype=F32)
    toep = jnp.einsum('ab,kaxy->kaxby', eye2, (tf + tb).reshape(NP, 2, TT * Hg, TT * Hg)).reshape(
        NP, S5_PW, S5_PW).astype(BF16)
    wc = jnp.concatenate([wcf, wcb], axis=1)
    uu = u[:, ucol * BW:(ucol + 1) * BW]
    up = uu.reshape(N, TT, NP, 2, Hg).transpose(2, 0, 3, 1, 4).reshape(NP, N, S5_PW).astype(BF16)
    pair = lambda a, b: pl.BlockSpec((None, a, b), lambda k: (k, 0, 0))
    f = pl.pallas_call(
        _s5_state_kernel,
        grid=(NP,),
        in_specs=[pair(N, S5_PW), pair(S5_PW, 4 * S5_STATE), pair(S5_PW, 4 * S5_STATE)],
        out_specs=pl.BlockSpec((2, 2, N, 2 * S5_STATE), lambda k: (0, 0, 0, k)),
        out_shape=jax.ShapeDtypeStruct((2, 2, N, S5_NSTATE), F32),
        compiler_params=_cp(("parallel",)),
        name="s5_state",
    )(up, wff, wfb)
    cs = []
    for d, (hs, pw, reverse) in enumerate(((hsf, pwf, False), (hsb, pwb, True))):
        cs.append(pl.pallas_call(
            functools.partial(_s5_scan_kernel, reverse=reverse, N=N),
            grid=(1,),
            in_specs=[pl.BlockSpec((None, 2, N, S5_NSTATE), lambda i, d=d: (d, 0, 0, 0)),
                      pl.BlockSpec((6, 8, S5_NSTATE), lambda i: (0, 0, 0)),
                      pl.BlockSpec((2, 8, S5_NSTATE), lambda i: (0, 0, 0))],
            out_specs=pl.BlockSpec((2, N, S5_NSTATE), lambda i: (0, 0, 0)),
            out_shape=jax.ShapeDtypeStruct((2, N, S5_NSTATE), F32),
            scratch_shapes=[pltpu.VMEM((N + 8, S5_NSTATE), F32), pltpu.VMEM((N + 8, S5_NSTATE), F32)],
            compiler_params=_cp(("arbitrary",)),
            name="s5_scan_rev" if reverse else "s5_scan_fwd",
        )(f, hs, pw))
    cspec = pl.BlockSpec((2, N, 2 * S5_STATE), lambda k: (0, 0, k))
    yflat = pl.pallas_call(
        _s5_out_kernel,
        grid=(NP,),
        in_specs=[pair(N, S5_PW), pair(S5_PW, S5_PW), cspec, cspec, pair(S5_PW, S5_PW)],
        out_specs=pair(N, S5_PW),
        out_shape=jax.ShapeDtypeStruct((NP, N, S5_PW), F32),
        compiler_params=_cp(("parallel",)),
        name="s5_out",
    )(up, toep, cs[0], cs[1], wc)
    y = yflat.reshape(NP, N, 2, TT, Hg).transpose(1, 3, 0, 2, 4).reshape(L, BW)
    row = pl.BlockSpec((RF, BW), lambda i: (i, 0))
    vec = pl.BlockSpec((1, BW), lambda i: (0, 0))
    return pl.pallas_call(
        _s5_final_kernel,
        grid=(L // RF,),
        in_specs=[row, pl.BlockSpec((RF, BW), lambda i: (i, ucol)), vec,
                  pl.BlockSpec((BW, BW), lambda i: (0, 0)), vec],
        out_specs=row,
        out_shape=jax.ShapeDtypeStruct((L, BW), BF16),
        compiler_params=_cp(("parallel",)),
        name="s5_final",
    )(y, u, d_skip.astype(F32).reshape(1, BW), w_glu.astype(BF16), b_glu.astype(F32).reshape(1, BW))


def _gdn_kernel(*refs, reverse, finalize, R, nb):
    if finalize:
        (qkv_ref, prev_ref, next_ref, cw_ref, sm_ref, smt_ref, gpr_ref, gpc_ref,
         ob_ref, z_ref, og_ref, o_ref, s_scr, q_scr, k_scr, v_scr, g_scr, b_scr, o_scr) = refs
    else:
        (qkv_ref, prev_ref, next_ref, cw_ref, sm_ref, smt_ref, gpr_ref, gpc_ref,
         o_ref, s_scr, q_scr, k_scr, v_scr, g_scr, b_scr, o_scr) = refs
    C = GDN_CHUNK
    nc = R // C
    i = pl.program_id(0)

    @pl.when(i == 0)
    def _():
        s_scr[...] = jnp.zeros_like(s_scr)

    blk = (nb - 1 - i) if reverse else i
    x = qkv_ref[...]
    pv = jnp.where(blk > 0, prev_ref[...], 0.0)
    nx = jnp.where(blk < nb - 1, next_ref[...], 0.0)
    xp = jnp.concatenate([pv, x, nx], axis=0)
    w = cw_ref[...]
    conv = (w[0:1] * xp[6:6 + R] + w[1:2] * xp[7:7 + R]
            + w[2:3] * xp[8:8 + R] + w[3:4] * xp[9:9 + R])
    act = conv * _sigmoid(conv)
    for h in range(GDN_HEADS):
        sl = slice(h * GDN_DH, (h + 1) * GDN_DH)
        qh = act[:, sl]
        q_scr[:, sl] = qh * lax.rsqrt(jnp.sum(qh * qh, axis=-1, keepdims=True) + 1e-6) * (GDN_DH ** -0.5)
        kh = act[:, BW + h * GDN_DH:BW + (h + 1) * GDN_DH]
        k_scr[:, sl] = kh * lax.rsqrt(jnp.sum(kh * kh, axis=-1, keepdims=True) + 1e-6)
    v_scr[...] = act[:, 2 * BW:3 * BW]
    sm = sm_ref[...]
    b_scr[...] = _sigmoid(sm[:, 0:4])
    g_scr[...] = gpr_ref[0:1, :] * _softplus(sm[:, 4:8] + gpr_ref[1:2, :])

    ii = lax.broadcasted_iota(jnp.int32, (C, C), 0)
    jj = lax.broadcasted_iota(jnp.int32, (C, C), 1)
    if reverse:
        incl = jj >= ii
        strict = jj > ii
    else:
        incl = jj <= ii
        strict = jj < ii
    tri_l = jnp.where(incl, 1.0, 0.0).astype(F32)
    tri_u = jnp.where((ii >= jj) if reverse else (ii <= jj), 1.0, 0.0).astype(F32)
    last = 0 if reverse else C - 1
    eye = jnp.where(ii == jj, 1.0, 0.0).astype(F32)
    pair_masks = []
    for lvl in range(6):
        bi = ii >> lvl
        bj = jj >> lvl
        lo_, hi_ = (bi, bj) if reverse else (bj, bi)
        pair_masks.append(((ii >> (lvl + 1)) == (jj >> (lvl + 1))) & ((hi_ & 1) == 1) & ((lo_ & 1) == 0))

    def chunk(cc, carry):
        cidx = (nc - 1 - cc) if reverse else cc
        r0 = pl.multiple_of(cidx * C, C)
        gch = g_scr[pl.ds(r0, C), :]
        bch = b_scr[pl.ds(r0, C), :]
        gt = gpc_ref[:, 0:1] * _softplus(smt_ref[cidx][4:8, :] + gpc_ref[:, 1:2])
        gc_col = jnp.dot(tri_l, gch, preferred_element_type=F32, precision=lax.Precision.HIGHEST)
        gc_row = jnp.dot(gt, tri_u, preferred_element_type=F32, precision=lax.Precision.HIGHEST)
        for h in range(GDN_HEADS):
            sl = slice(h * GDN_DH, (h + 1) * GDN_DH)
            gcol = gc_col[:, h:h + 1]
            grow = gc_row[h:h + 1, :]
            glast = gc_col[last:last + 1, h:h + 1]
            decay = jnp.exp(jnp.where(incl, gcol - grow, -jnp.inf))
            k = k_scr[pl.ds(r0, C), sl]
            q = q_scr[pl.ds(r0, C), sl]
            v = v_scr[pl.ds(r0, C), sl]
            beta = bch[:, h:h + 1]
            kb = k * beta
            kbf = k.astype(BF16)
            kk = _dot_nt(kb.astype(BF16), kbf)
            qk = _dot_nt(q.astype(BF16), kbf)
            a = jnp.where(strict, kk * decay, 0.0)
            attn = qk * decay
            xs = jnp.concatenate([v * beta, kb * jnp.exp(gcol)], axis=1)
            minv = eye - jnp.where(pair_masks[0], a, 0.0)
            for lvl in range(1, 6):
                mb = minv.astype(BF16)
                em = _dot(jnp.where(pair_masks[lvl], a, 0.0).astype(BF16), mb)
                minv = minv - _dot(mb, em.astype(BF16))
            xs = _dot(minv.astype(BF16), xs.astype(BF16))
            u_ = xs[:, :GDN_DH]
            w_ = xs[:, GDN_DH:]
            s_h = s_scr[h]
            sb = s_h.astype(BF16)
            v_new = u_ - _dot(w_.astype(BF16), sb)
            vnb = v_new.astype(BF16)
            o = _dot((q * jnp.exp(gcol)).astype(BF16), sb) + _dot(attn.astype(BF16), vnb)
            kdec = (k * jnp.exp(glast - gcol)).astype(BF16)
            s_scr[h] = s_h * jnp.exp(glast) + _dot_tn(kdec, vnb)
            o_scr[pl.ds(r0, C), sl] = o
        return carry

    lax.fori_loop(0, nc, chunk, 0)

    if finalize:
        o = o_scr[...] + ob_ref[...]
        z = z_ref[...]
        og = og_ref[...]
        for h in range(GDN_HEADS):
            sl = slice(h * GDN_DH, (h + 1) * GDN_DH)
            oh = o[:, sl]
            oh = oh * lax.rsqrt(jnp.mean(oh * oh, axis=-1, keepdims=True) + NORM_EPS) * og
            zh = z[:, sl]
            o_ref[:, sl] = (oh * (zh * _sigmoid(zh))).astype(o_ref.dtype)
    else:
        o_ref[...] = o_scr[...]


def gdn_direction_call(qkv, conv_w, smd, smdt, gpr, gpc, *, reverse, fin=None, R=256):
    L = qkv.shape[0]
    nb = L // R
    r8 = R // 8
    nc = R // GDN_CHUNK
    bmap = (lambda i: nb - 1 - i) if reverse else (lambda i: i)
    in_specs = [
        pl.BlockSpec((R, 3 * BW), lambda i: (bmap(i), 0)),
        pl.BlockSpec((8, 3 * BW), lambda i: (jnp.maximum(bmap(i) * r8 - 1, 0), 0)),
        pl.BlockSpec((8, 3 * BW), lambda i: (jnp.minimum((bmap(i) + 1) * r8, L // 8 - 1), 0)),
        pl.BlockSpec((4, 3 * BW), lambda i: (0, 0)),
        pl.BlockSpec((R, 8), lambda i: (bmap(i), 0)),
        pl.BlockSpec((nc, 8, GDN_CHUNK), lambda i: (bmap(i), 0, 0)),
        pl.BlockSpec((2, 4), lambda i: (0, 0)),
        pl.BlockSpec((4, 2), lambda i: (0, 0)),
    ]
    args = [qkv, qkv, qkv, conv_w, smd, smdt, gpr, gpc]
    if fin is not None:
        ob, z, og = fin
        in_specs += [
            pl.BlockSpec((R, BW), lambda i: (bmap(i), 0)),
            pl.BlockSpec((R, BW), lambda i: (bmap(i), 0)),
            pl.BlockSpec((1, GDN_DH), lambda i: (0, 0)),
        ]
        args += [ob, z, og.reshape(1, GDN_DH)]
    return pl.pallas_call(
        functools.partial(_gdn_kernel, reverse=reverse, finalize=fin is not None, R=R, nb=nb),
        grid=(nb,),
        in_specs=in_specs,
        out_specs=pl.BlockSpec((R, BW), lambda i: (bmap(i), 0)),
        out_shape=jax.ShapeDtypeStruct((L, BW), BF16 if fin is not None else F32),
        scratch_shapes=[
            pltpu.VMEM((GDN_HEADS, GDN_DH, GDN_DH), F32),
            pltpu.VMEM((R, BW), F32),
            pltpu.VMEM((R, BW), F32),
            pltpu.VMEM((R, BW), F32),
            pltpu.VMEM((R, 4), F32),
            pltpu.VMEM((R, 4), F32),
            pltpu.VMEM((R, BW), F32),
        ],
        compiler_params=_cp(("arbitrary",)),
        name="gdn_rev" if reverse else "gdn_fwd",
    )(*args)


def gdn_mixer(qkv, z, small, conv_w, a_log, dt_bias, o_gain, R=256):
    L = qkv.shape[0]
    coef = -jnp.exp(a_log.astype(F32))
    dtb = dt_bias.astype(F32)
    outs = []
    o_b = None
    for d in (1, 0):
        smd = jnp.concatenate([small[:, 4 * d:4 * d + 4], small[:, 8 + 4 * d:12 + 4 * d]], axis=1)
        smdt = smd.reshape(L // GDN_CHUNK, GDN_CHUNK, 8).transpose(0, 2, 1)
        gpr = jnp.stack([coef[d], dtb[d]])
        gpc = gpr.T
        if d == 1:
            o_b = gdn_direction_call(qkv, conv_w.astype(F32), smd, smdt, gpr, gpc, reverse=True, R=R)
        else:
            return gdn_direction_call(qkv, conv_w.astype(F32), smd, smdt, gpr, gpc, reverse=False,
                                      fin=(o_b, z, o_gain.astype(F32)), R=R)


GDN_HC = GDN_HEADS * GDN_CHUNK
GDN_UNROLL = 2


def _stack_heads(ref, r0):
    return jnp.concatenate(
        [ref[pl.ds(r0, GDN_CHUNK), h * GDN_DH:(h + 1) * GDN_DH] for h in range(GDN_HEADS)], axis=0)


def _gdn_prologue(d, blk, nb, R, qkv_ref, prev_ref, next_ref, cw_ref, sm_ref, gt_ref, gpr_ref, gpc_ref,
                  q_scr, k_scr, v_scr, b_scr, gc_scr, gr_scr, tri):
    x = qkv_ref[...]
    pv = jnp.where(blk > 0, prev_ref[...], 0.0)
    nx = jnp.where(blk < nb - 1, next_ref[...], 0.0)
    xp = jnp.concatenate([pv, x, nx], axis=0)
    w = cw_ref[...]
    conv = (w[0:1] * xp[6:6 + R] + w[1:2] * xp[7:7 + R]
            + w[2:3] * xp[8:8 + R] + w[3:4] * xp[9:9 + R])
    act = conv * _sigmoid(conv)
    for h in range(GDN_HEADS):
        sl = slice(h * GDN_DH, (h + 1) * GDN_DH)
        qh = act[:, sl]
        q_scr[d, :, sl] = qh * lax.rsqrt(jnp.sum(qh * qh, axis=-1, keepdims=True) + 1e-6) * (GDN_DH ** -0.5)
        kh = act[:, BW + h * GDN_DH:BW + (h + 1) * GDN_DH]
        k_scr[d, :, sl] = kh * lax.rsqrt(jnp.sum(kh * kh, axis=-1, keepdims=True) + 1e-6)
    v_scr[d] = act[:, 2 * BW:3 * BW]
    sm = sm_ref[...]
    b_scr[d] = _sigmoid(sm[:, 0:4])
    g = gpr_ref[2 * d:2 * d + 1, :] * _softplus(sm[:, 4:8] + gpr_ref[2 * d + 1:2 * d + 2, :])
    gt = gpc_ref[:, 2 * d:2 * d + 1] * _softplus(gt_ref[...] + gpc_ref[:, 2 * d + 1:2 * d + 2])
    tri_l, tri_u = tri
    gc_scr[d] = jnp.dot(tri_l, g, preferred_element_type=F32, precision=lax.Precision.HIGHEST)
    gr = jnp.dot(gt, tri_u, preferred_element_type=F32, precision=lax.Precision.HIGHEST)
    for c in range(R // GDN_CHUNK):
        gr_scr[d, c] = gr[:, c * GDN_CHUNK:(c + 1) * GDN_CHUNK]


def _gdn_chunks(insts, m_scr, q_scr, k_scr, v_scr, b_scr, gc_scr, gr_scr, s_scr):
    C = GDN_CHUNK
    H = GDN_HEADS
    n = len(insts)
    rng = range(n)
    ds = [t[0] for t in insts]
    r0s = [pl.multiple_of(t[1] * C, C) for t in insts]
    lasts = [0 if t[2] else C - 1 for t in insts]
    ks = [_stack_heads(k_scr.at[ds[i]], r0s[i]) for i in rng]
    qs = [_stack_heads(q_scr.at[ds[i]], r0s[i]) for i in rng]
    vs = [_stack_heads(v_scr.at[ds[i]], r0s[i]) for i in rng]
    bch = [b_scr[ds[i], pl.ds(r0s[i], C), :] for i in rng]
    gch = [gc_scr[ds[i], pl.ds(r0s[i], C), :] for i in rng]
    grw = [gr_scr[ds[i], insts[i][1]] for i in rng]
    beta = [jnp.concatenate([b[:, h:h + 1] for h in range(H)], axis=0) for b in bch]
    gcol = [jnp.concatenate([g[:, h:h + 1] for h in range(H)], axis=0) for g in gch]
    grow = [jnp.concatenate([g[h:h + 1, :] for h in range(H)], axis=1) for g in grw]
    glast_r = [jnp.concatenate(
        [jnp.broadcast_to(grw[i][h:h + 1, lasts[i]:lasts[i] + 1], (1, C)) for h in range(H)], axis=1)
        for i in rng]
    e_cat = [jnp.concatenate(
        [jnp.broadcast_to(jnp.exp(gch[i][lasts[i]:lasts[i] + 1, h:h + 1]), (1, GDN_DH)) for h in range(H)],
        axis=1) for i in rng]

    kst = [k.T for k in ks]
    kstb = [k.astype(BF16) for k in kst]
    kbs = [ks[i] * beta[i] for i in rng]
    g2 = [_dot(jnp.concatenate([kbs[i], qs[i]], axis=0).astype(BF16), kstb[i]) for i in rng]
    decay = [jnp.exp((gcol[i] - grow[i]) + m_scr[ds[i], 0]) for i in rng]
    a = [g2[i][:GDN_HC] * decay[i] * m_scr[ds[i], 1] for i in rng]
    attn = [(g2[i][GDN_HC:] * decay[i]).astype(BF16) for i in rng]
    minv = [m_scr[ds[i], 8] - a[i] * m_scr[ds[i], 2] for i in rng]
    for lvl in range(1, 6):
        mb = [m.astype(BF16) for m in minv]
        em = [_dot((a[i] * m_scr[ds[i], 2 + lvl]).astype(BF16), mb[i]).astype(BF16) for i in rng]
        minv = [minv[i] - _dot(mb[i], em[i]) for i in rng]
    egc = [jnp.exp(g) for g in gcol]
    rhs = [jnp.concatenate([vs[i] * beta[i], kbs[i] * egc[i]], axis=1).astype(BF16) for i in rng]
    xs = [_dot(minv[i].astype(BF16), rhs[i]) for i in rng]
    qg = [qs[i] * egc[i] for i in rng]
    wq = [jnp.concatenate([xs[i][:, GDN_DH:], qg[i]], axis=0).astype(BF16) for i in rng]
    kdect = [(kst[i] * jnp.exp(glast_r[i] - grow[i])).astype(BF16) for i in rng]

    zero = jnp.zeros((C, GDN_DH), BF16)
    by_dir = {}
    for i in rng:
        by_dir.setdefault(ds[i], []).append(i)
    depth = max(len(v) for v in by_dir.values())
    s_cat = {d: s_scr[d] for d in by_dir}
    for step in range(depth):
        act = [v[step] for v in by_dir.values() if len(v) > step]
        p = [_dot(wq[i], s_cat[ds[i]].astype(BF16)) for i in act]
        vn = []
        qss = []
        for j, i in enumerate(act):
            ws = jnp.concatenate(
                [p[j][h * C:(h + 1) * C, h * GDN_DH:(h + 1) * GDN_DH] for h in range(H)], axis=0)
            qss.append(jnp.concatenate(
                [p[j][GDN_HC + h * C:GDN_HC + (h + 1) * C, h * GDN_DH:(h + 1) * GDN_DH] for h in range(H)],
                axis=0))
            vn.append((xs[i][:, :GDN_DH] - ws).astype(BF16))
        o = [qss[j] + _dot(attn[i], vn[j]) for j, i in enumerate(act)]
        for j, i in enumerate(act):
            vn_bd = jnp.concatenate(
                [jnp.concatenate([vn[j][h * C:(h + 1) * C] if hh == h else zero for hh in range(H)], axis=1)
                 for h in range(H)], axis=0)
            s_cat[ds[i]] = s_cat[ds[i]] * e_cat[i] + _dot(kdect[i], vn_bd)
            o_ref = insts[i][3]
            for h in range(H):
                o_ref[pl.ds(r0s[i], C), h * GDN_DH:(h + 1) * GDN_DH] = o[j][h * C:(h + 1) * C]
    for d in by_dir:
        s_scr[d] = s_cat[d]


def _gdn_bidir_kernel(qf_ref, pf_ref, nf_ref, qb_ref, pb_ref, nb_ref, cw_ref, smf_ref, smb_ref,
                      gtf_ref, gtb_ref, gpr_ref, gpc_ref, of_ref, ob_ref,
                      s_scr, q_scr, k_scr, v_scr, b_scr, gc_scr, gr_scr, m_scr, *, R, nb):
    C = GDN_CHUNK
    nc = R // C
    i = pl.program_id(0)

    @pl.when(i == 0)
    def _():
        s_scr[...] = jnp.zeros_like(s_scr)
        ii = lax.broadcasted_iota(jnp.int32, (GDN_HC, GDN_HC), 0)
        jj = lax.broadcasted_iota(jnp.int32, (GDN_HC, GDN_HC), 1)
        blockd = (ii // C) == (jj // C)
        for d, reverse in enumerate((False, True)):
            incl = blockd & ((jj >= ii) if reverse else (jj <= ii))
            strict = blockd & ((jj > ii) if reverse else (jj < ii))
            m_scr[d, 0] = jnp.where(incl, 0.0, -jnp.inf).astype(F32)
            m_scr[d, 1] = jnp.where(strict, 1.0, 0.0).astype(F32)
            for lvl in range(6):
                bi = ii >> lvl
                bj = jj >> lvl
                lo_, hi_ = (bi, bj) if reverse else (bj, bi)
                pm = ((ii >> (lvl + 1)) == (jj >> (lvl + 1))) & ((hi_ & 1) == 1) & ((lo_ & 1) == 0)
                m_scr[d, 2 + lvl] = jnp.where(pm, 1.0, 0.0).astype(F32)
            m_scr[d, 8] = jnp.where(ii == jj, 1.0, 0.0).astype(F32)

    ri = lax.broadcasted_iota(jnp.int32, (R, R), 0)
    rj = lax.broadcasted_iota(jnp.int32, (R, R), 1)
    same = (ri // C) == (rj // C)
    tris = []
    for reverse in (False, True):
        tl = jnp.where(same & ((rj >= ri) if reverse else (rj <= ri)), 1.0, 0.0).astype(F32)
        tu = jnp.where(same & ((ri >= rj) if reverse else (ri <= rj)), 1.0, 0.0).astype(F32)
        tris.append((tl, tu))
    _gdn_prologue(0, i, nb, R, qf_ref, pf_ref, nf_ref, cw_ref, smf_ref, gtf_ref, gpr_ref, gpc_ref,
                  q_scr, k_scr, v_scr, b_scr, gc_scr, gr_scr, tris[0])
    _gdn_prologue(1, nb - 1 - i, nb, R, qb_ref, pb_ref, nb_ref, cw_ref, smb_ref, gtb_ref, gpr_ref, gpc_ref,
                  q_scr, k_scr, v_scr, b_scr, gc_scr, gr_scr, tris[1])

    def chunk(cc, carry):
        insts = []
        for uu in range(GDN_UNROLL):
            c = cc * GDN_UNROLL + uu
            insts.append((0, c, False, of_ref))
            insts.append((1, nc - 1 - c, True, ob_ref))
        _gdn_chunks(insts, m_scr, q_scr, k_scr, v_scr, b_scr, gc_scr, gr_scr, s_scr)
        return carry

    lax.fori_loop(0, nc // GDN_UNROLL, chunk, 0)


def _gdn_final_kernel(of_ref, ob_ref, z_ref, og_ref, o_ref):
    o = of_ref[...] + ob_ref[...]
    z = z_ref[...]
    og = og_ref[...]
    for h in range(GDN_HEADS):
        sl = slice(h * GDN_DH, (h + 1) * GDN_DH)
        oh = o[:, sl]
        oh = oh * lax.rsqrt(jnp.mean(oh * oh, axis=-1, keepdims=True) + NORM_EPS) * og
        zh = z[:, sl]
        o_ref[:, sl] = (oh * (zh * _sigmoid(zh))).astype(o_ref.dtype)


def gdn_mixer_fused(qkv, z, small, conv_w, a_log, dt_bias, o_gain, R=256, RF=512, zcol=0):
    L = qkv.shape[0]
    nb = L // R
    r8 = R // 8
    coef = -jnp.exp(a_log.astype(F32))
    dtb = dt_bias.astype(F32)
    gpr = jnp.stack([coef[0], dtb[0], coef[1], dtb[1]])
    gpc = gpr.T
    smf = jnp.concatenate([small[:, 0:4], small[:, 8:12]], axis=1)
    smb = jnp.concatenate([small[:, 4:8], small[:, 12:16]], axis=1)
    gtf = small[:, 8:12].T
    gtb = small[:, 12:16].T
    fmap = lambda i: i
    bmap = lambda i: nb - 1 - i
    blk = lambda m: pl.BlockSpec((R, 3 * BW), lambda i: (m(i), 0))
    prv = lambda m: pl.BlockSpec((8, 3 * BW), lambda i: (jnp.maximum(m(i) * r8 - 1, 0), 0))
    nxt = lambda m: pl.BlockSpec((8, 3 * BW), lambda i: (jnp.minimum((m(i) + 1) * r8, L // 8 - 1), 0))
    full = lambda a, b: pl.BlockSpec((a, b), lambda i: (0, 0))
    o_f, o_b = pl.pallas_call(
        functools.partial(_gdn_bidir_kernel, R=R, nb=nb),
        grid=(nb,),
        in_specs=[blk(fmap), prv(fmap), nxt(fmap), blk(bmap), prv(bmap), nxt(bmap),
                  full(4, 3 * BW),
                  pl.BlockSpec((R, 8), lambda i: (i, 0)),
                  pl.BlockSpec((R, 8), lambda i: (nb - 1 - i, 0)),
                  pl.BlockSpec((4, R), lambda i: (0, i)),
                  pl.BlockSpec((4, R), lambda i: (0, nb - 1 - i)),
                  full(4, 4), full(4, 4)],
        out_specs=[pl.BlockSpec((R, BW), lambda i: (i, 0)),
                   pl.BlockSpec((R, BW), lambda i: (nb - 1 - i, 0))],
        out_shape=[jax.ShapeDtypeStruct((L, BW), F32), jax.ShapeDtypeStruct((L, BW), F32)],
        scratch_shapes=[
            pltpu.VMEM((2, GDN_DH, BW), F32),
            pltpu.VMEM((2, R, BW), F32),
            pltpu.VMEM((2, R, BW), F32),
            pltpu.VMEM((2, R, BW), F32),
            pltpu.VMEM((2, R, 4), F32),
            pltpu.VMEM((2, R, 4), F32),
            pltpu.VMEM((2, R // GDN_CHUNK, 4, GDN_CHUNK), F32),
            pltpu.VMEM((2, 9, GDN_HC, GDN_HC), F32),
        ],
        compiler_params=_cp(("arbitrary",)),
        name="gdn_bidir",
    )(qkv, qkv, qkv, qkv, qkv, qkv, conv_w.astype(F32), smf, smb, gtf, gtb, gpr, gpc)
    row = pl.BlockSpec((RF, BW), lambda i: (i, 0))
    zrow = pl.BlockSpec((RF, BW), lambda i: (i, zcol))
    return pl.pallas_call(
        _gdn_final_kernel,
        grid=(L // RF,),
        in_specs=[row, row, zrow, pl.BlockSpec((1, GDN_DH), lambda i: (0, 0))],
        out_specs=row,
        out_shape=jax.ShapeDtypeStruct((L, BW), BF16),
        compiler_params=_cp(("parallel",)),
        name="gdn_final",
    )(o_f, o_b, z, o_gain.astype(F32).reshape(1, GDN_DH))


def _t5_bucket(rel):
    nb = T5_BUCKETS // 2
    max_exact = nb // 2
    ret = jnp.where(rel > 0, nb, 0)
    n = jnp.abs(rel)
    nf = jnp.maximum(n, 1).astype(F32)
    large = max_exact + (jnp.log(nf / max_exact) / math.log(T5_MAX_DISTANCE / max_exact)
                         * (nb - max_exact)).astype(jnp.int32)
    large = jnp.minimum(large, nb - 1)
    return ret + jnp.where(n < max_exact, n, large)


def _swa_kernel(q_ref, kp_ref, kc_ref, kn_ref, vp_ref, vc_ref, vn_ref, bias_ref, sink_ref,
                o_ref, *, nb):
    i = pl.program_id(0)
    B = SWA_BLOCK
    G = SWA_HEADS // SWA_KV_HEADS
    qi = lax.broadcasted_iota(jnp.int32, (G * B, 3 * B), 0) % B
    sj = lax.broadcasted_iota(jnp.int32, (G * B, 3 * B), 1)
    rel = sj - B - qi
    lo = jnp.where(i == 0, B, 0)
    hi = jnp.where(i == nb - 1, 2 * B, 3 * B)
    valid = (jnp.abs(rel) <= WINDOW) & (sj >= lo) & (sj < hi)
    for g in range(SWA_KV_HEADS):
        kb = jnp.concatenate([kp_ref[g], kc_ref[g], kn_ref[g]], axis=0).astype(BF16)
        vb = jnp.concatenate([vp_ref[g], vc_ref[g], vn_ref[g]], axis=0).astype(BF16)
        q = q_ref[g * G:(g + 1) * G].reshape(G * B, SWA_DH).astype(BF16)
        bias = bias_ref[g * G:(g + 1) * G].reshape(G * B, 3 * B)
        s = _dot_nt(q, kb) * (SWA_DH ** -0.5) + bias
        s = jnp.where(valid, s, -1e30)
        sink = jnp.broadcast_to(sink_ref[g * G:(g + 1) * G], (G, B, 1)).reshape(G * B, 1)
        m = jnp.maximum(jnp.max(s, axis=-1, keepdims=True), sink)
        p = jnp.exp(s - m)
        den = jnp.sum(p, axis=-1, keepdims=True) + jnp.exp(sink - m)
        o = _dot(p.astype(BF16), vb) / den
        o_ref[g * G:(g + 1) * G] = o.reshape(G, B, SWA_DH).astype(o_ref.dtype)


def swa_mixer(q, kv, sink, t5_bias):
    L = q.shape[0]
    B = SWA_BLOCK
    nb = L // B
    qh = q.reshape(L, SWA_HEADS, SWA_DH).transpose(1, 0, 2)
    kh = kv[:, :SWA_KV_HEADS * SWA_DH].reshape(L, SWA_KV_HEADS, SWA_DH).transpose(1, 0, 2)
    vh = kv[:, SWA_KV_HEADS * SWA_DH:].reshape(L, SWA_KV_HEADS, SWA_DH).transpose(1, 0, 2)
    rel = jnp.arange(3 * B)[None, :] - B - jnp.arange(B)[:, None]
    bias = t5_bias.astype(F32)[_t5_bucket(rel)].transpose(2, 0, 1)
    kspec = lambda f: pl.BlockSpec((SWA_KV_HEADS, B, SWA_DH), lambda i: (0, f(i), 0))
    prev = lambda i: jnp.maximum(i - 1, 0)
    cur = lambda i: i
    nxt = lambda i: jnp.minimum(i + 1, nb - 1)
    out = pl.pallas_call(
        functools.partial(_swa_kernel, nb=nb),
        grid=(nb,),
        in_specs=[
            pl.BlockSpec((SWA_HEADS, B, SWA_DH), lambda i: (0, i, 0)),
            kspec(prev), kspec(cur), kspec(nxt),
            kspec(prev), kspec(cur), kspec(nxt),
            pl.BlockSpec((SWA_HEADS, B, 3 * B), lambda i: (0, 0, 0)),
            pl.BlockSpec((SWA_HEADS, 1, 1), lambda i: (0, 0, 0)),
        ],
        out_specs=pl.BlockSpec((SWA_HEADS, B, SWA_DH), lambda i: (0, i, 0)),
        out_shape=jax.ShapeDtypeStruct((SWA_HEADS, L, SWA_DH), BF16),
        compiler_params=_cp(("parallel",)),
        name="swa",
    )(qh, kh, kh, kh, vh, vh, vh, bias, sink.astype(F32).reshape(SWA_HEADS, 1, 1))
    return out.transpose(1, 0, 2).reshape(L, SWA_HEADS * SWA_DH)


SWA_G = SWA_HEADS // SWA_KV_HEADS
SWA_VT_ROWS = SWA_DH + 16


def _swa_t_kernel(q_ref, kvp_ref, kvc_ref, kvn_ref, bias_ref, sink_ref, o_ref, *, nb):
    i = pl.program_id(0)
    B = SWA_BLOCK
    KD = SWA_KV_HEADS * SWA_DH
    sj = lax.broadcasted_iota(jnp.int32, (3 * B, SWA_G * B), 0)
    qi = lax.broadcasted_iota(jnp.int32, (3 * B, SWA_G * B), 1) % B
    rel = sj - B - qi
    lo = jnp.where(i == 0, B, 0)
    hi = jnp.where(i == nb - 1, 2 * B, 3 * B)
    valid = (jnp.abs(rel) <= WINDOW) & (sj >= lo) & (sj < hi)
    qt = (q_ref[...] * (SWA_DH ** -0.5)).T
    kb = jnp.concatenate([kvp_ref[:, 0:KD], kvc_ref[:, 0:KD], kvn_ref[:, 0:KD]], axis=0).astype(BF16)
    vt = jnp.concatenate([kvp_ref[:, KD:2 * KD].T, kvc_ref[:, KD:2 * KD].T, kvn_ref[:, KD:2 * KD].T],
                         axis=1).astype(BF16)
    zq = jnp.zeros((SWA_DH, SWA_G * B), F32)
    ones = jnp.ones((SWA_VT_ROWS - SWA_DH, 3 * B), BF16)
    outs = []
    for g in range(SWA_KV_HEADS):
        qg = jnp.concatenate([qt[(g * SWA_G + hh) * SWA_DH:(g * SWA_G + hh + 1) * SWA_DH, :]
                              for hh in range(SWA_G)], axis=1)
        qpad = jnp.concatenate([qg if gg == g else zq for gg in range(SWA_KV_HEADS)], axis=0).astype(BF16)
        s = _dot(kb, qpad) + bias_ref[g]
        s = jnp.where(valid, s, -1e30)
        sink = sink_ref[g]
        m = jnp.maximum(jnp.max(s, axis=0, keepdims=True), sink)
        p = jnp.exp(s - m).astype(BF16)
        vg = jnp.concatenate([vt[g * SWA_DH:(g + 1) * SWA_DH, :], ones], axis=0)
        pv = _dot(vg, p)
        den = pv[SWA_DH:SWA_DH + 1] + jnp.exp(sink - m)
        o = pv[0:SWA_DH] / den
        outs += [o[:, hh * B:(hh + 1) * B] for hh in range(SWA_G)]
    ot = jnp.concatenate(outs, axis=0)
    o_ref[...] = ot.T.astype(o_ref.dtype)


def swa_mixer_t(q, kv, sink, t5_bias, qcol=0, kvcol=0):
    L = q.shape[0]
    B = SWA_BLOCK
    nb = L // B
    rel = jnp.arange(3 * B)[None, :] - B - jnp.arange(B)[:, None]
    onehot = (_t5_bucket(rel)[..., None] == jnp.arange(T5_BUCKETS)).astype(F32)
    bias = jnp.einsum('qsb,bh->hqs', onehot, t5_bias.astype(F32), precision=lax.Precision.HIGHEST)
    bias_t = bias.reshape(SWA_KV_HEADS, SWA_G, B, 3 * B).transpose(0, 3, 1, 2).reshape(
        SWA_KV_HEADS, 3 * B, SWA_G * B)
    sink_r = jnp.broadcast_to(sink.astype(F32).reshape(SWA_KV_HEADS, 1, SWA_G, 1),
                              (SWA_KV_HEADS, 1, SWA_G, B)).reshape(SWA_KV_HEADS, 1, SWA_G * B)
    kvspec = lambda f: pl.BlockSpec((B, 2 * SWA_KV_HEADS * SWA_DH), lambda i: (f(i), kvcol))
    return pl.pallas_call(
        functools.partial(_swa_t_kernel, nb=nb),
        grid=(nb,),
        in_specs=[
            pl.BlockSpec((B, SWA_HEADS * SWA_DH), lambda i: (i, qcol)),
            kvspec(lambda i: jnp.maximum(i - 1, 0)), kvspec(lambda i: i),
            kvspec(lambda i: jnp.minimum(i + 1, nb - 1)),
            pl.BlockSpec((SWA_KV_HEADS, 3 * B, SWA_G * B), lambda i: (0, 0, 0)),
            pl.BlockSpec((SWA_KV_HEADS, 1, SWA_G * B), lambda i: (0, 0, 0)),
        ],
        out_specs=pl.BlockSpec((B, SWA_HEADS * SWA_DH), lambda i: (i, 0)),
        out_shape=jax.ShapeDtypeStruct((L, SWA_HEADS * SWA_DH), BF16),
        compiler_params=_cp(("parallel",)),
        name="swa",
    )(q, kv, kv, kv, bias_t, sink_r)


def _mla_prep_kernel(cq_ref, ckv_ref, kr_ref, krr_ref, cos_ref, sin_ref, qg_ref, kvg_ref,
                     wqn_ref, wqp_ref, wqr_ref, wkv_ref, q_ref, k_ref, v_ref):
    cq = cq_ref[...]
    qn = (cq * lax.rsqrt(jnp.mean(cq * cq, axis=-1, keepdims=True) + NORM_EPS) * qg_ref[...]).astype(BF16)
    ckv = ckv_ref[...]
    kn = (ckv * lax.rsqrt(jnp.mean(ckv * ckv, axis=-1, keepdims=True) + NORM_EPS) * kvg_ref[...]).astype(BF16)
    cos = cos_ref[...]
    sin = sin_ref[...]
    q_nope = _dot(qn, wqn_ref[...])
    q_pe = _dot(qn, wqp_ref[...])
    q_rot = _dot(qn, wqr_ref[...])
    kv = _dot(kn, wkv_ref[...])
    k_pe = (kr_ref[...] * cos + krr_ref[...] * sin).astype(BF16)
    for h in range(MLA_HEADS):
        sl = slice(h * 128, (h + 1) * 128)
        q_ref[h, :, 0:128] = q_nope[:, sl].astype(BF16)
        q_ref[h, :, 128:256] = (q_pe[:, sl] * cos + q_rot[:, sl] * sin).astype(BF16)
        k_ref[h, :, 0:128] = kv[:, h * 256:h * 256 + 128].astype(BF16)
        k_ref[h, :, 128:256] = k_pe
        v_ref[h] = kv[:, h * 256 + 128:(h + 1) * 256].astype(BF16)


def _mla_attn_kernel(q_ref, k_ref, v_ref, o_ref, *, tk, nk):
    q = q_ref[0]
    tq = q.shape[0]
    scale = (MLA_NOPE + MLA_ROPE) ** -0.5

    def body(kc, carry):
        m, l, acc = carry
        k0 = pl.multiple_of(kc * tk, tk)
        k = k_ref[0, pl.ds(k0, tk), :]
        v = v_ref[0, pl.ds(k0, tk), :]
        s = _dot_nt(q, k) * scale
        m_new = jnp.maximum(m, jnp.max(s, axis=-1, keepdims=True))
        alpha = jnp.exp(m - m_new)
        p = jnp.exp(s - m_new)
        l = alpha * l + jnp.sum(p, axis=-1, keepdims=True)
        acc = alpha * acc + _dot(p.astype(BF16), v)
        return m_new, l, acc

    m0 = jnp.full((tq, 1), -jnp.inf, F32)
    l0 = jnp.zeros((tq, 1), F32)
    a0 = jnp.zeros((tq, MLA_V), F32)
    m, l, acc = lax.fori_loop(0, nk, body, (m0, l0, a0))
    o_ref[...] = (acc / l).astype(o_ref.dtype)


def mla_mixer(c_q, c_kv, kr_pad, krr_pad, q_gain, kv_gain, w_uq, w_ukv, R=512, tq=512, tk=512):
    L = c_q.shape[0]
    H = MLA_HEADS
    half = MLA_ROPE // 2
    pos = jnp.arange(L, dtype=F32)
    inv_freq = ROPE_THETA ** (-jnp.arange(0, MLA_ROPE, 2, dtype=F32) / MLA_ROPE)
    ang = pos[:, None] * inv_freq[None, :]
    cos, sin = jnp.cos(ang), jnp.sin(ang)
    zpad = jnp.zeros((L, 128 - MLA_ROPE), F32)
    cos_t = jnp.concatenate([cos, cos, zpad], axis=1)
    sin_t = jnp.concatenate([-sin, sin, zpad], axis=1)
    wq = w_uq.reshape(MLA_Q_RANK, H, MLA_NOPE + MLA_ROPE)
    wqn = wq[:, :, :MLA_NOPE].reshape(MLA_Q_RANK, H * MLA_NOPE).astype(BF16)
    wpe = wq[:, :, MLA_NOPE:]
    zw = jnp.zeros((MLA_Q_RANK, H, 128 - MLA_ROPE), w_uq.dtype)
    wqp = jnp.concatenate([wpe, zw], axis=2).reshape(MLA_Q_RANK, H * 128).astype(BF16)
    wrot = jnp.concatenate([wpe[:, :, half:], wpe[:, :, :half], zw], axis=2)
    wqr = wrot.reshape(MLA_Q_RANK, H * 128).astype(BF16)
    wkv = w_ukv.astype(BF16)
    row = lambda w: pl.BlockSpec((R, w), lambda i: (i, 0))
    full = lambda a, b: pl.BlockSpec((a, b), lambda i: (0, 0))
    q, k, v = pl.pallas_call(
        _mla_prep_kernel,
        grid=(L // R,),
        in_specs=[row(MLA_Q_RANK), row(MLA_KV_RANK), row(128), row(128), row(128), row(128),
                  full(1, MLA_Q_RANK), full(1, MLA_KV_RANK),
                  full(MLA_Q_RANK, H * 128), full(MLA_Q_RANK, H * 128), full(MLA_Q_RANK, H * 128),
                  full(MLA_KV_RANK, H * 256)],
        out_specs=[pl.BlockSpec((H, R, 256), lambda i: (0, i, 0)),
                   pl.BlockSpec((H, R, 256), lambda i: (0, i, 0)),
                   pl.BlockSpec((H, R, 128), lambda i: (0, i, 0))],
        out_shape=[jax.ShapeDtypeStruct((H, L, 256), BF16),
                   jax.ShapeDtypeStruct((H, L, 256), BF16),
                   jax.ShapeDtypeStruct((H, L, 128), BF16)],
        compiler_params=_cp(("parallel",)),
        name="mla_prep",
    )(c_q, c_kv, kr_pad, krr_pad, cos_t, sin_t,
      q_gain.astype(F32).reshape(1, -1), kv_gain.astype(F32).reshape(1, -1), wqn, wqp, wqr, wkv)
    return pl.pallas_call(
        functools.partial(_mla_attn_kernel, tk=tk, nk=L // tk),
        grid=(H, L // tq),
        in_specs=[pl.BlockSpec((1, tq, 256), lambda h, i: (h, i, 0)),
                  pl.BlockSpec((1, L, 256), lambda h, i: (h, 0, 0)),
                  pl.BlockSpec((1, L, 128), lambda h, i: (h, 0, 0))],
        out_specs=pl.BlockSpec((tq, MLA_V), lambda h, i: (i, h)),
        out_shape=jax.ShapeDtypeStruct((L, H * MLA_V), BF16),
        compiler_params=_cp(("parallel", "parallel")),
        name="mla_attn",
    )(q, k, v)


MLA_VT_ROWS = MLA_V + 16
LOG2E = 1.4426950408889634


def _mla_prep_t_kernel(cq_ref, ckv_ref, kr_ref, krr_ref, cos_ref, sin_ref, cost_ref, sint_ref,
                       qg_ref, kvg_ref, wqn_ref, wqp_ref, wqr_ref, wk_ref, wv_ref,
                       qt_ref, k_ref, vt_ref):
    cq = cq_ref[...]
    qn = (cq * lax.rsqrt(jnp.mean(cq * cq, axis=-1, keepdims=True) + NORM_EPS) * qg_ref[...]).astype(BF16)
    ckv = ckv_ref[...]
    kn = (ckv * lax.rsqrt(jnp.mean(ckv * ckv, axis=-1, keepdims=True) + NORM_EPS) * kvg_ref[...]).astype(BF16)
    c = (MLA_NOPE + MLA_ROPE) ** -0.5 * LOG2E
    qt_nope = _dot_nt(wqn_ref[...], qn)
    qt_pe = _dot_nt(wqp_ref[...], qn)
    qt_rot = _dot_nt(wqr_ref[...], qn)
    cost = cost_ref[...]
    sint = sint_ref[...]
    k_nope = _dot(kn, wk_ref[...])
    vt = _dot_nt(wv_ref[...], kn)
    k_pe = (kr_ref[...] * cos_ref[...] + krr_ref[...] * sin_ref[...]).astype(BF16)
    ones = jnp.ones((MLA_VT_ROWS - MLA_V, vt.shape[1]), BF16)
    for h in range(MLA_HEADS):
        sl = slice(h * 128, (h + 1) * 128)
        qt_ref[h, 0:128, :] = (qt_nope[sl] * c).astype(BF16)
        qt_ref[h, 128:256, :] = ((qt_pe[sl] * cost + qt_rot[sl] * sint) * c).astype(BF16)
        k_ref[h, :, 0:128] = k_nope[:, sl].astype(BF16)
        k_ref[h, :, 128:256] = k_pe
        vt_ref[h, 0:MLA_V, :] = vt[sl].astype(BF16)
        vt_ref[h, MLA_V:MLA_VT_ROWS, :] = ones


def _mla_attn_t_kernel(qt_ref, k_ref, vt_ref, o_ref, s_scr, *, tk, nk, tsub):
    tq = qt_ref.shape[2]
    nsub = tq // tsub
    subs = range(nsub)
    qts = [qt_ref[0, :, j * tsub:(j + 1) * tsub] for j in subs]

    def put_scores(kc, slot):
        k0 = pl.multiple_of(kc * tk, tk)
        kblk = k_ref[0, pl.ds(k0, tk), :]
        for j in subs:
            s_scr[slot, j] = _dot(kblk, qts[j])

    def consume(kc, slot, ms, accs):
        k0 = pl.multiple_of(kc * tk, tk)
        vt = vt_ref[0, :, pl.ds(k0, tk)]
        ss = [s_scr[slot, j] for j in subs]
        m_new = [jnp.maximum(ms[j], jnp.max(ss[j], axis=0, keepdims=True)) for j in subs]
        ps = [jnp.exp2(ss[j] - m_new[j]).astype(BF16) for j in subs]
        pv = [_dot(vt, ps[j]) for j in subs]
        accs = [jnp.exp2(ms[j] - m_new[j]) * accs[j] + pv[j] for j in subs]
        return m_new, accs

    def body(kk, carry):
        ms, accs = carry
        kc = 2 * kk
        put_scores(kc + 1, 1)
        ms, accs = consume(kc, 0, list(ms), list(accs))
        put_scores(jnp.minimum(kc + 2, nk - 1), 0)
        ms, accs = consume(kc + 1, 1, ms, accs)
        return tuple(ms), tuple(accs)

    m0 = tuple(jnp.full((1, tsub), -jnp.inf, F32) for _ in subs)
    a0 = tuple(jnp.zeros((MLA_VT_ROWS, tsub), F32) for _ in subs)
    put_scores(0, 0)
    ms, accs = lax.fori_loop(0, nk // 2, body, (m0, a0))
    for j in subs:
        o = accs[j][0:MLA_V] / accs[j][MLA_V:MLA_V + 1]
        o_ref[j * tsub:(j + 1) * tsub, :] = o.T.astype(o_ref.dtype)


def _rope_tables(L):
    pos = np.arange(L, dtype=np.float32)
    inv_freq = np.float32(ROPE_THETA) ** (-np.arange(0, MLA_ROPE, 2, dtype=np.float32) / np.float32(MLA_ROPE))
    ang = pos[:, None] * inv_freq.astype(np.float32)[None, :]
    return jnp.asarray(np.cos(ang), F32), jnp.asarray(np.sin(ang), F32)


def mla_mixer_t(c_q, c_kv, kr_pad, krr_pad, q_gain, kv_gain, w_uq, w_ukv, R=512, tq=1024, tk=512, tsub=256,
                cols=(0, 0, 0, 0)):
    L = c_q.shape[0]
    H = MLA_HEADS
    half = MLA_ROPE // 2
    cos, sin = _rope_tables(L)
    zpad = jnp.zeros((L, 128 - MLA_ROPE), F32)
    cos_t = jnp.concatenate([cos, cos, zpad], axis=1)
    sin_t = jnp.concatenate([-sin, sin, zpad], axis=1)
    wq = w_uq.reshape(MLA_Q_RANK, H, MLA_NOPE + MLA_ROPE)
    wqn = wq[:, :, :MLA_NOPE].reshape(MLA_Q_RANK, H * MLA_NOPE).T.astype(BF16)
    wpe = wq[:, :, MLA_NOPE:]
    zw = jnp.zeros((MLA_Q_RANK, H, 128 - MLA_ROPE), w_uq.dtype)
    wqp = jnp.concatenate([wpe, zw], axis=2).reshape(MLA_Q_RANK, H * 128).T.astype(BF16)
    wrot = jnp.concatenate([wpe[:, :, half:], wpe[:, :, :half], zw], axis=2)
    wqr = wrot.reshape(MLA_Q_RANK, H * 128).T.astype(BF16)
    wkv = w_ukv.reshape(MLA_KV_RANK, H, MLA_NOPE + MLA_V)
    wk = wkv[:, :, :MLA_NOPE].reshape(MLA_KV_RANK, H * MLA_NOPE).astype(BF16)
    wv = wkv[:, :, MLA_NOPE:].reshape(MLA_KV_RANK, H * MLA_V).T.astype(BF16)
    row = lambda w, c=0: pl.BlockSpec((R, w), lambda i: (i, c))
    col = lambda w: pl.BlockSpec((w, R), lambda i: (0, i))
    full = lambda a, b: pl.BlockSpec((a, b), lambda i: (0, 0))
    qt, k, vt = pl.pallas_call(
        _mla_prep_t_kernel,
        grid=(L // R,),
        in_specs=[row(MLA_Q_RANK, cols[0]), row(MLA_KV_RANK, cols[1]), row(128, cols[2]), row(128, cols[3]),
                  row(128), row(128), col(128), col(128),
                  full(1, MLA_Q_RANK), full(1, MLA_KV_RANK),
                  full(H * 128, MLA_Q_RANK), full(H * 128, MLA_Q_RANK), full(H * 128, MLA_Q_RANK),
                  full(MLA_KV_RANK, H * 128), full(H * 128, MLA_KV_RANK)],
        out_specs=[pl.BlockSpec((H, 256, R), lambda i: (0, 0, i)),
                   pl.BlockSpec((H, R, 256), lambda i: (0, i, 0)),
                   pl.BlockSpec((H, MLA_VT_ROWS, R), lambda i: (0, 0, i))],
        out_shape=[jax.ShapeDtypeStruct((H, 256, L), BF16),
                   jax.ShapeDtypeStruct((H, L, 256), BF16),
                   jax.ShapeDtypeStruct((H, MLA_VT_ROWS, L), BF16)],
        compiler_params=_cp(("parallel",)),
        name="mla_prep",
    )(c_q, c_kv, kr_pad, krr_pad, cos_t, sin_t, cos_t.T, sin_t.T,
      q_gain.astype(F32).reshape(1, -1), kv_gain.astype(F32).reshape(1, -1), wqn, wqp, wqr, wk, wv)
    return pl.pallas_call(
        functools.partial(_mla_attn_t_kernel, tk=tk, nk=L // tk, tsub=tsub),
        grid=(H, L // tq),
        in_specs=[pl.BlockSpec((1, 256, tq), lambda h, i: (h, 0, i)),
                  pl.BlockSpec((1, L, 256), lambda h, i: (h, 0, 0)),
                  pl.BlockSpec((1, MLA_VT_ROWS, L), lambda h, i: (h, 0, 0))],
        out_specs=pl.BlockSpec((tq, MLA_V), lambda h, i: (i, h)),
        out_shape=jax.ShapeDtypeStruct((L, H * MLA_V), BF16),
        scratch_shapes=[pltpu.VMEM((2, tq // tsub, tk, tsub), F32)],
        compiler_params=_cp(("parallel", "parallel")),
        name="mla_attn",
    )(qt, k, vt)


def _merge_kernel(h_ref, ya_ref, yb_ref, yc_ref, yd_ref, g0_ref, g1_ref, g2_ref, g3_ref, p_ref, o_ref):
    h = h_ref[...]
    acc = None
    for b, (y_ref, g_ref) in enumerate(((ya_ref, g0_ref), (yb_ref, g1_ref), (yc_ref, g2_ref), (yd_ref, g3_ref))):
        gate = _sigmoid(_dot(h, g_ref[...]))
        t = gate * _dot(y_ref[...], p_ref[b])
        acc = t if acc is None else acc + t
    o_ref[...] = acc.astype(o_ref.dtype)


def merge_branches(h, ys, w_gate, w_branch, tm=1024, tn=256):
    L = h.shape[0]
    nj = D_MODEL // tn
    gspec = lambda b: pl.BlockSpec((D_MODEL, tn), lambda i, j: (0, b * nj + j))
    yspec = pl.BlockSpec((tm, BW), lambda i, j: (i, 0))
    return pl.pallas_call(
        _merge_kernel,
        grid=(L // tm, nj),
        in_specs=[pl.BlockSpec((tm, D_MODEL), lambda i, j: (i, 0)), yspec, yspec, yspec, yspec,
                  gspec(0), gspec(1), gspec(2), gspec(3),
                  pl.BlockSpec((4, BW, tn), lambda i, j: (0, 0, j))],
        out_specs=pl.BlockSpec((tm, tn), lambda i, j: (i, j)),
        out_shape=jax.ShapeDtypeStruct((L, D_MODEL), BF16),
        compiler_params=_cp(("parallel", "arbitrary")),
        name="merge",
    )(h, *ys, w_gate, w_gate, w_gate, w_gate, w_branch)


def _out_proj_kernel(x_ref, m_ref, w_ref, g_ref, o_ref):
    f = _dot(m_ref[...], w_ref[...])
    f = f * lax.rsqrt(jnp.mean(f * f, axis=-1, keepdims=True) + NORM_EPS) * g_ref[...]
    o_ref[...] = x_ref[...] + f


def out_proj(x, m, w_out, gain, tm=512):
    L = x.shape[0]
    return pl.pallas_call(
        _out_proj_kernel,
        grid=(L // tm,),
        in_specs=[pl.BlockSpec((tm, D_MODEL), lambda i: (i, 0)),
                  pl.BlockSpec((tm, D_MODEL), lambda i: (i, 0)),
                  pl.BlockSpec((D_MODEL, D_MODEL), lambda i: (0, 0)),
                  pl.BlockSpec((1, D_MODEL), lambda i: (0, 0))],
        out_specs=pl.BlockSpec((tm, D_MODEL), lambda i: (i, 0)),
        out_shape=jax.ShapeDtypeStruct((L, D_MODEL), F32),
        compiler_params=_cp(("parallel",)),
        name="out_proj",
    )(x, m, w_out, gain.reshape(1, D_MODEL))


def _mlp_kernel(x_ref, g1_ref, w1_ref, w2_ref, g2_ref, o_ref, h_scr, acc):
    j = pl.program_id(1)

    @pl.when(j == 0)
    def _():
        x = x_ref[...]
        ms = jnp.mean(x * x, axis=-1, keepdims=True)
        h_scr[...] = (x * lax.rsqrt(ms + NORM_EPS) * g1_ref[...]).astype(BF16)
        acc[...] = jnp.zeros_like(acc)

    a = _dot(h_scr[...], w1_ref[...])
    a = jnp.maximum(a, 0.0)
    a = (a * a).astype(BF16)
    acc[...] += _dot(a, w2_ref[...])

    @pl.when(j == pl.num_programs(1) - 1)
    def _():
        f = acc[...]
        f = f * lax.rsqrt(jnp.mean(f * f, axis=-1, keepdims=True) + NORM_EPS) * g2_ref[...]
        o_ref[...] = x_ref[...] + f


def mlp(x, g1, w1, w2, g2, tm=512, tf=512):
    L = x.shape[0]
    return pl.pallas_call(
        _mlp_kernel,
        grid=(L // tm, D_FF // tf),
        in_specs=[pl.BlockSpec((tm, D_MODEL), lambda i, j: (i, 0)),
                  pl.BlockSpec((1, D_MODEL), lambda i, j: (0, 0)),
                  pl.BlockSpec((D_MODEL, tf), lambda i, j: (0, j)),
                  pl.BlockSpec((tf, D_MODEL), lambda i, j: (j, 0)),
                  pl.BlockSpec((1, D_MODEL), lambda i, j: (0, 0))],
        out_specs=pl.BlockSpec((tm, D_MODEL), lambda i, j: (i, 0)),
        out_shape=jax.ShapeDtypeStruct((L, D_MODEL), F32),
        scratch_shapes=[pltpu.VMEM((tm, D_MODEL), BF16), pltpu.VMEM((tm, D_MODEL), F32)],
        compiler_params=_cp(("parallel", "arbitrary")),
        name="mlp",
    )(x, g1.reshape(1, D_MODEL), w1, w2, g2.reshape(1, D_MODEL))


def _mlp_w32_kernel(x_ref, g1_ref, w1_ref, w2_ref, g2_ref, o_ref, h_scr):
    j = pl.program_id(1)

    @pl.when(j == 0)
    def _():
        x = x_ref[...]
        ms = jnp.mean(x * x, axis=-1, keepdims=True)
        h_scr[...] = (x * lax.rsqrt(ms + NORM_EPS) * g1_ref[...]).astype(BF16)
        o_ref[...] = jnp.zeros_like(o_ref)

    a = _dot(h_scr[...], w1_ref[...].astype(BF16))
    a = jnp.maximum(a, 0.0)
    a = (a * a).astype(BF16)
    o_ref[...] += _dot(a, w2_ref[...].astype(BF16))

    @pl.when(j == pl.num_programs(1) - 1)
    def _():
        f = o_ref[...]
        f = f * lax.rsqrt(jnp.mean(f * f, axis=-1, keepdims=True) + NORM_EPS) * g2_ref[...]
        o_ref[...] = x_ref[...] + f


def mlp_w32(x, g1, w1_all, w2_all, g2, layer, tm=1024, tf=256):
    L = x.shape[0]
    return pl.pallas_call(
        _mlp_w32_kernel,
        grid=(L // tm, D_FF // tf),
        in_specs=[pl.BlockSpec((tm, D_MODEL), lambda i, j: (i, 0)),
                  pl.BlockSpec((1, D_MODEL), lambda i, j: (0, 0)),
                  pl.BlockSpec((None, D_MODEL, tf), lambda i, j: (layer, 0, j)),
                  pl.BlockSpec((None, tf, D_MODEL), lambda i, j: (layer, j, 0)),
                  pl.BlockSpec((1, D_MODEL), lambda i, j: (0, 0))],
        out_specs=pl.BlockSpec((tm, D_MODEL), lambda i, j: (i, 0)),
        out_shape=jax.ShapeDtypeStruct((L, D_MODEL), F32),
        scratch_shapes=[pltpu.VMEM((tm, D_MODEL), BF16)],
        compiler_params=_cp(("parallel", "arbitrary")),
        name="mlp",
    )(x, g1.reshape(1, D_MODEL), w1_all, w2_all, g2.reshape(1, D_MODEL))


def _mix_weight(w):
    half = MLA_ROPE // 2
    z = lambda n: jnp.zeros((D_MODEL, n), w.dtype)
    kr = w[:, _OFF_KR:_OFF_KR + MLA_ROPE]
    cols = [
        w[:, _OFF_QKV:_OFF_QKV + 3 * BW],
        w[:, _OFF_U:_OFF_U + BW],
        w[:, _OFF_Z:_OFF_Z + BW],
        w[:, _OFF_SWAQ:_OFF_SWAQ + BW],
        w[:, _OFF_CKV:_OFF_CKV + MLA_KV_RANK],
        w[:, _OFF_SWAKV:_OFF_SWAKV + 256],
        w[:, _OFF_CQ:_OFF_CQ + MLA_Q_RANK],
        kr, z(64),
        kr[:, half:], kr[:, :half], z(64),
        w[:, _OFF_BETA:_OFF_BETA + 16], z(112),
    ]
    return jnp.concatenate(cols, axis=1).astype(BF16)


def kernel(x, w_in, s5_lam_re, s5_lam_im, s5_log_step, s5_b_re, s5_b_im, s5_c_re, s5_c_im, s5_d, s5_w_glu, s5_b_glu, gdn_conv, gdn_a_log, gdn_dt_bias, gdn_o_gain, swa_sink, t5_bias, mla_q_gain, mla_kv_gain, mla_w_uq, mla_w_ukv, w_branch, w_out, mix_pre_gain, mix_post_gain, mlp_pre_gain, mlp_post_gain, w_mlp_in, w_mlp_out):
    B_, L, _ = x.shape
    outs = []
    for b in range(B_):
        xb = x[b].astype(F32)
        for l in range(DEPTH):
            w_mix = _mix_weight(w_in[l])
            w_gate = w_in[l][:, _OFF_GATE:].astype(BF16)
            proj, h = in_proj(xb, mix_pre_gain[l].astype(F32), w_mix, tm=1024)
            y_a = s5_mixer_tiled(proj, s5_lam_re[l], s5_lam_im[l], s5_log_step[l],
                                 s5_b_re[l], s5_b_im[l], s5_c_re[l], s5_c_im[l], s5_d[l], s5_w_glu[l],
                                 s5_b_glu[l], ucol=MIX_U // BW)
            y_b = gdn_mixer_fused(proj, proj, proj[:, MIX_SMALL:MIX_SMALL + 16], gdn_conv[l], gdn_a_log[l],
                                  gdn_dt_bias[l], gdn_o_gain[l], zcol=MIX_Z // BW)
            y_c = swa_mixer_t(proj, proj, swa_sink[l], t5_bias, qcol=MIX_SWAQ // BW, kvcol=MIX_SWAKV // 256)
            y_d = mla_mixer_t(proj, proj, proj, proj, mla_q_gain[l], mla_kv_gain[l], mla_w_uq[l], mla_w_ukv[l],
                              cols=(MIX_CQ // MLA_Q_RANK, MIX_CKV // MLA_KV_RANK, MIX_KR // 128,
                                    MIX_KRROT // 128))
            merged = merge_branches(h, (y_a, y_b, y_c, y_d), w_gate, w_branch[l].astype(BF16))
            xb = out_proj(xb, merged, w_out[l].astype(BF16), mix_post_gain[l].astype(F32))
            xb = mlp_w32(xb, mlp_pre_gain[l].astype(F32), w_mlp_in.astype(F32), w_mlp_out.astype(F32),
                         mlp_post_gain[l].astype(F32), l, tm=min(1024, L))
        outs.append(xb)
    return jnp.stack(outs).astype(x.dtype)
```

```python
import functools
import math

import numpy as np
import jax
import jax.numpy as jnp
from jax import lax
from jax.experimental import pallas as pl
from jax.experimental.pallas import tpu as pltpu

F32 = jnp.float32
BF16 = jnp.bfloat16

D_MODEL = 2048
DEPTH = 4
BW = 512
D_FF = 4 * D_MODEL
NORM_EPS = 1e-6

S5_GROUP = 16
S5_GROUPS = 32
S5_STATE = 64
S5_NSTATE = S5_GROUPS * S5_STATE
S5_HALF = S5_NSTATE // 2

GDN_HEADS = 4
GDN_DH = 128
GDN_CHUNK = 64

SWA_HEADS = 8
SWA_KV_HEADS = 2
SWA_DH = 64
SWA_BLOCK = 128
WINDOW = 128
T5_BUCKETS = 32
T5_MAX_DISTANCE = 128

MLA_HEADS = 4
MLA_Q_RANK = 384
MLA_KV_RANK = 512
MLA_NOPE = 128
MLA_ROPE = 64
MLA_V = 128
ROPE_THETA = 10000.0

_OFF_U = 0
_OFF_QKV = 512
_OFF_Z = 2048
_OFF_BETA = 2560
_OFF_DECAY = 2568
_OFF_SWAQ = 2576
_OFF_SWAKV = 3088
_OFF_CQ = 3344
_OFF_CKV = 3728
_OFF_KR = 4240
_OFF_GATE = 4304
D_IN = _OFF_GATE + 4 * D_MODEL

MIX_QKV = 0
MIX_U = 1536
MIX_Z = 2048
MIX_SWAQ = 2560
MIX_CKV = 3072
MIX_SWAKV = 3584
MIX_CQ = 3840
MIX_KR = 4224
MIX_KRROT = 4352
MIX_SMALL = 4480
N_MIX = 4608

VMEM_LIMIT = 56 * 1024 * 1024


def _cp(sem, limit=VMEM_LIMIT):
    return pltpu.CompilerParams(dimension_semantics=sem, vmem_limit_bytes=limit)


def _dot(a, b):
    return jnp.dot(a, b, preferred_element_type=F32)


def _dot_nt(a, b):
    return lax.dot_general(a, b, (((1,), (1,)), ((), ())), preferred_element_type=F32)


def _dot_tn(a, b):
    return lax.dot_general(a, b, (((0,), (0,)), ((), ())), preferred_element_type=F32)


def _sigmoid(x):
    return 1.0 / (1.0 + jnp.exp(-x))


def _softplus(x):
    return jnp.maximum(x, 0.0) + jnp.log(1.0 + jnp.exp(-jnp.abs(x)))


def _in_proj_kernel(x_ref, g_ref, w_ref, o_ref, h_ref, h_scr):
    @pl.when(pl.program_id(1) == 0)
    def _():
        x = x_ref[...]
        ms = jnp.mean(x * x, axis=-1, keepdims=True)
        h = (x * lax.rsqrt(ms + NORM_EPS) * g_ref[...]).astype(BF16)
        h_scr[...] = h
        h_ref[...] = h

    o_ref[...] = _dot(h_scr[...], w_ref[...])


def in_proj(x, gain, w_mix, tm=512, tn=512):
    L = x.shape[0]
    n = w_mix.shape[1]
    return pl.pallas_call(
        _in_proj_kernel,
        grid=(L // tm, n // tn),
        in_specs=[
            pl.BlockSpec((tm, D_MODEL), lambda i, j: (i, 0)),
            pl.BlockSpec((1, D_MODEL), lambda i, j: (0, 0)),
            pl.BlockSpec((D_MODEL, tn), lambda i, j: (0, j)),
        ],
        out_specs=[
            pl.BlockSpec((tm, tn), lambda i, j: (i, j)),
            pl.BlockSpec((tm, D_MODEL), lambda i, j: (i, 0)),
        ],
        out_shape=[
            jax.ShapeDtypeStruct((L, n), F32),
            jax.ShapeDtypeStruct((L, D_MODEL), BF16),
        ],
        scratch_shapes=[pltpu.VMEM((tm, D_MODEL), BF16)],
        compiler_params=_cp(("parallel", "arbitrary")),
        name="in_proj",
    )(x, gain.reshape(1, D_MODEL), w_mix)


def _s5_kernel(*refs, reverse, finalize, T):
    if finalize:
        (u_ref, bd_ref, cd_ref, hs_ref, p_ref, yb_ref, dsk_ref, wglu_ref, bglu_ref,
         o_ref, bur, bui, carry) = refs
    else:
        u_ref, bd_ref, cd_ref, hs_ref, p_ref, o_ref, bur, bui, carry = refs
    nt = T // 8

    @pl.when(pl.program_id(0) == 0)
    def _():
        carry[...] = jnp.zeros_like(carry)

    u = u_ref[...]
    ub = u.astype(BF16)
    for k in range(2):
        r = _dot(ub[:, 256 * k:256 * (k + 1)], bd_ref[k])
        bur[:, S5_HALF * k:S5_HALF * (k + 1)] = r[:, :S5_HALF]
        bui[:, S5_HALF * k:S5_HALF * (k + 1)] = r[:, S5_HALF:]

    def tile(kk, c):
        cr, ci = c
        idx = (nt - 1 - kk) if reverse else kk
        t0 = pl.multiple_of(idx * 8, 8)
        xr = bur[pl.ds(t0, 8), :]
        xi = bui[pl.ds(t0, 8), :]
        for s, d in enumerate((1, 2, 4)):
            sh = (8 - d) if reverse else d
            sr = pltpu.roll(xr, sh, 0)
            si = pltpu.roll(xi, sh, 0)
            ar = hs_ref[2 * s]
            ai = hs_ref[2 * s + 1]
            xr, xi = xr + ar * sr - ai * si, xi + ar * si + ai * sr
        pr = p_ref[0]
        pi = p_ref[1]
        xr, xi = xr + pr * cr - pi * ci, xi + pr * ci + pi * cr
        bur[pl.ds(t0, 8), :] = xr
        bui[pl.ds(t0, 8), :] = xi
        row = 0 if reverse else 7
        return (jnp.broadcast_to(xr[row:row + 1, :], (8, S5_NSTATE)),
                jnp.broadcast_to(xi[row:row + 1, :], (8, S5_NSTATE)))

    cr, ci = lax.fori_loop(0, nt, tile, (carry[0], carry[1]))
    carry[0] = cr
    carry[1] = ci

    ys = []
    for k in range(2):
        sr = bur[:, S5_HALF * k:S5_HALF * (k + 1)].astype(BF16)
        si = bui[:, S5_HALF * k:S5_HALF * (k + 1)].astype(BF16)
        ys.append(_dot(sr, cd_ref[k, 0]) + _dot(si, cd_ref[k, 1]))
    y = jnp.concatenate(ys, axis=1)
    if finalize:
        y = y + yb_ref[...] + dsk_ref[...] * u
        y = jax.nn.gelu(y)
        gate = _sigmoid(_dot(y.astype(BF16), wglu_ref[...]) + bglu_ref[...])
        o_ref[...] = (y * gate).astype(o_ref.dtype)
    else:
        o_ref[...] = y


def s5_direction_call(u, bd, cd, hs, pw, *, reverse, fin=None, T=256, ucol=0):
    L = u.shape[0]
    nb = L // T
    rmap = (lambda i: (nb - 1 - i, 0)) if reverse else (lambda i: (i, 0))
    umap = (lambda i: (nb - 1 - i, ucol)) if reverse else (lambda i: (i, ucol))
    c3 = lambda i: (0, 0, 0)
    in_specs = [
        pl.BlockSpec((T, BW), umap),
        pl.BlockSpec((2, 256, S5_NSTATE), c3),
        pl.BlockSpec((2, 2, S5_HALF, 256), lambda i: (0, 0, 0, 0)),
        pl.BlockSpec((6, 8, S5_NSTATE), c3),
        pl.BlockSpec((2, 8, S5_NSTATE), c3),
    ]
    args = [u, bd, cd, hs, pw]
    if fin is not None:
        yb, dsk, wglu, bglu = fin
        in_specs += [
            pl.BlockSpec((T, BW), rmap),
            pl.BlockSpec((1, BW), lambda i: (0, 0)),
            pl.BlockSpec((BW, BW), lambda i: (0, 0)),
            pl.BlockSpec((1, BW), lambda i: (0, 0)),
        ]
        args += [yb, dsk.reshape(1, BW), wglu, bglu.reshape(1, BW)]
    return pl.pallas_call(
        functools.partial(_s5_kernel, reverse=reverse, finalize=fin is not None, T=T),
        grid=(nb,),
        in_specs=in_specs,
        out_specs=pl.BlockSpec((T, BW), rmap),
        out_shape=jax.ShapeDtypeStruct((L, BW), BF16 if fin is not None else F32),
        scratch_shapes=[
            pltpu.VMEM((T, S5_NSTATE), F32),
            pltpu.VMEM((T, S5_NSTATE), F32),
            pltpu.VMEM((2, 8, S5_NSTATE), F32),
        ],
        compiler_params=_cp(("arbitrary",)),
        name="s5_rev" if reverse else "s5_fwd",
    )(*args)


def _s5_prep(lam_re, lam_im, log_step, b_re, b_im, c_re, c_im, reverse):
    G, P, Hg = S5_GROUPS, S5_STATE, S5_GROUP
    lam_re = jnp.minimum(lam_re.astype(F32), -1e-4)
    lam_im = lam_im.astype(F32)
    dt = jnp.exp(log_step.astype(F32))[:, None]
    mag = jnp.exp(lam_re * dt)
    abar_r = mag * jnp.cos(lam_im * dt)
    abar_i = mag * jnp.sin(lam_im * dt)
    den = lam_re * lam_re + lam_im * lam_im
    xr = abar_r - 1.0
    xi = abar_i
    coef_r = (xr * lam_re + xi * lam_im) / den
    coef_i = (xi * lam_re - xr * lam_im) / den
    b_re = b_re.astype(F32)
    b_im = b_im.astype(F32)
    bbar_r = coef_r[..., None] * b_re - coef_i[..., None] * b_im
    bbar_i = coef_r[..., None] * b_im + coef_i[..., None] * b_re

    eye = jnp.eye(16, dtype=F32)

    def bdiag_in(bb):
        blk = bb.transpose(0, 2, 1).reshape(2, 16, Hg, P)
        return jnp.einsum('gq,kghp->kghqp', eye, blk).reshape(2, 16 * Hg, 16 * P)

    bd = jnp.concatenate([bdiag_in(bbar_r), bdiag_in(bbar_i)], axis=-1).astype(BF16)

    def bdiag_out(cc):
        blk = cc.astype(F32).transpose(0, 2, 1).reshape(2, 16, P, Hg)
        return jnp.einsum('gq,kgph->kgpqh', eye, blk).reshape(2, 16 * P, 16 * Hg)

    cd = jnp.stack([bdiag_out(c_re), -bdiag_out(c_im)], axis=1).astype(BF16)

    ar = abar_r.reshape(-1)
    ai = abar_i.reshape(-1)

    def cm(a, b):
        return a[0] * b[0] - a[1] * b[1], a[0] * b[1] + a[1] * b[0]

    a1 = (ar, ai)
    a2 = cm(a1, a1)
    a3 = cm(a2, a1)
    a4 = cm(a2, a2)
    a5 = cm(a4, a1)
    a6 = cm(a4, a2)
    a7 = cm(a4, a3)
    a8 = cm(a4, a4)
    pows = [a1, a2, a3, a4, a5, a6, a7, a8]
    rows = np.arange(8)[:, None]
    hs = []
    for d, ad in ((1, a1), (2, a2), (4, a4)):
        mask = jnp.asarray((rows <= 7 - d) if reverse else (rows >= d), F32)
        hs.append(mask * ad[0][None, :])
        hs.append(mask * ad[1][None, :])
    hs = jnp.stack(hs)
    order = list(range(7, -1, -1)) if reverse else list(range(8))
    pw = jnp.stack([jnp.stack([pows[t][0] for t in order]),
                    jnp.stack([pows[t][1] for t in order])])
    return bd, cd, hs, pw


def s5_mixer(u, lam_re, lam_im, log_step, b_re, b_im, c_re, c_im, d_skip, w_glu, b_glu, T=256, ucol=0):
    pf = _s5_prep(lam_re[0], lam_im[0], log_step[0], b_re[0], b_im[0], c_re[0], c_im[0], False)
    pb = _s5_prep(lam_re[1], lam_im[1], log_step[1], b_re[1], b_im[1], c_re[1], c_im[1], True)
    yb = s5_direction_call(u, *pb, reverse=True, T=T, ucol=ucol)
    return s5_direction_call(u, *pf, reverse=False, T=T, ucol=ucol,
                             fin=(yb, d_skip.astype(F32), w_glu.astype(BF16), b_glu.astype(F32)))


S5_TT = 16
S5_GB = 8
S5_NPAIR = S5_GROUPS // S5_GB
S5_PW = S5_GB * S5_TT * S5_GROUP


def _hs_params(ar, ai, reverse):
    def cm(a, b):
        return a[0] * b[0] - a[1] * b[1], a[0] * b[1] + a[1] * b[0]

    a1 = (ar, ai)
    a2 = cm(a1, a1)
    a3 = cm(a2, a1)
    a4 = cm(a2, a2)
    pows = [a1, a2, a3, a4, cm(a4, a1), cm(a4, a2), cm(a4, a3), cm(a4, a4)]
    rows = np.arange(8)[:, None]
    hs = []
    for d, ad in ((1, a1), (2, a2), (4, a4)):
        mask = jnp.asarray((rows <= 7 - d) if reverse else (rows >= d), F32)
        hs.append(mask * ad[0][None, :])
        hs.append(mask * ad[1][None, :])
    order = list(range(7, -1, -1)) if reverse else list(range(8))
    pw = jnp.stack([jnp.stack([pows[t][0] for t in order]),
                    jnp.stack([pows[t][1] for t in order])])
    return jnp.stack(hs), pw


def _s5_tile_params(lam_re, lam_im, log_step, b_re, b_im, c_re, c_im, reverse):
    G, P, Hg, TT = S5_GROUPS, S5_STATE, S5_GROUP, S5_TT
    hp = lax.Precision.HIGHEST
    lam_re = jnp.minimum(lam_re.astype(F32), -1e-4)
    lam_im = lam_im.astype(F32)
    dt = jnp.exp(log_step.astype(F32))[:, None]
    mag = jnp.exp(lam_re * dt)
    abar_r = mag * jnp.cos(lam_im * dt)
    abar_i = mag * jnp.sin(lam_im * dt)
    den = lam_re * lam_re + lam_im * lam_im
    xr = abar_r - 1.0
    xi = abar_i
    coef_r = (xr * lam_re + xi * lam_im) / den
    coef_i = (xi * lam_re - xr * lam_im) / den
    b_re = b_re.astype(F32)
    b_im = b_im.astype(F32)
    bb_r = coef_r[..., None] * b_re - coef_i[..., None] * b_im
    bb_i = coef_r[..., None] * b_im + coef_i[..., None] * b_re
    cr = c_re.astype(F32)
    ci = c_im.astype(F32)
    pr = [jnp.ones_like(abar_r)]
    pi = [jnp.zeros_like(abar_i)]
    for _ in range(TT):
        pr.append(pr[-1] * abar_r - pi[-1] * abar_i)
        pi.append(pr[-2] * abar_i + pi[-1] * abar_r)
    pr = jnp.stack(pr)
    pi = jnp.stack(pi)
    car = cr[None] * pr[:TT, :, None, :] - ci[None] * pi[:TT, :, None, :]
    cai = cr[None] * pi[:TT, :, None, :] + ci[None] * pr[:TT, :, None, :]
    kd = (jnp.einsum('dgkp,gph->dgkh', car, bb_r, precision=hp)
          - jnp.einsum('dgkp,gph->dgkh', cai, bb_i, precision=hp))
    ii = np.arange(TT)[:, None]
    jj = np.arange(TT)[None, :]
    lag = (ii - jj) if reverse else (jj - ii)
    shift = jnp.asarray((lag[None] == np.arange(TT)[:, None, None]), F32)
    toep = jnp.einsum('dij,dgkh->gihjk', shift, kd, precision=hp)
    eye2 = jnp.eye(S5_GB, dtype=F32)
    GB = S5_GB
    e_idx = np.arange(TT) if reverse else (TT - 1 - np.arange(TT))
    er = pr[e_idx]
    ei = pi[e_idx]
    wf_r = (er[..., None] * bb_r[None] - ei[..., None] * bb_i[None]).transpose(1, 0, 3, 2)
    wf_i = (er[..., None] * bb_i[None] + ei[..., None] * bb_r[None]).transpose(1, 0, 3, 2)

    def pair_rows(w):
        return jnp.einsum('ab,kaihp->kiahbp', eye2, w.reshape(S5_NPAIR, GB, TT, Hg, P)).reshape(
            S5_NPAIR, S5_PW, GB * P)

    wf = jnp.concatenate([pair_rows(wf_r), pair_rows(wf_i)], axis=-1).astype(BF16)
    f_idx = (TT - np.arange(TT)) if reverse else (np.arange(TT) + 1)
    fr = pr[f_idx]
    fi = pi[f_idx]
    mr = (cr[None] * fr[:, :, None, :] - ci[None] * fi[:, :, None, :]).transpose(1, 3, 0, 2)
    mi = (cr[None] * fi[:, :, None, :] + ci[None] * fr[:, :, None, :]).transpose(1, 3, 0, 2)

    def pair_cols(w):
        return jnp.einsum('ab,kapjc->kapjbc', eye2, w.reshape(S5_NPAIR, GB, P, TT, Hg)).reshape(
            S5_NPAIR, GB * P, S5_PW)

    wc = jnp.concatenate([pair_cols(mr), pair_cols(-mi)], axis=1).astype(BF16)
    hs, pw = _hs_params(pr[TT].reshape(-1), pi[TT].reshape(-1), reverse)
    return toep, wf, wc, hs, pw


def _s5_state_kernel(u_ref, wff_ref, wfb_ref, f_ref):
    u = u_ref[...]
    for d, w_ref in enumerate((wff_ref, wfb_ref)):
        f = _dot(u, w_ref[...])
        f_ref[d, 0] = f[:, :S5_GB * S5_STATE]
        f_ref[d, 1] = f[:, S5_GB * S5_STATE:]


def _s5_scan_kernel(f_ref, hs_ref, p_ref, o_ref, pad_r, pad_i, *, reverse, N):
    nt = N // 8
    z8 = jnp.zeros((8, S5_NSTATE), F32)
    if reverse:
        pad_r[0:N] = f_ref[0]
        pad_i[0:N] = f_ref[1]
        pad_r[N:N + 8] = z8
        pad_i[N:N + 8] = z8
        o_ref[0] = pad_r[1:N + 1]
        o_ref[1] = pad_i[1:N + 1]
    else:
        pad_r[0:8] = z8
        pad_i[0:8] = z8
        pad_r[8:N + 8] = f_ref[0]
        pad_i[8:N + 8] = f_ref[1]
        o_ref[0] = pad_r[7:N + 7]
        o_ref[1] = pad_i[7:N + 7]

    def tile(kk, c):
        cr, ci = c
        idx = (nt - 1 - kk) if reverse else kk
        t0 = pl.multiple_of(idx * 8, 8)
        xr = o_ref[0, pl.ds(t0, 8), :]
        xi = o_ref[1, pl.ds(t0, 8), :]
        for s, d in enumerate((1, 2, 4)):
            sh = (8 - d) if reverse else d
            sr = pltpu.roll(xr, sh, 0)
            si = pltpu.roll(xi, sh, 0)
            ar = hs_ref[2 * s]
            ai = hs_ref[2 * s + 1]
            xr, xi = xr + ar * sr - ai * si, xi + ar * si + ai * sr
        pr = p_ref[0]
        pi = p_ref[1]
        xr, xi = xr + pr * cr - pi * ci, xi + pr * ci + pi * cr
        o_ref[0, pl.ds(t0, 8), :] = xr
        o_ref[1, pl.ds(t0, 8), :] = xi
        row = 0 if reverse else 7
        return (jnp.broadcast_to(xr[row:row + 1, :], (8, S5_NSTATE)),
                jnp.broadcast_to(xi[row:row + 1, :], (8, S5_NSTATE)))

    lax.fori_loop(0, nt, tile, (z8, z8))


def _s5_out_kernel(u_ref, toep_ref, cf_ref, cb_ref, wc_ref, y_ref):
    u = u_ref[...]
    st = jnp.concatenate([cf_ref[0], cf_ref[1], cb_ref[0], cb_ref[1]], axis=1).astype(BF16)
    y_ref[...] = _dot(u, toep_ref[...]) + _dot(st, wc_ref[...])


def _s5_final_kernel(y_ref, u_ref, dsk_ref, wglu_ref, bglu_ref, o_ref):
    y = y_ref[...] + dsk_ref[...] * u_ref[...]
    y = jax.nn.gelu(y)
    gate = _sigmoid(_dot(y.astype(BF16), wglu_ref[...]) + bglu_ref[...])
    o_ref[...] = (y * gate).astype(o_ref.dtype)


def s5_mixer_tiled(u, lam_re, lam_im, log_step, b_re, b_im, c_re, c_im, d_skip, w_glu, b_glu, ucol=0, RF=512):
    L = u.shape[0]
    TT, Hg = S5_TT, S5_GROUP
    N = L // TT
    NP = S5_NPAIR
    tf, wff, wcf, hsf, pwf = _s5_tile_params(lam_re[0], lam_im[0], log_step[0], b_re[0], b_im[0],
                                             c_re[0], c_im[0], False)
    tb, wfb, wcb, hsb, pwb = _s5_tile_params(lam_re[1], lam_im[1], log_step[1], b_re[1], b_im[1],
                                             c_re[1], c_im[1], True)
    GB = S5_GB
    eye = jnp.eye(GB, dtype=F32)
    toep = jnp.einsum('ab,kaihjc->kiahjbc', eye, (tf + tb).reshape(NP, GB, TT, Hg, TT, Hg)).reshape(
        NP, S5_PW, S5_PW).astype(BF16)
    wc = jnp.concatenate([wcf, wcb], axis=1)
    uu = u[:, ucol * BW:(ucol + 1) * BW]
    up = uu.reshape(N, TT, NP, GB * Hg).transpose(2, 0, 1, 3).reshape(NP, N, S5_PW).astype(BF16)
    pair = lambda a, b: pl.BlockSpec((None, a, b), lambda k: (k, 0, 0))
    f = pl.pallas_call(
        _s5_state_kernel,
        grid=(NP,),
        in_specs=[pair(N, S5_PW), pair(S5_PW, 2 * GB * S5_STATE), pair(S5_PW, 2 * GB * S5_STATE)],
        out_specs=pl.BlockSpec((2, 2, N, GB * S5_STATE), lambda k: (0, 0, 0, k)),
        out_shape=jax.ShapeDtypeStruct((2, 2, N, S5_NSTATE), F32),
        compiler_params=_cp(("parallel",)),
        name="s5_state",
    )(up, wff, wfb)
    cs = []
    for d, (hs, pw, reverse) in enumerate(((hsf, pwf, False), (hsb, pwb, True))):
        cs.append(pl.pallas_call(
            functools.partial(_s5_scan_kernel, reverse=reverse, N=N),
            grid=(1,),
            in_specs=[pl.BlockSpec((None, 2, N, S5_NSTATE), lambda i, d=d: (d, 0, 0, 0)),
                      pl.BlockSpec((6, 8, S5_NSTATE), lambda i: (0, 0, 0)),
                      pl.BlockSpec((2, 8, S5_NSTATE), lambda i: (0, 0, 0))],
            out_specs=pl.BlockSpec((2, N, S5_NSTATE), lambda i: (0, 0, 0)),
            out_shape=jax.ShapeDtypeStruct((2, N, S5_NSTATE), F32),
            scratch_shapes=[pltpu.VMEM((N + 8, S5_NSTATE), F32), pltpu.VMEM((N + 8, S5_NSTATE), F32)],
            compiler_params=_cp(("arbitrary",)),
            name="s5_scan_rev" if reverse else "s5_scan_fwd",
        )(f, hs, pw))
    NR = min(N, 256)
    cspec = pl.BlockSpec((2, NR, GB * S5_STATE), lambda k, r: (0, r, k))
    rows = pl.BlockSpec((None, NR, S5_PW), lambda k, r: (k, r, 0))
    wres = pl.BlockSpec((None, S5_PW, S5_PW), lambda k, r: (k, 0, 0))
    yflat = pl.pallas_call(
        _s5_out_kernel,
        grid=(NP, N // NR),
        in_specs=[rows, wres, cspec, cspec, wres],
        out_specs=rows,
        out_shape=jax.ShapeDtypeStruct((NP, N, S5_PW), F32),
        compiler_params=_cp(("parallel", "arbitrary")),
        name="s5_out",
    )(up, toep, cs[0], cs[1], wc)
    y = yflat.reshape(NP, N, TT, GB * Hg).transpose(1, 2, 0, 3).reshape(L, BW)
    row = pl.BlockSpec((RF, BW), lambda i: (i, 0))
    vec = pl.BlockSpec((1, BW), lambda i: (0, 0))
    return pl.pallas_call(
        _s5_final_kernel,
        grid=(L // RF,),
        in_specs=[row, pl.BlockSpec((RF, BW), lambda i: (i, ucol)), vec,
                  pl.BlockSpec((BW, BW), lambda i: (0, 0)), vec],
        out_specs=row,
        out_shape=jax.ShapeDtypeStruct((L, BW), BF16),
        compiler_params=_cp(("parallel",)),
        name="s5_final",
    )(y, u, d_skip.astype(F32).reshape(1, BW), w_glu.astype(BF16), b_glu.astype(F32).reshape(1, BW))


def _gdn_kernel(*refs, reverse, finalize, R, nb):
    if finalize:
        (qkv_ref, prev_ref, next_ref, cw_ref, sm_ref, smt_ref, gpr_ref, gpc_ref,
         ob_ref, z_ref, og_ref, o_ref, s_scr, q_scr, k_scr, v_scr, g_scr, b_scr, o_scr) = refs
    else:
        (qkv_ref, prev_ref, next_ref, cw_ref, sm_ref, smt_ref, gpr_ref, gpc_ref,
         o_ref, s_scr, q_scr, k_scr, v_scr, g_scr, b_scr, o_scr) = refs
    C = GDN_CHUNK
    nc = R // C
    i = pl.program_id(0)

    @pl.when(i == 0)
    def _():
        s_scr[...] = jnp.zeros_like(s_scr)

    blk = (nb - 1 - i) if reverse else i
    x = qkv_ref[...]
    pv = jnp.where(blk > 0, prev_ref[...], 0.0)
    nx = jnp.where(blk < nb - 1, next_ref[...], 0.0)
    xp = jnp.concatenate([pv, x, nx], axis=0)
    w = cw_ref[...]
    conv = (w[0:1] * xp[6:6 + R] + w[1:2] * xp[7:7 + R]
            + w[2:3] * xp[8:8 + R] + w[3:4] * xp[9:9 + R])
    act = conv * _sigmoid(conv)
    for h in range(GDN_HEADS):
        sl = slice(h * GDN_DH, (h + 1) * GDN_DH)
        qh = act[:, sl]
        q_scr[:, sl] = qh * lax.rsqrt(jnp.sum(qh * qh, axis=-1, keepdims=True) + 1e-6) * (GDN_DH ** -0.5)
        kh = act[:, BW + h * GDN_DH:BW + (h + 1) * GDN_DH]
        k_scr[:, sl] = kh * lax.rsqrt(jnp.sum(kh * kh, axis=-1, keepdims=True) + 1e-6)
    v_scr[...] = act[:, 2 * BW:3 * BW]
    sm = sm_ref[...]
    b_scr[...] = _sigmoid(sm[:, 0:4])
    g_scr[...] = gpr_ref[0:1, :] * _softplus(sm[:, 4:8] + gpr_ref[1:2, :])

    ii = lax.broadcasted_iota(jnp.int32, (C, C), 0)
    jj = lax.broadcasted_iota(jnp.int32, (C, C), 1)
    if reverse:
        incl = jj >= ii
        strict = jj > ii
    else:
        incl = jj <= ii
        strict = jj < ii
    tri_l = jnp.where(incl, 1.0, 0.0).astype(F32)
    tri_u = jnp.where((ii >= jj) if reverse else (ii <= jj), 1.0, 0.0).astype(F32)
    last = 0 if reverse else C - 1
    eye = jnp.where(ii == jj, 1.0, 0.0).astype(F32)
    pair_masks = []
    for lvl in range(6):
        bi = ii >> lvl
        bj = jj >> lvl
        lo_, hi_ = (bi, bj) if reverse else (bj, bi)
        pair_masks.append(((ii >> (lvl + 1)) == (jj >> (lvl + 1))) & ((hi_ & 1) == 1) & ((lo_ & 1) == 0))

    def chunk(cc, carry):
        cidx = (nc - 1 - cc) if reverse else cc
        r0 = pl.multiple_of(cidx * C, C)
        gch = g_scr[pl.ds(r0, C), :]
        bch = b_scr[pl.ds(r0, C), :]
        gt = gpc_ref[:, 0:1] * _softplus(smt_ref[cidx][4:8, :] + gpc_ref[:, 1:2])
        gc_col = jnp.dot(tri_l, gch, preferred_element_type=F32, precision=lax.Precision.HIGHEST)
        gc_row = jnp.dot(gt, tri_u, preferred_element_type=F32, precision=lax.Precision.HIGHEST)
        for h in range(GDN_HEADS):
            sl = slice(h * GDN_DH, (h + 1) * GDN_DH)
            gcol = gc_col[:, h:h + 1]
            grow = gc_row[h:h + 1, :]
            glast = gc_col[last:last + 1, h:h + 1]
            decay = jnp.exp(jnp.where(incl, gcol - grow, -jnp.inf))
            k = k_scr[pl.ds(r0, C), sl]
            q = q_scr[pl.ds(r0, C), sl]
            v = v_scr[pl.ds(r0, C), sl]
            beta = bch[:, h:h + 1]
            kb = k * beta
            kbf = k.astype(BF16)
            kk = _dot_nt(kb.astype(BF16), kbf)
            qk = _dot_nt(q.astype(BF16), kbf)
            a = jnp.where(strict, kk * decay, 0.0)
            attn = qk * decay
            xs = jnp.concatenate([v * beta, kb * jnp.exp(gcol)], axis=1)
            minv = eye - jnp.where(pair_masks[0], a, 0.0)
            for lvl in range(1, 6):
                mb = minv.astype(BF16)
                em = _dot(jnp.where(pair_masks[lvl], a, 0.0).astype(BF16), mb)
                minv = minv - _dot(mb, em.astype(BF16))
            xs = _dot(minv.astype(BF16), xs.astype(BF16))
            u_ = xs[:, :GDN_DH]
            w_ = xs[:, GDN_DH:]
            s_h = s_scr[h]
            sb = s_h.astype(BF16)
            v_new = u_ - _dot(w_.astype(BF16), sb)
            vnb = v_new.astype(BF16)
            o = _dot((q * jnp.exp(gcol)).astype(BF16), sb) + _dot(attn.astype(BF16), vnb)
            kdec = (k * jnp.exp(glast - gcol)).astype(BF16)
            s_scr[h] = s_h * jnp.exp(glast) + _dot_tn(kdec, vnb)
            o_scr[pl.ds(r0, C), sl] = o
        return carry

    lax.fori_loop(0, nc, chunk, 0)

    if finalize:
        o = o_scr[...] + ob_ref[...]
        z = z_ref[...]
        og = og_ref[...]
        for h in range(GDN_HEADS):
            sl = slice(h * GDN_DH, (h + 1) * GDN_DH)
            oh = o[:, sl]
            oh = oh * lax.rsqrt(jnp.mean(oh * oh, axis=-1, keepdims=True) + NORM_EPS) * og
            zh = z[:, sl]
            o_ref[:, sl] = (oh * (zh * _sigmoid(zh))).astype(o_ref.dtype)
    else:
        o_ref[...] = o_scr[...]


def gdn_direction_call(qkv, conv_w, smd, smdt, gpr, gpc, *, reverse, fin=None, R=256):
    L = qkv.shape[0]
    nb = L // R
    r8 = R // 8
    nc = R // GDN_CHUNK
    bmap = (lambda i: nb - 1 - i) if reverse else (lambda i: i)
    in_specs = [
        pl.BlockSpec((R, 3 * BW), lambda i: (bmap(i), 0)),
        pl.BlockSpec((8, 3 * BW), lambda i: (jnp.maximum(bmap(i) * r8 - 1, 0), 0)),
        pl.BlockSpec((8, 3 * BW), lambda i: (jnp.minimum((bmap(i) + 1) * r8, L // 8 - 1), 0)),
        pl.BlockSpec((4, 3 * BW), lambda i: (0, 0)),
        pl.BlockSpec((R, 8), lambda i: (bmap(i), 0)),
        pl.BlockSpec((nc, 8, GDN_CHUNK), lambda i: (bmap(i), 0, 0)),
        pl.BlockSpec((2, 4), lambda i: (0, 0)),
        pl.BlockSpec((4, 2), lambda i: (0, 0)),
    ]
    args = [qkv, qkv, qkv, conv_w, smd, smdt, gpr, gpc]
    if fin is not None:
        ob, z, og = fin
        in_specs += [
            pl.BlockSpec((R, BW), lambda i: (bmap(i), 0)),
            pl.BlockSpec((R, BW), lambda i: (bmap(i), 0)),
            pl.BlockSpec((1, GDN_DH), lambda i: (0, 0)),
        ]
        args += [ob, z, og.reshape(1, GDN_DH)]
    return pl.pallas_call(
        functools.partial(_gdn_kernel, reverse=reverse, finalize=fin is not None, R=R, nb=nb),
        grid=(nb,),
        in_specs=in_specs,
        out_specs=pl.BlockSpec((R, BW), lambda i: (bmap(i), 0)),
        out_shape=jax.ShapeDtypeStruct((L, BW), BF16 if fin is not None else F32),
        scratch_shapes=[
            pltpu.VMEM((GDN_HEADS, GDN_DH, GDN_DH), F32),
            pltpu.VMEM((R, BW), F32),
            pltpu.VMEM((R, BW), F32),
            pltpu.VMEM((R, BW), F32),
            pltpu.VMEM((R, 4), F32),
            pltpu.VMEM((R, 4), F32),
            pltpu.VMEM((R, BW), F32),
        ],
        compiler_params=_cp(("arbitrary",)),
        name="gdn_rev" if reverse else "gdn_fwd",
    )(*args)


def gdn_mixer(qkv, z, small, conv_w, a_log, dt_bias, o_gain, R=256):
    L = qkv.shape[0]
    coef = -jnp.exp(a_log.astype(F32))
    dtb = dt_bias.astype(F32)
    outs = []
    o_b = None
    for d in (1, 0):
        smd = jnp.concatenate([small[:, 4 * d:4 * d + 4], small[:, 8 + 4 * d:12 + 4 * d]], axis=1)
        smdt = smd.reshape(L // GDN_CHUNK, GDN_CHUNK, 8).transpose(0, 2, 1)
        gpr = jnp.stack([coef[d], dtb[d]])
        gpc = gpr.T
        if d == 1:
            o_b = gdn_direction_call(qkv, conv_w.astype(F32), smd, smdt, gpr, gpc, reverse=True, R=R)
        else:
            return gdn_direction_call(qkv, conv_w.astype(F32), smd, smdt, gpr, gpc, reverse=False,
                                      fin=(o_b, z, o_gain.astype(F32)), R=R)


GDN_HC = GDN_HEADS * GDN_CHUNK
GDN_UNROLL = 2


def _stack_heads(ref, r0):
    return jnp.concatenate(
        [ref[pl.ds(r0, GDN_CHUNK), h * GDN_DH:(h + 1) * GDN_DH] for h in range(GDN_HEADS)], axis=0)


def _gdn_prologue(d, blk, nb, R, qkv_ref, prev_ref, next_ref, cw_ref, sm_ref, gt_ref, gpr_ref, gpc_ref,
                  q_scr, k_scr, v_scr, b_scr, gc_scr, gr_scr, tri):
    x = qkv_ref[...]
    pv = jnp.where(blk > 0, prev_ref[...], 0.0)
    nx = jnp.where(blk < nb - 1, next_ref[...], 0.0)
    xp = jnp.concatenate([pv, x, nx], axis=0)
    w = cw_ref[...]
    conv = (w[0:1] * xp[6:6 + R] + w[1:2] * xp[7:7 + R]
            + w[2:3] * xp[8:8 + R] + w[3:4] * xp[9:9 + R])
    act = conv * _sigmoid(conv)
    for h in range(GDN_HEADS):
        sl = slice(h * GDN_DH, (h + 1) * GDN_DH)
        qh = act[:, sl]
        q_scr[d, :, sl] = qh * lax.rsqrt(jnp.sum(qh * qh, axis=-1, keepdims=True) + 1e-6) * (GDN_DH ** -0.5)
        kh = act[:, BW + h * GDN_DH:BW + (h + 1) * GDN_DH]
        k_scr[d, :, sl] = kh * lax.rsqrt(jnp.sum(kh * kh, axis=-1, keepdims=True) + 1e-6)
    v_scr[d] = act[:, 2 * BW:3 * BW]
    sm = sm_ref[...]
    b_scr[d] = _sigmoid(sm[:, 0:4])
    g = gpr_ref[2 * d:2 * d + 1, :] * _softplus(sm[:, 4:8] + gpr_ref[2 * d + 1:2 * d + 2, :])
    gt = gpc_ref[:, 2 * d:2 * d + 1] * _softplus(gt_ref[...] + gpc_ref[:, 2 * d + 1:2 * d + 2])
    tri_l, tri_u = tri
    gc_scr[d] = jnp.dot(tri_l, g, preferred_element_type=F32, precision=lax.Precision.HIGHEST)
    gr = jnp.dot(gt, tri_u, preferred_element_type=F32, precision=lax.Precision.HIGHEST)
    for c in range(R // GDN_CHUNK):
        gr_scr[d, c] = gr[:, c * GDN_CHUNK:(c + 1) * GDN_CHUNK]


def _gdn_chunks(insts, m_scr, q_scr, k_scr, v_scr, b_scr, gc_scr, gr_scr, s_scr):
    C = GDN_CHUNK
    H = GDN_HEADS
    n = len(insts)
    rng = range(n)
    ds = [t[0] for t in insts]
    r0s = [pl.multiple_of(t[1] * C, C) for t in insts]
    lasts = [0 if t[2] else C - 1 for t in insts]
    ks = [_stack_heads(k_scr.at[ds[i]], r0s[i]) for i in rng]
    qs = [_stack_heads(q_scr.at[ds[i]], r0s[i]) for i in rng]
    vs = [_stack_heads(v_scr.at[ds[i]], r0s[i]) for i in rng]
    bch = [b_scr[ds[i], pl.ds(r0s[i], C), :] for i in rng]
    gch = [gc_scr[ds[i], pl.ds(r0s[i], C), :] for i in rng]
    grw = [gr_scr[ds[i], insts[i][1]] for i in rng]
    beta = [jnp.concatenate([b[:, h:h + 1] for h in range(H)], axis=0) for b in bch]
    gcol = [jnp.concatenate([g[:, h:h + 1] for h in range(H)], axis=0) for g in gch]
    grow = [jnp.concatenate([g[h:h + 1, :] for h in range(H)], axis=1) for g in grw]
    glast_r = [jnp.concatenate(
        [jnp.broadcast_to(grw[i][h:h + 1, lasts[i]:lasts[i] + 1], (1, C)) for h in range(H)], axis=1)
        for i in rng]
    e_cat = [jnp.concatenate(
        [jnp.broadcast_to(jnp.exp(gch[i][lasts[i]:lasts[i] + 1, h:h + 1]), (1, GDN_DH)) for h in range(H)],
        axis=1) for i in rng]

    kst = [k.T for k in ks]
    kstb = [k.astype(BF16) for k in kst]
    kbs = [ks[i] * beta[i] for i in rng]
    g2 = [_dot(jnp.concatenate([kbs[i], qs[i]], axis=0).astype(BF16), kstb[i]) for i in rng]
    decay = [jnp.exp((gcol[i] - grow[i]) + m_scr[ds[i], 0]) for i in rng]
    a = [g2[i][:GDN_HC] * decay[i] * m_scr[ds[i], 1] for i in rng]
    attn = [(g2[i][GDN_HC:] * decay[i]).astype(BF16) for i in rng]
    minv = [m_scr[ds[i], 8] - a[i] * m_scr[ds[i], 2] for i in rng]
    for lvl in range(1, 6):
        mb = [m.astype(BF16) for m in minv]
        em = [_dot((a[i] * m_scr[ds[i], 2 + lvl]).astype(BF16), mb[i]).astype(BF16) for i in rng]
        minv = [minv[i] - _dot(mb[i], em[i]) for i in rng]
    egc = [jnp.exp(g) for g in gcol]
    rhs = [jnp.concatenate([vs[i] * beta[i], kbs[i] * egc[i]], axis=1).astype(BF16) for i in rng]
    xs = [_dot(minv[i].astype(BF16), rhs[i]) for i in rng]
    qg = [qs[i] * egc[i] for i in rng]
    wq = [jnp.concatenate([xs[i][:, GDN_DH:], qg[i]], axis=0).astype(BF16) for i in rng]
    kdect = [(kst[i] * jnp.exp(glast_r[i] - grow[i])).astype(BF16) for i in rng]

    zero = jnp.zeros((C, GDN_DH), BF16)
    by_dir = {}
    for i in rng:
        by_dir.setdefault(ds[i], []).append(i)
    depth = max(len(v) for v in by_dir.values())
    s_cat = {d: s_scr[d] for d in by_dir}
    for step in range(depth):
        act = [v[step] for v in by_dir.values() if len(v) > step]
        p = [_dot(wq[i], s_cat[ds[i]].astype(BF16)) for i in act]
        vn = []
        qss = []
        for j, i in enumerate(act):
            ws = jnp.concatenate(
                [p[j][h * C:(h + 1) * C, h * GDN_DH:(h + 1) * GDN_DH] for h in range(H)], axis=0)
            qss.append(jnp.concatenate(
                [p[j][GDN_HC + h * C:GDN_HC + (h + 1) * C, h * GDN_DH:(h + 1) * GDN_DH] for h in range(H)],
                axis=0))
            vn.append((xs[i][:, :GDN_DH] - ws).astype(BF16))
        o = [qss[j] + _dot(attn[i], vn[j]) for j, i in enumerate(act)]
        for j, i in enumerate(act):
            vn_bd = jnp.concatenate(
                [jnp.concatenate([vn[j][h * C:(h + 1) * C] if hh == h else zero for hh in range(H)], axis=1)
                 for h in range(H)], axis=0)
            s_cat[ds[i]] = s_cat[ds[i]] * e_cat[i] + _dot(kdect[i], vn_bd)
            o_ref = insts[i][3]
            for h in range(H):
                o_ref[pl.ds(r0s[i], C), h * GDN_DH:(h + 1) * GDN_DH] = o[j][h * C:(h + 1) * C]
    for d in by_dir:
        s_scr[d] = s_cat[d]


def _gdn_bidir_kernel(qf_ref, pf_ref, nf_ref, qb_ref, pb_ref, nb_ref, cw_ref, smf_ref, smb_ref,
                      gtf_ref, gtb_ref, gpr_ref, gpc_ref, of_ref, ob_ref,
                      s_scr, q_scr, k_scr, v_scr, b_scr, gc_scr, gr_scr, m_scr, *, R, nb):
    C = GDN_CHUNK
    nc = R // C
    i = pl.program_id(0)

    @pl.when(i == 0)
    def _():
        s_scr[...] = jnp.zeros_like(s_scr)
        ii = lax.broadcasted_iota(jnp.int32, (GDN_HC, GDN_HC), 0)
        jj = lax.broadcasted_iota(jnp.int32, (GDN_HC, GDN_HC), 1)
        blockd = (ii // C) == (jj // C)
        for d, reverse in enumerate((False, True)):
            incl = blockd & ((jj >= ii) if reverse else (jj <= ii))
            strict = blockd & ((jj > ii) if reverse else (jj < ii))
            m_scr[d, 0] = jnp.where(incl, 0.0, -jnp.inf).astype(F32)
            m_scr[d, 1] = jnp.where(strict, 1.0, 0.0).astype(F32)
            for lvl in range(6):
                bi = ii >> lvl
                bj = jj >> lvl
                lo_, hi_ = (bi, bj) if reverse else (bj, bi)
                pm = ((ii >> (lvl + 1)) == (jj >> (lvl + 1))) & ((hi_ & 1) == 1) & ((lo_ & 1) == 0)
                m_scr[d, 2 + lvl] = jnp.where(pm, 1.0, 0.0).astype(F32)
            m_scr[d, 8] = jnp.where(ii == jj, 1.0, 0.0).astype(F32)

    ri = lax.broadcasted_iota(jnp.int32, (R, R), 0)
    rj = lax.broadcasted_iota(jnp.int32, (R, R), 1)
    same = (ri // C) == (rj // C)
    tris = []
    for reverse in (False, True):
        tl = jnp.where(same & ((rj >= ri) if reverse else (rj <= ri)), 1.0, 0.0).astype(F32)
        tu = jnp.where(same & ((ri >= rj) if reverse else (ri <= rj)), 1.0, 0.0).astype(F32)
        tris.append((tl, tu))
    _gdn_prologue(0, i, nb, R, qf_ref, pf_ref, nf_ref, cw_ref, smf_ref, gtf_ref, gpr_ref, gpc_ref,
                  q_scr, k_scr, v_scr, b_scr, gc_scr, gr_scr, tris[0])
    _gdn_prologue(1, nb - 1 - i, nb, R, qb_ref, pb_ref, nb_ref, cw_ref, smb_ref, gtb_ref, gpr_ref, gpc_ref,
                  q_scr, k_scr, v_scr, b_scr, gc_scr, gr_scr, tris[1])

    def chunk(cc, carry):
        insts = []
        for uu in range(GDN_UNROLL):
            c = cc * GDN_UNROLL + uu
            insts.append((0, c, False, of_ref))
            insts.append((1, nc - 1 - c, True, ob_ref))
        _gdn_chunks(insts, m_scr, q_scr, k_scr, v_scr, b_scr, gc_scr, gr_scr, s_scr)
        return carry

    lax.fori_loop(0, nc // GDN_UNROLL, chunk, 0)


def _gdn_final_kernel(of_ref, ob_ref, z_ref, og_ref, o_ref):
    o = of_ref[...] + ob_ref[...]
    z = z_ref[...]
    og = og_ref[...]
    for h in range(GDN_HEADS):
        sl = slice(h * GDN_DH, (h + 1) * GDN_DH)
        oh = o[:, sl]
        oh = oh * lax.rsqrt(jnp.mean(oh * oh, axis=-1, keepdims=True) + NORM_EPS) * og
        zh = z[:, sl]
        o_ref[:, sl] = (oh * (zh * _sigmoid(zh))).astype(o_ref.dtype)


def gdn_mixer_fused(qkv, z, small, conv_w, a_log, dt_bias, o_gain, R=256, RF=512, zcol=0):
    L = qkv.shape[0]
    nb = L // R
    r8 = R // 8
    coef = -jnp.exp(a_log.astype(F32))
    dtb = dt_bias.astype(F32)
    gpr = jnp.stack([coef[0], dtb[0], coef[1], dtb[1]])
    gpc = gpr.T
    smf = jnp.concatenate([small[:, 0:4], small[:, 8:12]], axis=1)
    smb = jnp.concatenate([small[:, 4:8], small[:, 12:16]], axis=1)
    gtf = small[:, 8:12].T
    gtb = small[:, 12:16].T
    fmap = lambda i: i
    bmap = lambda i: nb - 1 - i
    blk = lambda m: pl.BlockSpec((R, 3 * BW), lambda i: (m(i), 0))
    prv = lambda m: pl.BlockSpec((8, 3 * BW), lambda i: (jnp.maximum(m(i) * r8 - 1, 0), 0))
    nxt = lambda m: pl.BlockSpec((8, 3 * BW), lambda i: (jnp.minimum((m(i) + 1) * r8, L // 8 - 1), 0))
    full = lambda a, b: pl.BlockSpec((a, b), lambda i: (0, 0))
    o_f, o_b = pl.pallas_call(
        functools.partial(_gdn_bidir_kernel, R=R, nb=nb),
        grid=(nb,),
        in_specs=[blk(fmap), prv(fmap), nxt(fmap), blk(bmap), prv(bmap), nxt(bmap),
                  full(4, 3 * BW),
                  pl.BlockSpec((R, 8), lambda i: (i, 0)),
                  pl.BlockSpec((R, 8), lambda i: (nb - 1 - i, 0)),
                  pl.BlockSpec((4, R), lambda i: (0, i)),
                  pl.BlockSpec((4, R), lambda i: (0, nb - 1 - i)),
                  full(4, 4), full(4, 4)],
        out_specs=[pl.BlockSpec((R, BW), lambda i: (i, 0)),
                   pl.BlockSpec((R, BW), lambda i: (nb - 1 - i, 0))],
        out_shape=[jax.ShapeDtypeStruct((L, BW), F32), jax.ShapeDtypeStruct((L, BW), F32)],
        scratch_shapes=[
            pltpu.VMEM((2, GDN_DH, BW), F32),
            pltpu.VMEM((2, R, BW), F32),
            pltpu.VMEM((2, R, BW), F32),
            pltpu.VMEM((2, R, BW), F32),
            pltpu.VMEM((2, R, 4), F32),
            pltpu.VMEM((2, R, 4), F32),
            pltpu.VMEM((2, R // GDN_CHUNK, 4, GDN_CHUNK), F32),
            pltpu.VMEM((2, 9, GDN_HC, GDN_HC), F32),
        ],
        compiler_params=_cp(("arbitrary",)),
        name="gdn_bidir",
    )(qkv, qkv, qkv, qkv, qkv, qkv, conv_w.astype(F32), smf, smb, gtf, gtb, gpr, gpc)
    row = pl.BlockSpec((RF, BW), lambda i: (i, 0))
    zrow = pl.BlockSpec((RF, BW), lambda i: (i, zcol))
    return pl.pallas_call(
        _gdn_final_kernel,
        grid=(L // RF,),
        in_specs=[row, row, zrow, pl.BlockSpec((1, GDN_DH), lambda i: (0, 0))],
        out_specs=row,
        out_shape=jax.ShapeDtypeStruct((L, BW), BF16),
        compiler_params=_cp(("parallel",)),
        name="gdn_final",
    )(o_f, o_b, z, o_gain.astype(F32).reshape(1, GDN_DH))


def _t5_bucket(rel):
    nb = T5_BUCKETS // 2
    max_exact = nb // 2
    ret = jnp.where(rel > 0, nb, 0)
    n = jnp.abs(rel)
    nf = jnp.maximum(n, 1).astype(F32)
    large = max_exact + (jnp.log(nf / max_exact) / math.log(T5_MAX_DISTANCE / max_exact)
                         * (nb - max_exact)).astype(jnp.int32)
    large = jnp.minimum(large, nb - 1)
    return ret + jnp.where(n < max_exact, n, large)


def _swa_kernel(q_ref, kp_ref, kc_ref, kn_ref, vp_ref, vc_ref, vn_ref, bias_ref, sink_ref,
                o_ref, *, nb):
    i = pl.program_id(0)
    B = SWA_BLOCK
    G = SWA_HEADS // SWA_KV_HEADS
    qi = lax.broadcasted_iota(jnp.int32, (G * B, 3 * B), 0) % B
    sj = lax.broadcasted_iota(jnp.int32, (G * B, 3 * B), 1)
    rel = sj - B - qi
    lo = jnp.where(i == 0, B, 0)
    hi = jnp.where(i == nb - 1, 2 * B, 3 * B)
    valid = (jnp.abs(rel) <= WINDOW) & (sj >= lo) & (sj < hi)
    for g in range(SWA_KV_HEADS):
        kb = jnp.concatenate([kp_ref[g], kc_ref[g], kn_ref[g]], axis=0).astype(BF16)
        vb = jnp.concatenate([vp_ref[g], vc_ref[g], vn_ref[g]], axis=0).astype(BF16)
        q = q_ref[g * G:(g + 1) * G].reshape(G * B, SWA_DH).astype(BF16)
        bias = bias_ref[g * G:(g + 1) * G].reshape(G * B, 3 * B)
        s = _dot_nt(q, kb) * (SWA_DH ** -0.5) + bias
        s = jnp.where(valid, s, -1e30)
        sink = jnp.broadcast_to(sink_ref[g * G:(g + 1) * G], (G, B, 1)).reshape(G * B, 1)
        m = jnp.maximum(jnp.max(s, axis=-1, keepdims=True), sink)
        p = jnp.exp(s - m)
        den = jnp.sum(p, axis=-1, keepdims=True) + jnp.exp(sink - m)
        o = _dot(p.astype(BF16), vb) / den
        o_ref[g * G:(g + 1) * G] = o.reshape(G, B, SWA_DH).astype(o_ref.dtype)


def swa_mixer(q, kv, sink, t5_bias):
    L = q.shape[0]
    B = SWA_BLOCK
    nb = L // B
    qh = q.reshape(L, SWA_HEADS, SWA_DH).transpose(1, 0, 2)
    kh = kv[:, :SWA_KV_HEADS * SWA_DH].reshape(L, SWA_KV_HEADS, SWA_DH).transpose(1, 0, 2)
    vh = kv[:, SWA_KV_HEADS * SWA_DH:].reshape(L, SWA_KV_HEADS, SWA_DH).transpose(1, 0, 2)
    rel = jnp.arange(3 * B)[None, :] - B - jnp.arange(B)[:, None]
    bias = t5_bias.astype(F32)[_t5_bucket(rel)].transpose(2, 0, 1)
    kspec = lambda f: pl.BlockSpec((SWA_KV_HEADS, B, SWA_DH), lambda i: (0, f(i), 0))
    prev = lambda i: jnp.maximum(i - 1, 0)
    cur = lambda i: i
    nxt = lambda i: jnp.minimum(i + 1, nb - 1)
    out = pl.pallas_call(
        functools.partial(_swa_kernel, nb=nb),
        grid=(nb,),
        in_specs=[
            pl.BlockSpec((SWA_HEADS, B, SWA_DH), lambda i: (0, i, 0)),
            kspec(prev), kspec(cur), kspec(nxt),
            kspec(prev), kspec(cur), kspec(nxt),
            pl.BlockSpec((SWA_HEADS, B, 3 * B), lambda i: (0, 0, 0)),
            pl.BlockSpec((SWA_HEADS, 1, 1), lambda i: (0, 0, 0)),
        ],
        out_specs=pl.BlockSpec((SWA_HEADS, B, SWA_DH), lambda i: (0, i, 0)),
        out_shape=jax.ShapeDtypeStruct((SWA_HEADS, L, SWA_DH), BF16),
        compiler_params=_cp(("parallel",)),
        name="swa",
    )(qh, kh, kh, kh, vh, vh, vh, bias, sink.astype(F32).reshape(SWA_HEADS, 1, 1))
    return out.transpose(1, 0, 2).reshape(L, SWA_HEADS * SWA_DH)


SWA_G = SWA_HEADS // SWA_KV_HEADS
SWA_VT_ROWS = SWA_DH + 16


def _swa_t_kernel(q_ref, kvp_ref, kvc_ref, kvn_ref, bias_ref, sink_ref, o_ref, *, nb):
    i = pl.program_id(0)
    B = SWA_BLOCK
    KD = SWA_KV_HEADS * SWA_DH
    sj = lax.broadcasted_iota(jnp.int32, (3 * B, SWA_G * B), 0)
    qi = lax.broadcasted_iota(jnp.int32, (3 * B, SWA_G * B), 1) % B
    rel = sj - B - qi
    lo = jnp.where(i == 0, B, 0)
    hi = jnp.where(i == nb - 1, 2 * B, 3 * B)
    valid = (jnp.abs(rel) <= WINDOW) & (sj >= lo) & (sj < hi)
    qt = (q_ref[...] * (SWA_DH ** -0.5)).T
    kb = jnp.concatenate([kvp_ref[:, 0:KD], kvc_ref[:, 0:KD], kvn_ref[:, 0:KD]], axis=0).astype(BF16)
    vt = jnp.concatenate([kvp_ref[:, KD:2 * KD].T, kvc_ref[:, KD:2 * KD].T, kvn_ref[:, KD:2 * KD].T],
                         axis=1).astype(BF16)
    zq = jnp.zeros((SWA_DH, SWA_G * B), F32)
    ones = jnp.ones((SWA_VT_ROWS - SWA_DH, 3 * B), BF16)
    outs = []
    for g in range(SWA_KV_HEADS):
        qg = jnp.concatenate([qt[(g * SWA_G + hh) * SWA_DH:(g * SWA_G + hh + 1) * SWA_DH, :]
                              for hh in range(SWA_G)], axis=1)
        qpad = jnp.concatenate([qg if gg == g else zq for gg in range(SWA_KV_HEADS)], axis=0).astype(BF16)
        s = _dot(kb, qpad) + bias_ref[g]
        s = jnp.where(valid, s, -1e30)
        sink = sink_ref[g]
        m = jnp.maximum(jnp.max(s, axis=0, keepdims=True), sink)
        p = jnp.exp(s - m).astype(BF16)
        vg = jnp.concatenate([vt[g * SWA_DH:(g + 1) * SWA_DH, :], ones], axis=0)
        pv = _dot(vg, p)
        den = pv[SWA_DH:SWA_DH + 1] + jnp.exp(sink - m)
        o = pv[0:SWA_DH] / den
        outs += [o[:, hh * B:(hh + 1) * B] for hh in range(SWA_G)]
    ot = jnp.concatenate(outs, axis=0)
    o_ref[...] = ot.T.astype(o_ref.dtype)


def swa_mixer_t(q, kv, sink, t5_bias, qcol=0, kvcol=0):
    L = q.shape[0]
    B = SWA_BLOCK
    nb = L // B
    rel = jnp.arange(3 * B)[None, :] - B - jnp.arange(B)[:, None]
    onehot = (_t5_bucket(rel)[..., None] == jnp.arange(T5_BUCKETS)).astype(F32)
    bias = jnp.einsum('qsb,bh->hqs', onehot, t5_bias.astype(F32), precision=lax.Precision.HIGHEST)
    bias_t = bias.reshape(SWA_KV_HEADS, SWA_G, B, 3 * B).transpose(0, 3, 1, 2).reshape(
        SWA_KV_HEADS, 3 * B, SWA_G * B)
    sink_r = jnp.broadcast_to(sink.astype(F32).reshape(SWA_KV_HEADS, 1, SWA_G, 1),
                              (SWA_KV_HEADS, 1, SWA_G, B)).reshape(SWA_KV_HEADS, 1, SWA_G * B)
    kvspec = lambda f: pl.BlockSpec((B, 2 * SWA_KV_HEADS * SWA_DH), lambda i: (f(i), kvcol))
    return pl.pallas_call(
        functools.partial(_swa_t_kernel, nb=nb),
        grid=(nb,),
        in_specs=[
            pl.BlockSpec((B, SWA_HEADS * SWA_DH), lambda i: (i, qcol)),
            kvspec(lambda i: jnp.maximum(i - 1, 0)), kvspec(lambda i: i),
            kvspec(lambda i: jnp.minimum(i + 1, nb - 1)),
            pl.BlockSpec((SWA_KV_HEADS, 3 * B, SWA_G * B), lambda i: (0, 0, 0)),
            pl.BlockSpec((SWA_KV_HEADS, 1, SWA_G * B), lambda i: (0, 0, 0)),
        ],
        out_specs=pl.BlockSpec((B, SWA_HEADS * SWA_DH), lambda i: (i, 0)),
        out_shape=jax.ShapeDtypeStruct((L, SWA_HEADS * SWA_DH), BF16),
        compiler_params=_cp(("parallel",)),
        name="swa",
    )(q, kv, kv, kv, bias_t, sink_r)


def _mla_prep_kernel(cq_ref, ckv_ref, kr_ref, krr_ref, cos_ref, sin_ref, qg_ref, kvg_ref,
                     wqn_ref, wqp_ref, wqr_ref, wkv_ref, q_ref, k_ref, v_ref):
    cq = cq_ref[...]
    qn = (cq * lax.rsqrt(jnp.mean(cq * cq, axis=-1, keepdims=True) + NORM_EPS) * qg_ref[...]).astype(BF16)
    ckv = ckv_ref[...]
    kn = (ckv * lax.rsqrt(jnp.mean(ckv * ckv, axis=-1, keepdims=True) + NORM_EPS) * kvg_ref[...]).astype(BF16)
    cos = cos_ref[...]
    sin = sin_ref[...]
    q_nope = _dot(qn, wqn_ref[...])
    q_pe = _dot(qn, wqp_ref[...])
    q_rot = _dot(qn, wqr_ref[...])
    kv = _dot(kn, wkv_ref[...])
    k_pe = (kr_ref[...] * cos + krr_ref[...] * sin).astype(BF16)
    for h in range(MLA_HEADS):
        sl = slice(h * 128, (h + 1) * 128)
        q_ref[h, :, 0:128] = q_nope[:, sl].astype(BF16)
        q_ref[h, :, 128:256] = (q_pe[:, sl] * cos + q_rot[:, sl] * sin).astype(BF16)
        k_ref[h, :, 0:128] = kv[:, h * 256:h * 256 + 128].astype(BF16)
        k_ref[h, :, 128:256] = k_pe
        v_ref[h] = kv[:, h * 256 + 128:(h + 1) * 256].astype(BF16)


def _mla_attn_kernel(q_ref, k_ref, v_ref, o_ref, *, tk, nk):
    q = q_ref[0]
    tq = q.shape[0]
    scale = (MLA_NOPE + MLA_ROPE) ** -0.5

    def body(kc, carry):
        m, l, acc = carry
        k0 = pl.multiple_of(kc * tk, tk)
        k = k_ref[0, pl.ds(k0, tk), :]
        v = v_ref[0, pl.ds(k0, tk), :]
        s = _dot_nt(q, k) * scale
        m_new = jnp.maximum(m, jnp.max(s, axis=-1, keepdims=True))
        alpha = jnp.exp(m - m_new)
        p = jnp.exp(s - m_new)
        l = alpha * l + jnp.sum(p, axis=-1, keepdims=True)
        acc = alpha * acc + _dot(p.astype(BF16), v)
        return m_new, l, acc

    m0 = jnp.full((tq, 1), -jnp.inf, F32)
    l0 = jnp.zeros((tq, 1), F32)
    a0 = jnp.zeros((tq, MLA_V), F32)
    m, l, acc = lax.fori_loop(0, nk, body, (m0, l0, a0))
    o_ref[...] = (acc / l).astype(o_ref.dtype)


def mla_mixer(c_q, c_kv, kr_pad, krr_pad, q_gain, kv_gain, w_uq, w_ukv, R=512, tq=512, tk=512):
    L = c_q.shape[0]
    H = MLA_HEADS
    half = MLA_ROPE // 2
    pos = jnp.arange(L, dtype=F32)
    inv_freq = ROPE_THETA ** (-jnp.arange(0, MLA_ROPE, 2, dtype=F32) / MLA_ROPE)
    ang = pos[:, None] * inv_freq[None, :]
    cos, sin = jnp.cos(ang), jnp.sin(ang)
    zpad = jnp.zeros((L, 128 - MLA_ROPE), F32)
    cos_t = jnp.concatenate([cos, cos, zpad], axis=1)
    sin_t = jnp.concatenate([-sin, sin, zpad], axis=1)
    wq = w_uq.reshape(MLA_Q_RANK, H, MLA_NOPE + MLA_ROPE)
    wqn = wq[:, :, :MLA_NOPE].reshape(MLA_Q_RANK, H * MLA_NOPE).astype(BF16)
    wpe = wq[:, :, MLA_NOPE:]
    zw = jnp.zeros((MLA_Q_RANK, H, 128 - MLA_ROPE), w_uq.dtype)
    wqp = jnp.concatenate([wpe, zw], axis=2).reshape(MLA_Q_RANK, H * 128).astype(BF16)
    wrot = jnp.concatenate([wpe[:, :, half:], wpe[:, :, :half], zw], axis=2)
    wqr = wrot.reshape(MLA_Q_RANK, H * 128).astype(BF16)
    wkv = w_ukv.astype(BF16)
    row = lambda w: pl.BlockSpec((R, w), lambda i: (i, 0))
    full = lambda a, b: pl.BlockSpec((a, b), lambda i: (0, 0))
    q, k, v = pl.pallas_call(
        _mla_prep_kernel,
        grid=(L // R,),
        in_specs=[row(MLA_Q_RANK), row(MLA_KV_RANK), row(128), row(128), row(128), row(128),
                  full(1, MLA_Q_RANK), full(1, MLA_KV_RANK),
                  full(MLA_Q_RANK, H * 128), full(MLA_Q_RANK, H * 128), full(MLA_Q_RANK, H * 128),
                  full(MLA_KV_RANK, H * 256)],
        out_specs=[pl.BlockSpec((H, R, 256), lambda i: (0, i, 0)),
                   pl.BlockSpec((H, R, 256), lambda i: (0, i, 0)),
                   pl.BlockSpec((H, R, 128), lambda i: (0, i, 0))],
        out_shape=[jax.ShapeDtypeStruct((H, L, 256), BF16),
                   jax.ShapeDtypeStruct((H, L, 256), BF16),
                   jax.ShapeDtypeStruct((H, L, 128), BF16)],
        compiler_params=_cp(("parallel",)),
        name="mla_prep",
    )(c_q, c_kv, kr_pad, krr_pad, cos_t, sin_t,
      q_gain.astype(F32).reshape(1, -1), kv_gain.astype(F32).reshape(1, -1), wqn, wqp, wqr, wkv)
    return pl.pallas_call(
        functools.partial(_mla_attn_kernel, tk=tk, nk=L // tk),
        grid=(H, L // tq),
        in_specs=[pl.BlockSpec((1, tq, 256), lambda h, i: (h, i, 0)),
                  pl.BlockSpec((1, L, 256), lambda h, i: (h, 0, 0)),
                  pl.BlockSpec((1, L, 128), lambda h, i: (h, 0, 0))],
        out_specs=pl.BlockSpec((tq, MLA_V), lambda h, i: (i, h)),
        out_shape=jax.ShapeDtypeStruct((L, H * MLA_V), BF16),
        compiler_params=_cp(("parallel", "parallel")),
        name="mla_attn",
    )(q, k, v)


MLA_VT_ROWS = MLA_V + 16
LOG2E = 1.4426950408889634


def _mla_prep_t_kernel(cq_ref, ckv_ref, kr_ref, krr_ref, cos_ref, sin_ref, cost_ref, sint_ref,
                       qg_ref, kvg_ref, wqn_ref, wqp_ref, wqr_ref, wk_ref, wv_ref,
                       qt_ref, k_ref, vt_ref):
    cq = cq_ref[...]
    qn = (cq * lax.rsqrt(jnp.mean(cq * cq, axis=-1, keepdims=True) + NORM_EPS) * qg_ref[...]).astype(BF16)
    ckv = ckv_ref[...]
    kn = (ckv * lax.rsqrt(jnp.mean(ckv * ckv, axis=-1, keepdims=True) + NORM_EPS) * kvg_ref[...]).astype(BF16)
    c = (MLA_NOPE + MLA_ROPE) ** -0.5 * LOG2E
    qt_nope = _dot_nt(wqn_ref[...], qn)
    qt_pe = _dot_nt(wqp_ref[...], qn)
    qt_rot = _dot_nt(wqr_ref[...], qn)
    cost = cost_ref[...]
    sint = sint_ref[...]
    k_nope = _dot(kn, wk_ref[...])
    vt = _dot_nt(wv_ref[...], kn)
    k_pe = (kr_ref[...] * cos_ref[...] + krr_ref[...] * sin_ref[...]).astype(BF16)
    ones = jnp.ones((MLA_VT_ROWS - MLA_V, vt.shape[1]), BF16)
    for h in range(MLA_HEADS):
        sl = slice(h * 128, (h + 1) * 128)
        qt_ref[h, 0:128, :] = (qt_nope[sl] * c).astype(BF16)
        qt_ref[h, 128:256, :] = ((qt_pe[sl] * cost + qt_rot[sl] * sint) * c).astype(BF16)
        k_ref[h, :, 0:128] = k_nope[:, sl].astype(BF16)
        k_ref[h, :, 128:256] = k_pe
        vt_ref[h, 0:MLA_V, :] = vt[sl].astype(BF16)
        vt_ref[h, MLA_V:MLA_VT_ROWS, :] = ones


def _mla_attn_t_kernel(qt_ref, k_ref, vt_ref, o_ref, s_scr, *, tk, nk, tsub):
    tq = qt_ref.shape[2]
    nsub = tq // tsub
    subs = range(nsub)
    qts = [qt_ref[0, :, j * tsub:(j + 1) * tsub] for j in subs]

    def put_scores(kc, slot):
        k0 = pl.multiple_of(kc * tk, tk)
        kblk = k_ref[0, pl.ds(k0, tk), :]
        for j in subs:
            s_scr[slot, j] = _dot(kblk, qts[j])

    def consume(kc, slot, ms, accs):
        k0 = pl.multiple_of(kc * tk, tk)
        vt = vt_ref[0, :, pl.ds(k0, tk)]
        ss = [s_scr[slot, j] for j in subs]
        m_new = [jnp.maximum(ms[j], jnp.max(ss[j], axis=0, keepdims=True)) for j in subs]
        ps = [jnp.exp2(ss[j] - m_new[j]).astype(BF16) for j in subs]
        pv = [_dot(vt, ps[j]) for j in subs]
        accs = [jnp.exp2(ms[j] - m_new[j]) * accs[j] + pv[j] for j in subs]
        return m_new, accs

    def body(kk, carry):
        ms, accs = carry
        kc = 2 * kk
        put_scores(kc + 1, 1)
        ms, accs = consume(kc, 0, list(ms), list(accs))
        put_scores(jnp.minimum(kc + 2, nk - 1), 0)
        ms, accs = consume(kc + 1, 1, ms, accs)
        return tuple(ms), tuple(accs)

    m0 = tuple(jnp.full((1, tsub), -jnp.inf, F32) for _ in subs)
    a0 = tuple(jnp.zeros((MLA_VT_ROWS, tsub), F32) for _ in subs)
    put_scores(0, 0)
    ms, accs = lax.fori_loop(0, nk // 2, body, (m0, a0))
    for j in subs:
        o = accs[j][0:MLA_V] / accs[j][MLA_V:MLA_V + 1]
        o_ref[j * tsub:(j + 1) * tsub, :] = o.T.astype(o_ref.dtype)


def _rope_tables(L):
    pos = np.arange(L, dtype=np.float32)
    inv_freq = np.float32(ROPE_THETA) ** (-np.arange(0, MLA_ROPE, 2, dtype=np.float32) / np.float32(MLA_ROPE))
    ang = pos[:, None] * inv_freq.astype(np.float32)[None, :]
    return jnp.asarray(np.cos(ang), F32), jnp.asarray(np.sin(ang), F32)


def mla_mixer_t(c_q, c_kv, kr_pad, krr_pad, q_gain, kv_gain, w_uq, w_ukv, R=512, tq=1024, tk=512, tsub=256,
                cols=(0, 0, 0, 0)):
    L = c_q.shape[0]
    H = MLA_HEADS
    half = MLA_ROPE // 2
    cos, sin = _rope_tables(L)
    zpad = jnp.zeros((L, 128 - MLA_ROPE), F32)
    cos_t = jnp.concatenate([cos, cos, zpad], axis=1)
    sin_t = jnp.concatenate([-sin, sin, zpad], axis=1)
    wq = w_uq.reshape(MLA_Q_RANK, H, MLA_NOPE + MLA_ROPE)
    wqn = wq[:, :, :MLA_NOPE].reshape(MLA_Q_RANK, H * MLA_NOPE).T.astype(BF16)
    wpe = wq[:, :, MLA_NOPE:]
    zw = jnp.zeros((MLA_Q_RANK, H, 128 - MLA_ROPE), w_uq.dtype)
    wqp = jnp.concatenate([wpe, zw], axis=2).reshape(MLA_Q_RANK, H * 128).T.astype(BF16)
    wrot = jnp.concatenate([wpe[:, :, half:], wpe[:, :, :half], zw], axis=2)
    wqr = wrot.reshape(MLA_Q_RANK, H * 128).T.astype(BF16)
    wkv = w_ukv.reshape(MLA_KV_RANK, H, MLA_NOPE + MLA_V)
    wk = wkv[:, :, :MLA_NOPE].reshape(MLA_KV_RANK, H * MLA_NOPE).astype(BF16)
    wv = wkv[:, :, MLA_NOPE:].reshape(MLA_KV_RANK, H * MLA_V).T.astype(BF16)
    row = lambda w, c=0: pl.BlockSpec((R, w), lambda i: (i, c))
    col = lambda w: pl.BlockSpec((w, R), lambda i: (0, i))
    full = lambda a, b: pl.BlockSpec((a, b), lambda i: (0, 0))
    qt, k, vt = pl.pallas_call(
        _mla_prep_t_kernel,
        grid=(L // R,),
        in_specs=[row(MLA_Q_RANK, cols[0]), row(MLA_KV_RANK, cols[1]), row(128, cols[2]), row(128, cols[3]),
                  row(128), row(128), col(128), col(128),
                  full(1, MLA_Q_RANK), full(1, MLA_KV_RANK),
                  full(H * 128, MLA_Q_RANK), full(H * 128, MLA_Q_RANK), full(H * 128, MLA_Q_RANK),
                  full(MLA_KV_RANK, H * 128), full(H * 128, MLA_KV_RANK)],
        out_specs=[pl.BlockSpec((H, 256, R), lambda i: (0, 0, i)),
                   pl.BlockSpec((H, R, 256), lambda i: (0, i, 0)),
                   pl.BlockSpec((H, MLA_VT_ROWS, R), lambda i: (0, 0, i))],
        out_shape=[jax.ShapeDtypeStruct((H, 256, L), BF16),
                   jax.ShapeDtypeStruct((H, L, 256), BF16),
                   jax.ShapeDtypeStruct((H, MLA_VT_ROWS, L), BF16)],
        compiler_params=_cp(("parallel",)),
        name="mla_prep",
    )(c_q, c_kv, kr_pad, krr_pad, cos_t, sin_t, cos_t.T, sin_t.T,
      q_gain.astype(F32).reshape(1, -1), kv_gain.astype(F32).reshape(1, -1), wqn, wqp, wqr, wk, wv)
    return pl.pallas_call(
        functools.partial(_mla_attn_t_kernel, tk=tk, nk=L // tk, tsub=tsub),
        grid=(H, L // tq),
        in_specs=[pl.BlockSpec((1, 256, tq), lambda h, i: (h, 0, i)),
                  pl.BlockSpec((1, L, 256), lambda h, i: (h, 0, 0)),
                  pl.BlockSpec((1, MLA_VT_ROWS, L), lambda h, i: (h, 0, 0))],
        out_specs=pl.BlockSpec((tq, MLA_V), lambda h, i: (i, h)),
        out_shape=jax.ShapeDtypeStruct((L, H * MLA_V), BF16),
        scratch_shapes=[pltpu.VMEM((2, tq // tsub, tk, tsub), F32)],
        compiler_params=_cp(("parallel", "parallel")),
        name="mla_attn",
    )(qt, k, vt)


def _merge_kernel(h_ref, ya_ref, yb_ref, yc_ref, yd_ref, g0_ref, g1_ref, g2_ref, g3_ref, p_ref, o_ref):
    h = h_ref[...]
    acc = None
    for b, (y_ref, g_ref) in enumerate(((ya_ref, g0_ref), (yb_ref, g1_ref), (yc_ref, g2_ref), (yd_ref, g3_ref))):
        gate = _sigmoid(_dot(h, g_ref[...]))
        t = gate * _dot(y_ref[...], p_ref[b])
        acc = t if acc is None else acc + t
    o_ref[...] = acc.astype(o_ref.dtype)


def merge_branches(h, ys, w_gate, w_branch, tm=1024, tn=512):
    L = h.shape[0]
    nj = D_MODEL // tn
    gspec = lambda b: pl.BlockSpec((D_MODEL, tn), lambda i, j: (0, b * nj + j))
    yspec = pl.BlockSpec((tm, BW), lambda i, j: (i, 0))
    return pl.pallas_call(
        _merge_kernel,
        grid=(L // tm, nj),
        in_specs=[pl.BlockSpec((tm, D_MODEL), lambda i, j: (i, 0)), yspec, yspec, yspec, yspec,
                  gspec(0), gspec(1), gspec(2), gspec(3),
                  pl.BlockSpec((4, BW, tn), lambda i, j: (0, 0, j))],
        out_specs=pl.BlockSpec((tm, tn), lambda i, j: (i, j)),
        out_shape=jax.ShapeDtypeStruct((L, D_MODEL), BF16),
        compiler_params=_cp(("parallel", "arbitrary")),
        name="merge",
    )(h, *ys, w_gate, w_gate, w_gate, w_gate, w_branch)


def _out_proj_kernel(x_ref, m_ref, w_ref, g_ref, o_ref):
    f = _dot(m_ref[...], w_ref[...])
    f = f * lax.rsqrt(jnp.mean(f * f, axis=-1, keepdims=True) + NORM_EPS) * g_ref[...]
    o_ref[...] = x_ref[...] + f


def out_proj(x, m, w_out, gain, tm=512):
    L = x.shape[0]
    return pl.pallas_call(
        _out_proj_kernel,
        grid=(L // tm,),
        in_specs=[pl.BlockSpec((tm, D_MODEL), lambda i: (i, 0)),
                  pl.BlockSpec((tm, D_MODEL), lambda i: (i, 0)),
                  pl.BlockSpec((D_MODEL, D_MODEL), lambda i: (0, 0)),
                  pl.BlockSpec((1, D_MODEL), lambda i: (0, 0))],
        out_specs=pl.BlockSpec((tm, D_MODEL), lambda i: (i, 0)),
        out_shape=jax.ShapeDtypeStruct((L, D_MODEL), F32),
        compiler_params=_cp(("parallel",)),
        name="out_proj",
    )(x, m, w_out, gain.reshape(1, D_MODEL))


def _mlp_kernel(x_ref, g1_ref, w1_ref, w2_ref, g2_ref, o_ref, h_scr):
    j = pl.program_id(1)

    @pl.when(j == 0)
    def _():
        x = x_ref[...]
        ms = jnp.mean(x * x, axis=-1, keepdims=True)
        h_scr[...] = (x * lax.rsqrt(ms + NORM_EPS) * g1_ref[...]).astype(BF16)
        o_ref[...] = jnp.zeros_like(o_ref)

    a = _dot(h_scr[...], w1_ref[...])
    a = jnp.maximum(a, 0.0)
    a = (a * a).astype(BF16)
    o_ref[...] += _dot(a, w2_ref[...])

    @pl.when(j == pl.num_programs(1) - 1)
    def _():
        f = o_ref[...]
        f = f * lax.rsqrt(jnp.mean(f * f, axis=-1, keepdims=True) + NORM_EPS) * g2_ref[...]
        o_ref[...] = x_ref[...] + f


def mlp(x, g1, w1, w2, g2, tm=512, tf=1024):
    L = x.shape[0]
    return pl.pallas_call(
        _mlp_kernel,
        grid=(L // tm, D_FF // tf),
        in_specs=[pl.BlockSpec((tm, D_MODEL), lambda i, j: (i, 0)),
                  pl.BlockSpec((1, D_MODEL), lambda i, j: (0, 0)),
                  pl.BlockSpec((D_MODEL, tf), lambda i, j: (0, j)),
                  pl.BlockSpec((tf, D_MODEL), lambda i, j: (j, 0)),
                  pl.BlockSpec((1, D_MODEL), lambda i, j: (0, 0))],
        out_specs=pl.BlockSpec((tm, D_MODEL), lambda i, j: (i, 0)),
        out_shape=jax.ShapeDtypeStruct((L, D_MODEL), F32),
        scratch_shapes=[pltpu.VMEM((tm, D_MODEL), BF16)],
        compiler_params=_cp(("parallel", "arbitrary")),
        name="mlp",
    )(x, g1.reshape(1, D_MODEL), w1, w2, g2.reshape(1, D_MODEL))


def _mlp_w32_kernel(x_ref, g1_ref, w1_ref, w2_ref, g2_ref, o_ref, h_scr):
    j = pl.program_id(1)

    @pl.when(j == 0)
    def _():
        x = x_ref[...]
        ms = jnp.mean(x * x, axis=-1, keepdims=True)
        h_scr[...] = (x * lax.rsqrt(ms + NORM_EPS) * g1_ref[...]).astype(BF16)
        o_ref[...] = jnp.zeros_like(o_ref)

    a = _dot(h_scr[...], w1_ref[...].astype(BF16))
    a = jnp.maximum(a, 0.0)
    a = (a * a).astype(BF16)
    o_ref[...] += _dot(a, w2_ref[...].astype(BF16))

    @pl.when(j == pl.num_programs(1) - 1)
    def _():
        f = o_ref[...]
        f = f * lax.rsqrt(jnp.mean(f * f, axis=-1, keepdims=True) + NORM_EPS) * g2_ref[...]
        o_ref[...] = x_ref[...] + f


def mlp_w32(x, g1, w1_all, w2_all, g2, layer, tm=1024, tf=256):
    L = x.shape[0]
    return pl.pallas_call(
        _mlp_w32_kernel,
        grid=(L // tm, D_FF // tf),
        in_specs=[pl.BlockSpec((tm, D_MODEL), lambda i, j: (i, 0)),
                  pl.BlockSpec((1, D_MODEL), lambda i, j: (0, 0)),
                  pl.BlockSpec((None, D_MODEL, tf), lambda i, j: (layer, 0, j)),
                  pl.BlockSpec((None, tf, D_MODEL), lambda i, j: (layer, j, 0)),
                  pl.BlockSpec((1, D_MODEL), lambda i, j: (0, 0))],
        out_specs=pl.BlockSpec((tm, D_MODEL), lambda i, j: (i, 0)),
        out_shape=jax.ShapeDtypeStruct((L, D_MODEL), F32),
        scratch_shapes=[pltpu.VMEM((tm, D_MODEL), BF16)],
        compiler_params=_cp(("parallel", "arbitrary")),
        name="mlp",
    )(x, g1.reshape(1, D_MODEL), w1_all, w2_all, g2.reshape(1, D_MODEL))


def _mix_weight(w):
    half = MLA_ROPE // 2
    z = lambda n: jnp.zeros((D_MODEL, n), w.dtype)
    kr = w[:, _OFF_KR:_OFF_KR + MLA_ROPE]
    cols = [
        w[:, _OFF_QKV:_OFF_QKV + 3 * BW],
        w[:, _OFF_U:_OFF_U + BW],
        w[:, _OFF_Z:_OFF_Z + BW],
        w[:, _OFF_SWAQ:_OFF_SWAQ + BW],
        w[:, _OFF_CKV:_OFF_CKV + MLA_KV_RANK],
        w[:, _OFF_SWAKV:_OFF_SWAKV + 256],
        w[:, _OFF_CQ:_OFF_CQ + MLA_Q_RANK],
        kr, z(64),
        kr[:, half:], kr[:, :half], z(64),
        w[:, _OFF_BETA:_OFF_BETA + 16], z(112),
    ]
    return jnp.concatenate(cols, axis=1).astype(BF16)


def kernel(x, w_in, s5_lam_re, s5_lam_im, s5_log_step, s5_b_re, s5_b_im, s5_c_re, s5_c_im, s5_d, s5_w_glu, s5_b_glu, gdn_conv, gdn_a_log, gdn_dt_bias, gdn_o_gain, swa_sink, t5_bias, mla_q_gain, mla_kv_gain, mla_w_uq, mla_w_ukv, w_branch, w_out, mix_pre_gain, mix_post_gain, mlp_pre_gain, mlp_post_gain, w_mlp_in, w_mlp_out):
    B_, L, _ = x.shape
    outs = []
    for b in range(B_):
        xb = x[b].astype(F32)
        for l in range(DEPTH):
            w_mix = _mix_weight(w_in[l])
            w_gate = w_in[l][:, _OFF_GATE:].astype(BF16)
            proj, h = in_proj(xb, mix_pre_gain[l].astype(F32), w_mix, tm=1024)
            y_a = s5_mixer(proj, s5_lam_re[l], s5_lam_im[l], s5_log_step[l],
                           s5_b_re[l], s5_b_im[l], s5_c_re[l], s5_c_im[l], s5_d[l], s5_w_glu[l],
                           s5_b_glu[l], ucol=MIX_U // BW)
            y_b = gdn_mixer_fused(proj, proj, proj[:, MIX_SMALL:MIX_SMALL + 16], gdn_conv[l], gdn_a_log[l],
                                  gdn_dt_bias[l], gdn_o_gain[l], zcol=MIX_Z // BW)
            y_c = swa_mixer_t(proj, proj, swa_sink[l], t5_bias, qcol=MIX_SWAQ // BW, kvcol=MIX_SWAKV // 256)
            y_d = mla_mixer_t(proj, proj, proj, proj, mla_q_gain[l], mla_kv_gain[l], mla_w_uq[l], mla_w_ukv[l],
                              cols=(MIX_CQ // MLA_Q_RANK, MIX_CKV // MLA_KV_RANK, MIX_KR // 128,
                                    MIX_KRROT // 128))
            merged = merge_branches(h, (y_a, y_b, y_c, y_d), w_gate, w_branch[l].astype(BF16))
            xb = out_proj(xb, merged, w_out[l].astype(BF16), mix_post_gain[l].astype(F32))
            xb = mlp(xb, mlp_pre_gain[l].astype(F32), w_mlp_in[l].astype(BF16),
                     w_mlp_out[l].astype(BF16), mlp_post_gain[l].astype(F32))
        outs.append(xb)
    return jnp.stack(outs).astype(x.dtype)
```

```python
import functools
import math

import numpy as np
import jax
import jax.numpy as jnp
from jax import lax
from jax.experimental import pallas as pl
from jax.experimental.pallas import tpu as pltpu

F32 = jnp.float32
BF16 = jnp.bfloat16

D_MODEL = 2048
DEPTH = 4
BW = 512
D_FF = 4 * D_MODEL
NORM_EPS = 1e-6

S5_GROUP = 16
S5_GROUPS = 32
S5_STATE = 64
S5_NSTATE = S5_GROUPS * S5_STATE
S5_HALF = S5_NSTATE // 2

GDN_HEADS = 4
GDN_DH = 128
GDN_CHUNK = 64

SWA_HEADS = 8
SWA_KV_HEADS = 2
SWA_DH = 64
SWA_BLOCK = 128
WINDOW = 128
T5_BUCKETS = 32
T5_MAX_DISTANCE = 128

MLA_HEADS = 4
MLA_Q_RANK = 384
MLA_KV_RANK = 512
MLA_NOPE = 128
MLA_ROPE = 64
MLA_V = 128
ROPE_THETA = 10000.0

_OFF_U = 0
_OFF_QKV = 512
_OFF_Z = 2048
_OFF_BETA = 2560
_OFF_DECAY = 2568
_OFF_SWAQ = 2576
_OFF_SWAKV = 3088
_OFF_CQ = 3344
_OFF_CKV = 3728
_OFF_KR = 4240
_OFF_GATE = 4304
D_IN = _OFF_GATE + 4 * D_MODEL

MIX_QKV = 0
MIX_U = 1536
MIX_Z = 2048
MIX_SWAQ = 2560
MIX_CKV = 3072
MIX_SWAKV = 3584
MIX_CQ = 3840
MIX_KR = 4224
MIX_KRROT = 4352
MIX_SMALL = 4480
N_MIX = 4608

VMEM_LIMIT = 56 * 1024 * 1024


def _cp(sem, limit=VMEM_LIMIT):
    return pltpu.CompilerParams(dimension_semantics=sem, vmem_limit_bytes=limit)


def _dot(a, b):
    return jnp.dot(a, b, preferred_element_type=F32)


def _dot_nt(a, b):
    return lax.dot_general(a, b, (((1,), (1,)), ((), ())), preferred_element_type=F32)


def _dot_tn(a, b):
    return lax.dot_general(a, b, (((0,), (0,)), ((), ())), preferred_element_type=F32)


def _sigmoid(x):
    return 1.0 / (1.0 + jnp.exp(-x))


def _softplus(x):
    return jnp.maximum(x, 0.0) + jnp.log(1.0 + jnp.exp(-jnp.abs(x)))


def _in_proj_kernel(x_ref, g_ref, w_ref, o_ref, h_ref, h_scr):
    @pl.when(pl.program_id(1) == 0)
    def _():
        x = x_ref[...]
        ms = jnp.mean(x * x, axis=-1, keepdims=True)
        h = (x * lax.rsqrt(ms + NORM_EPS) * g_ref[...]).astype(BF16)
        h_scr[...] = h
        h_ref[...] = h

    o_ref[...] = _dot(h_scr[...], w_ref[...])


def in_proj(x, gain, w_mix, tm=512, tn=512):
    L = x.shape[0]
    n = w_mix.shape[1]
    return pl.pallas_call(
        _in_proj_kernel,
        grid=(L // tm, n // tn),
        in_specs=[
            pl.BlockSpec((tm, D_MODEL), lambda i, j: (i, 0)),
            pl.BlockSpec((1, D_MODEL), lambda i, j: (0, 0)),
            pl.BlockSpec((D_MODEL, tn), lambda i, j: (0, j)),
        ],
        out_specs=[
            pl.BlockSpec((tm, tn), lambda i, j: (i, j)),
            pl.BlockSpec((tm, D_MODEL), lambda i, j: (i, 0)),
        ],
        out_shape=[
            jax.ShapeDtypeStruct((L, n), F32),
            jax.ShapeDtypeStruct((L, D_MODEL), BF16),
        ],
        scratch_shapes=[pltpu.VMEM((tm, D_MODEL), BF16)],
        compiler_params=_cp(("parallel", "arbitrary")),
        name="in_proj",
    )(x, gain.reshape(1, D_MODEL), w_mix)


def _s5_kernel(*refs, reverse, finalize, T):
    if finalize:
        (u_ref, bd_ref, cd_ref, hs_ref, p_ref, yb_ref, dsk_ref, wglu_ref, bglu_ref,
         o_ref, bur, bui, carry) = refs
    else:
        u_ref, bd_ref, cd_ref, hs_ref, p_ref, o_ref, bur, bui, carry = refs
    nt = T // 8

    @pl.when(pl.program_id(0) == 0)
    def _():
        carry[...] = jnp.zeros_like(carry)

    u = u_ref[...]
    ub = u.astype(BF16)
    for k in range(2):
        r = _dot(ub[:, 256 * k:256 * (k + 1)], bd_ref[k])
        bur[:, S5_HALF * k:S5_HALF * (k + 1)] = r[:, :S5_HALF]
        bui[:, S5_HALF * k:S5_HALF * (k + 1)] = r[:, S5_HALF:]

    def tile(kk, c):
        cr, ci = c
        idx = (nt - 1 - kk) if reverse else kk
        t0 = pl.multiple_of(idx * 8, 8)
        xr = bur[pl.ds(t0, 8), :]
        xi = bui[pl.ds(t0, 8), :]
        for s, d in enumerate((1, 2, 4)):
            sh = (8 - d) if reverse else d
            sr = pltpu.roll(xr, sh, 0)
            si = pltpu.roll(xi, sh, 0)
            ar = hs_ref[2 * s]
            ai = hs_ref[2 * s + 1]
            xr, xi = xr + ar * sr - ai * si, xi + ar * si + ai * sr
        pr = p_ref[0]
        pi = p_ref[1]
        xr, xi = xr + pr * cr - pi * ci, xi + pr * ci + pi * cr
        bur[pl.ds(t0, 8), :] = xr
        bui[pl.ds(t0, 8), :] = xi
        row = 0 if reverse else 7
        return (jnp.broadcast_to(xr[row:row + 1, :], (8, S5_NSTATE)),
                jnp.broadcast_to(xi[row:row + 1, :], (8, S5_NSTATE)))

    cr, ci = lax.fori_loop(0, nt, tile, (carry[0], carry[1]))
    carry[0] = cr
    carry[1] = ci

    ys = []
    for k in range(2):
        sr = bur[:, S5_HALF * k:S5_HALF * (k + 1)].astype(BF16)
        si = bui[:, S5_HALF * k:S5_HALF * (k + 1)].astype(BF16)
        ys.append(_dot(sr, cd_ref[k, 0]) + _dot(si, cd_ref[k, 1]))
    y = jnp.concatenate(ys, axis=1)
    if finalize:
        y = y + yb_ref[...] + dsk_ref[...] * u
        y = jax.nn.gelu(y)
        gate = _sigmoid(_dot(y.astype(BF16), wglu_ref[...]) + bglu_ref[...])
        o_ref[...] = (y * gate).astype(o_ref.dtype)
    else:
        o_ref[...] = y


def s5_direction_call(u, bd, cd, hs, pw, *, reverse, fin=None, T=256, ucol=0):
    L = u.shape[0]
    nb = L // T
    rmap = (lambda i: (nb - 1 - i, 0)) if reverse else (lambda i: (i, 0))
    umap = (lambda i: (nb - 1 - i, ucol)) if reverse else (lambda i: (i, ucol))
    c3 = lambda i: (0, 0, 0)
    in_specs = [
        pl.BlockSpec((T, BW), umap),
        pl.BlockSpec((2, 256, S5_NSTATE), c3),
        pl.BlockSpec((2, 2, S5_HALF, 256), lambda i: (0, 0, 0, 0)),
        pl.BlockSpec((6, 8, S5_NSTATE), c3),
        pl.BlockSpec((2, 8, S5_NSTATE), c3),
    ]
    args = [u, bd, cd, hs, pw]
    if fin is not None:
        yb, dsk, wglu, bglu = fin
        in_specs += [
            pl.BlockSpec((T, BW), rmap),
            pl.BlockSpec((1, BW), lambda i: (0, 0)),
            pl.BlockSpec((BW, BW), lambda i: (0, 0)),
            pl.BlockSpec((1, BW), lambda i: (0, 0)),
        ]
        args += [yb, dsk.reshape(1, BW), wglu, bglu.reshape(1, BW)]
    return pl.pallas_call(
        functools.partial(_s5_kernel, reverse=reverse, finalize=fin is not None, T=T),
        grid=(nb,),
        in_specs=in_specs,
        out_specs=pl.BlockSpec((T, BW), rmap),
        out_shape=jax.ShapeDtypeStruct((L, BW), BF16 if fin is not None else F32),
        scratch_shapes=[
            pltpu.VMEM((T, S5_NSTATE), F32),
            pltpu.VMEM((T, S5_NSTATE), F32),
            pltpu.VMEM((2, 8, S5_NSTATE), F32),
        ],
        compiler_params=_cp(("arbitrary",)),
        name="s5_rev" if reverse else "s5_fwd",
    )(*args)


def _s5_prep(lam_re, lam_im, log_step, b_re, b_im, c_re, c_im, reverse):
    G, P, Hg = S5_GROUPS, S5_STATE, S5_GROUP
    lam_re = jnp.minimum(lam_re.astype(F32), -1e-4)
    lam_im = lam_im.astype(F32)
    dt = jnp.exp(log_step.astype(F32))[:, None]
    mag = jnp.exp(lam_re * dt)
    abar_r = mag * jnp.cos(lam_im * dt)
    abar_i = mag * jnp.sin(lam_im * dt)
    den = lam_re * lam_re + lam_im * lam_im
    xr = abar_r - 1.0
    xi = abar_i
    coef_r = (xr * lam_re + xi * lam_im) / den
    coef_i = (xi * lam_re - xr * lam_im) / den
    b_re = b_re.astype(F32)
    b_im = b_im.astype(F32)
    bbar_r = coef_r[..., None] * b_re - coef_i[..., None] * b_im
    bbar_i = coef_r[..., None] * b_im + coef_i[..., None] * b_re

    eye = jnp.eye(16, dtype=F32)

    def bdiag_in(bb):
        blk = bb.transpose(0, 2, 1).reshape(2, 16, Hg, P)
        return jnp.einsum('gq,kghp->kghqp', eye, blk).reshape(2, 16 * Hg, 16 * P)

    bd = jnp.concatenate([bdiag_in(bbar_r), bdiag_in(bbar_i)], axis=-1).astype(BF16)

    def bdiag_out(cc):
        blk = cc.astype(F32).transpose(0, 2, 1).reshape(2, 16, P, Hg)
        return jnp.einsum('gq,kgph->kgpqh', eye, blk).reshape(2, 16 * P, 16 * Hg)

    cd = jnp.stack([bdiag_out(c_re), -bdiag_out(c_im)], axis=1).astype(BF16)

    ar = abar_r.reshape(-1)
    ai = abar_i.reshape(-1)

    def cm(a, b):
        return a[0] * b[0] - a[1] * b[1], a[0] * b[1] + a[1] * b[0]

    a1 = (ar, ai)
    a2 = cm(a1, a1)
    a3 = cm(a2, a1)
    a4 = cm(a2, a2)
    a5 = cm(a4, a1)
    a6 = cm(a4, a2)
    a7 = cm(a4, a3)
    a8 = cm(a4, a4)
    pows = [a1, a2, a3, a4, a5, a6, a7, a8]
    rows = np.arange(8)[:, None]
    hs = []
    for d, ad in ((1, a1), (2, a2), (4, a4)):
        mask = jnp.asarray((rows <= 7 - d) if reverse else (rows >= d), F32)
        hs.append(mask * ad[0][None, :])
        hs.append(mask * ad[1][None, :])
    hs = jnp.stack(hs)
    order = list(range(7, -1, -1)) if reverse else list(range(8))
    pw = jnp.stack([jnp.stack([pows[t][0] for t in order]),
                    jnp.stack([pows[t][1] for t in order])])
    return bd, cd, hs, pw


def s5_mixer(u, lam_re, lam_im, log_step, b_re, b_im, c_re, c_im, d_skip, w_glu, b_glu, T=256, ucol=0):
    pf = _s5_prep(lam_re[0], lam_im[0], log_step[0], b_re[0], b_im[0], c_re[0], c_im[0], False)
    pb = _s5_prep(lam_re[1], lam_im[1], log_step[1], b_re[1], b_im[1], c_re[1], c_im[1], True)
    yb = s5_direction_call(u, *pb, reverse=True, T=T, ucol=ucol)
    return s5_direction_call(u, *pf, reverse=False, T=T, ucol=ucol,
                             fin=(yb, d_skip.astype(F32), w_glu.astype(BF16), b_glu.astype(F32)))


S5_TT = 16
S5_GB = 8
S5_NPAIR = S5_GROUPS // S5_GB
S5_PW = S5_GB * S5_TT * S5_GROUP


def _hs_params(ar, ai, reverse):
    def cm(a, b):
        return a[0] * b[0] - a[1] * b[1], a[0] * b[1] + a[1] * b[0]

    a1 = (ar, ai)
    a2 = cm(a1, a1)
    a3 = cm(a2, a1)
    a4 = cm(a2, a2)
    pows = [a1, a2, a3, a4, cm(a4, a1), cm(a4, a2), cm(a4, a3), cm(a4, a4)]
    rows = np.arange(8)[:, None]
    hs = []
    for d, ad in ((1, a1), (2, a2), (4, a4)):
        mask = jnp.asarray((rows <= 7 - d) if reverse else (rows >= d), F32)
        hs.append(mask * ad[0][None, :])
        hs.append(mask * ad[1][None, :])
    order = list(range(7, -1, -1)) if reverse else list(range(8))
    pw = jnp.stack([jnp.stack([pows[t][0] for t in order]),
                    jnp.stack([pows[t][1] for t in order])])
    return jnp.stack(hs), pw


def _s5_tile_params(lam_re, lam_im, log_step, b_re, b_im, c_re, c_im, reverse):
    G, P, Hg, TT = S5_GROUPS, S5_STATE, S5_GROUP, S5_TT
    hp = lax.Precision.HIGHEST
    lam_re = jnp.minimum(lam_re.astype(F32), -1e-4)
    lam_im = lam_im.astype(F32)
    dt = jnp.exp(log_step.astype(F32))[:, None]
    mag = jnp.exp(lam_re * dt)
    abar_r = mag * jnp.cos(lam_im * dt)
    abar_i = mag * jnp.sin(lam_im * dt)
    den = lam_re * lam_re + lam_im * lam_im
    xr = abar_r - 1.0
    xi = abar_i
    coef_r = (xr * lam_re + xi * lam_im) / den
    coef_i = (xi * lam_re - xr * lam_im) / den
    b_re = b_re.astype(F32)
    b_im = b_im.astype(F32)
    bb_r = coef_r[..., None] * b_re - coef_i[..., None] * b_im
    bb_i = coef_r[..., None] * b_im + coef_i[..., None] * b_re
    cr = c_re.astype(F32)
    ci = c_im.astype(F32)
    pr = [jnp.ones_like(abar_r)]
    pi = [jnp.zeros_like(abar_i)]
    for _ in range(TT):
        pr.append(pr[-1] * abar_r - pi[-1] * abar_i)
        pi.append(pr[-2] * abar_i + pi[-1] * abar_r)
    pr = jnp.stack(pr)
    pi = jnp.stack(pi)
    car = cr[None] * pr[:TT, :, None, :] - ci[None] * pi[:TT, :, None, :]
    cai = cr[None] * pi[:TT, :, None, :] + ci[None] * pr[:TT, :, None, :]
    kd = (jnp.einsum('dgkp,gph->dgkh', car, bb_r, precision=hp)
          - jnp.einsum('dgkp,gph->dgkh', cai, bb_i, precision=hp))
    ii = np.arange(TT)[:, None]
    jj = np.arange(TT)[None, :]
    lag = (ii - jj) if reverse else (jj - ii)
    shift = jnp.asarray((lag[None] == np.arange(TT)[:, None, None]), F32)
    toep = jnp.einsum('dij,dgkh->gihjk', shift, kd, precision=hp)
    eye2 = jnp.eye(S5_GB, dtype=F32)
    GB = S5_GB
    e_idx = np.arange(TT) if reverse else (TT - 1 - np.arange(TT))
    er = pr[e_idx]
    ei = pi[e_idx]
    wf_r = (er[..., None] * bb_r[None] - ei[..., None] * bb_i[None]).transpose(1, 0, 3, 2)
    wf_i = (er[..., None] * bb_i[None] + ei[..., None] * bb_r[None]).transpose(1, 0, 3, 2)

    def pair_rows(w):
        return jnp.einsum('ab,kaihp->kiahbp', eye2, w.reshape(S5_NPAIR, GB, TT, Hg, P)).reshape(
            S5_NPAIR, S5_PW, GB * P)

    wf = jnp.concatenate([pair_rows(wf_r), pair_rows(wf_i)], axis=-1).astype(BF16)
    f_idx = (TT - np.arange(TT)) if reverse else (np.arange(TT) + 1)
    fr = pr[f_idx]
    fi = pi[f_idx]
    mr = (cr[None] * fr[:, :, None, :] - ci[None] * fi[:, :, None, :]).transpose(1, 3, 0, 2)
    mi = (cr[None] * fi[:, :, None, :] + ci[None] * fr[:, :, None, :]).transpose(1, 3, 0, 2)

    def pair_cols(w):
        return jnp.einsum('ab,kapjc->kapjbc', eye2, w.reshape(S5_NPAIR, GB, P, TT, Hg)).reshape(
            S5_NPAIR, GB * P, S5_PW)

    wc = jnp.concatenate([pair_cols(mr), pair_cols(-mi)], axis=1).astype(BF16)
    hs, pw = _hs_params(pr[TT].reshape(-1), pi[TT].reshape(-1), reverse)
    return toep, wf, wc, hs, pw


def _s5_state_kernel(u_ref, wff_ref, wfb_ref, f_ref):
    u = u_ref[...]
    for d, w_ref in enumerate((wff_ref, wfb_ref)):
        f = _dot(u, w_ref[...])
        f_ref[d, 0] = f[:, :S5_GB * S5_STATE]
        f_ref[d, 1] = f[:, S5_GB * S5_STATE:]


def _s5_scan_kernel(f_ref, hs_ref, p_ref, o_ref, pad_r, pad_i, *, reverse, N):
    nt = N // 8
    z8 = jnp.zeros((8, S5_NSTATE), F32)
    if reverse:
        pad_r[0:N] = f_ref[0]
        pad_i[0:N] = f_ref[1]
        pad_r[N:N + 8] = z8
        pad_i[N:N + 8] = z8
        o_ref[0] = pad_r[1:N + 1]
        o_ref[1] = pad_i[1:N + 1]
    else:
        pad_r[0:8] = z8
        pad_i[0:8] = z8
        pad_r[8:N + 8] = f_ref[0]
        pad_i[8:N + 8] = f_ref[1]
        o_ref[0] = pad_r[7:N + 7]
        o_ref[1] = pad_i[7:N + 7]

    def tile(kk, c):
        cr, ci = c
        idx = (nt - 1 - kk) if reverse else kk
        t0 = pl.multiple_of(idx * 8, 8)
        xr = o_ref[0, pl.ds(t0, 8), :]
        xi = o_ref[1, pl.ds(t0, 8), :]
        for s, d in enumerate((1, 2, 4)):
            sh = (8 - d) if reverse else d
            sr = pltpu.roll(xr, sh, 0)
            si = pltpu.roll(xi, sh, 0)
            ar = hs_ref[2 * s]
            ai = hs_ref[2 * s + 1]
            xr, xi = xr + ar * sr - ai * si, xi + ar * si + ai * sr
        pr = p_ref[0]
        pi = p_ref[1]
        xr, xi = xr + pr * cr - pi * ci, xi + pr * ci + pi * cr
        o_ref[0, pl.ds(t0, 8), :] = xr
        o_ref[1, pl.ds(t0, 8), :] = xi
        row = 0 if reverse else 7
        return (jnp.broadcast_to(xr[row:row + 1, :], (8, S5_NSTATE)),
                jnp.broadcast_to(xi[row:row + 1, :], (8, S5_NSTATE)))

    lax.fori_loop(0, nt, tile, (z8, z8))


def _s5_out_kernel(u_ref, toep_ref, cf_ref, cb_ref, wc_ref, y_ref):
    u = u_ref[...]
    st = jnp.concatenate([cf_ref[0], cf_ref[1], cb_ref[0], cb_ref[1]], axis=1).astype(BF16)
    y_ref[...] = _dot(u, toep_ref[...]) + _dot(st, wc_ref[...])


def _s5_final_kernel(y_ref, u_ref, dsk_ref, wglu_ref, bglu_ref, o_ref):
    y = y_ref[...] + dsk_ref[...] * u_ref[...]
    y = jax.nn.gelu(y)
    gate = _sigmoid(_dot(y.astype(BF16), wglu_ref[...]) + bglu_ref[...])
    o_ref[...] = (y * gate).astype(o_ref.dtype)


def s5_mixer_tiled(u, lam_re, lam_im, log_step, b_re, b_im, c_re, c_im, d_skip, w_glu, b_glu, ucol=0, RF=512):
    L = u.shape[0]
    TT, Hg = S5_TT, S5_GROUP
    N = L // TT
    NP = S5_NPAIR
    tf, wff, wcf, hsf, pwf = _s5_tile_params(lam_re[0], lam_im[0], log_step[0], b_re[0], b_im[0],
                                             c_re[0], c_im[0], False)
    tb, wfb, wcb, hsb, pwb = _s5_tile_params(lam_re[1], lam_im[1], log_step[1], b_re[1], b_im[1],
                                             c_re[1], c_im[1], True)
    GB = S5_GB
    eye = jnp.eye(GB, dtype=F32)
    toep = jnp.einsum('ab,kaihjc->kiahjbc', eye, (tf + tb).reshape(NP, GB, TT, Hg, TT, Hg)).reshape(
        NP, S5_PW, S5_PW).astype(BF16)
    wc = jnp.concatenate([wcf, wcb], axis=1)
    uu = u[:, ucol * BW:(ucol + 1) * BW]
    up = uu.reshape(N, TT, NP, GB * Hg).transpose(2, 0, 1, 3).reshape(NP, N, S5_PW).astype(BF16)
    pair = lambda a, b: pl.BlockSpec((None, a, b), lambda k: (k, 0, 0))
    f = pl.pallas_call(
        _s5_state_kernel,
        grid=(NP,),
        in_specs=[pair(N, S5_PW), pair(S5_PW, 2 * GB * S5_STATE), pair(S5_PW, 2 * GB * S5_STATE)],
        out_specs=pl.BlockSpec((2, 2, N, GB * S5_STATE), lambda k: (0, 0, 0, k)),
        out_shape=jax.ShapeDtypeStruct((2, 2, N, S5_NSTATE), F32),
        compiler_params=_cp(("parallel",)),
        name="s5_state",
    )(up, wff, wfb)
    cs = []
    for d, (hs, pw, reverse) in enumerate(((hsf, pwf, False), (hsb, pwb, True))):
        cs.append(pl.pallas_call(
            functools.partial(_s5_scan_kernel, reverse=reverse, N=N),
            grid=(1,),
            in_specs=[pl.BlockSpec((None, 2, N, S5_NSTATE), lambda i, d=d: (d, 0, 0, 0)),
                      pl.BlockSpec((6, 8, S5_NSTATE), lambda i: (0, 0, 0)),
                      pl.BlockSpec((2, 8, S5_NSTATE), lambda i: (0, 0, 0))],
            out_specs=pl.BlockSpec((2, N, S5_NSTATE), lambda i: (0, 0, 0)),
            out_shape=jax.ShapeDtypeStruct((2, N, S5_NSTATE), F32),
            scratch_shapes=[pltpu.VMEM((N + 8, S5_NSTATE), F32), pltpu.VMEM((N + 8, S5_NSTATE), F32)],
            compiler_params=_cp(("arbitrary",)),
            name="s5_scan_rev" if reverse else "s5_scan_fwd",
        )(f, hs, pw))
    NR = min(N, 256)
    cspec = pl.BlockSpec((2, NR, GB * S5_STATE), lambda k, r: (0, r, k))
    rows = pl.BlockSpec((None, NR, S5_PW), lambda k, r: (k, r, 0))
    wres = pl.BlockSpec((None, S5_PW, S5_PW), lambda k, r: (k, 0, 0))
    yflat = pl.pallas_call(
        _s5_out_kernel,
        grid=(NP, N // NR),
        in_specs=[rows, wres, cspec, cspec, wres],
        out_specs=rows,
        out_shape=jax.ShapeDtypeStruct((NP, N, S5_PW), F32),
        compiler_params=_cp(("parallel", "arbitrary")),
        name="s5_out",
    )(up, toep, cs[0], cs[1], wc)
    y = yflat.reshape(NP, N, TT, GB * Hg).transpose(1, 2, 0, 3).reshape(L, BW)
    row = pl.BlockSpec((RF, BW), lambda i: (i, 0))
    vec = pl.BlockSpec((1, BW), lambda i: (0, 0))
    return pl.pallas_call(
        _s5_final_kernel,
        grid=(L // RF,),
        in_specs=[row, pl.BlockSpec((RF, BW), lambda i: (i, ucol)), vec,
                  pl.BlockSpec((BW, BW), lambda i: (0, 0)), vec],
        out_specs=row,
        out_shape=jax.ShapeDtypeStruct((L, BW), BF16),
        compiler_params=_cp(("parallel",)),
        name="s5_final",
    )(y, u, d_skip.astype(F32).reshape(1, BW), w_glu.astype(BF16), b_glu.astype(F32).reshape(1, BW))


def _gdn_kernel(*refs, reverse, finalize, R, nb):
    if finalize:
        (qkv_ref, prev_ref, next_ref, cw_ref, sm_ref, smt_ref, gpr_ref, gpc_ref,
         ob_ref, z_ref, og_ref, o_ref, s_scr, q_scr, k_scr, v_scr, g_scr, b_scr, o_scr) = refs
    else:
        (qkv_ref, prev_ref, next_ref, cw_ref, sm_ref, smt_ref, gpr_ref, gpc_ref,
         o_ref, s_scr, q_scr, k_scr, v_scr, g_scr, b_scr, o_scr) = refs
    C = GDN_CHUNK
    nc = R // C
    i = pl.program_id(0)

    @pl.when(i == 0)
    def _():
        s_scr[...] = jnp.zeros_like(s_scr)

    blk = (nb - 1 - i) if reverse else i
    x = qkv_ref[...]
    pv = jnp.where(blk > 0, prev_ref[...], 0.0)
    nx = jnp.where(blk < nb - 1, next_ref[...], 0.0)
    xp = jnp.concatenate([pv, x, nx], axis=0)
    w = cw_ref[...]
    conv = (w[0:1] * xp[6:6 + R] + w[1:2] * xp[7:7 + R]
            + w[2:3] * xp[8:8 + R] + w[3:4] * xp[9:9 + R])
    act = conv * _sigmoid(conv)
    for h in range(GDN_HEADS):
        sl = slice(h * GDN_DH, (h + 1) * GDN_DH)
        qh = act[:, sl]
        q_scr[:, sl] = qh * lax.rsqrt(jnp.sum(qh * qh, axis=-1, keepdims=True) + 1e-6) * (GDN_DH ** -0.5)
        kh = act[:, BW + h * GDN_DH:BW + (h + 1) * GDN_DH]
        k_scr[:, sl] = kh * lax.rsqrt(jnp.sum(kh * kh, axis=-1, keepdims=True) + 1e-6)
    v_scr[...] = act[:, 2 * BW:3 * BW]
    sm = sm_ref[...]
    b_scr[...] = _sigmoid(sm[:, 0:4])
    g_scr[...] = gpr_ref[0:1, :] * _softplus(sm[:, 4:8] + gpr_ref[1:2, :])

    ii = lax.broadcasted_iota(jnp.int32, (C, C), 0)
    jj = lax.broadcasted_iota(jnp.int32, (C, C), 1)
    if reverse:
        incl = jj >= ii
        strict = jj > ii
    else:
        incl = jj <= ii
        strict = jj < ii
    tri_l = jnp.where(incl, 1.0, 0.0).astype(F32)
    tri_u = jnp.where((ii >= jj) if reverse else (ii <= jj), 1.0, 0.0).astype(F32)
    last = 0 if reverse else C - 1
    eye = jnp.where(ii == jj, 1.0, 0.0).astype(F32)
    pair_masks = []
    for lvl in range(6):
        bi = ii >> lvl
        bj = jj >> lvl
        lo_, hi_ = (bi, bj) if reverse else (bj, bi)
        pair_masks.append(((ii >> (lvl + 1)) == (jj >> (lvl + 1))) & ((hi_ & 1) == 1) & ((lo_ & 1) == 0))

    def chunk(cc, carry):
        cidx = (nc - 1 - cc) if reverse else cc
        r0 = pl.multiple_of(cidx * C, C)
        gch = g_scr[pl.ds(r0, C), :]
        bch = b_scr[pl.ds(r0, C), :]
        gt = gpc_ref[:, 0:1] * _softplus(smt_ref[cidx][4:8, :] + gpc_ref[:, 1:2])
        gc_col = jnp.dot(tri_l, gch, preferred_element_type=F32, precision=lax.Precision.HIGHEST)
        gc_row = jnp.dot(gt, tri_u, preferred_element_type=F32, precision=lax.Precision.HIGHEST)
        for h in range(GDN_HEADS):
            sl = slice(h * GDN_DH, (h + 1) * GDN_DH)
            gcol = gc_col[:, h:h + 1]
            grow = gc_row[h:h + 1, :]
            glast = gc_col[last:last + 1, h:h + 1]
            decay = jnp.exp(jnp.where(incl, gcol - grow, -jnp.inf))
            k = k_scr[pl.ds(r0, C), sl]
            q = q_scr[pl.ds(r0, C), sl]
            v = v_scr[pl.ds(r0, C), sl]
            beta = bch[:, h:h + 1]
            kb = k * beta
            kbf = k.astype(BF16)
            kk = _dot_nt(kb.astype(BF16), kbf)
            qk = _dot_nt(q.astype(BF16), kbf)
            a = jnp.where(strict, kk * decay, 0.0)
            attn = qk * decay
            xs = jnp.concatenate([v * beta, kb * jnp.exp(gcol)], axis=1)
            minv = eye - jnp.where(pair_masks[0], a, 0.0)
            for lvl in range(1, 6):
                mb = minv.astype(BF16)
                em = _dot(jnp.where(pair_masks[lvl], a, 0.0).astype(BF16), mb)
                minv = minv - _dot(mb, em.astype(BF16))
            xs = _dot(minv.astype(BF16), xs.astype(BF16))
            u_ = xs[:, :GDN_DH]
            w_ = xs[:, GDN_DH:]
            s_h = s_scr[h]
            sb = s_h.astype(BF16)
            v_new = u_ - _dot(w_.astype(BF16), sb)
            vnb = v_new.astype(BF16)
            o = _dot((q * jnp.exp(gcol)).astype(BF16), sb) + _dot(attn.astype(BF16), vnb)
            kdec = (k * jnp.exp(glast - gcol)).astype(BF16)
            s_scr[h] = s_h * jnp.exp(glast) + _dot_tn(kdec, vnb)
            o_scr[pl.ds(r0, C), sl] = o
        return carry

    lax.fori_loop(0, nc, chunk, 0)

    if finalize:
        o = o_scr[...] + ob_ref[...]
        z = z_ref[...]
        og = og_ref[...]
        for h in range(GDN_HEADS):
            sl = slice(h * GDN_DH, (h + 1) * GDN_DH)
            oh = o[:, sl]
            oh = oh * lax.rsqrt(jnp.mean(oh * oh, axis=-1, keepdims=True) + NORM_EPS) * og
            zh = z[:, sl]
            o_ref[:, sl] = (oh * (zh * _sigmoid(zh))).astype(o_ref.dtype)
    else:
        o_ref[...] = o_scr[...]


def gdn_direction_call(qkv, conv_w, smd, smdt, gpr, gpc, *, reverse, fin=None, R=256):
    L = qkv.shape[0]
    nb = L // R
    r8 = R // 8
    nc = R // GDN_CHUNK
    bmap = (lambda i: nb - 1 - i) if reverse else (lambda i: i)
    in_specs = [
        pl.BlockSpec((R, 3 * BW), lambda i: (bmap(i), 0)),
        pl.BlockSpec((8, 3 * BW), lambda i: (jnp.maximum(bmap(i) * r8 - 1, 0), 0)),
        pl.BlockSpec((8, 3 * BW), lambda i: (jnp.minimum((bmap(i) + 1) * r8, L // 8 - 1), 0)),
        pl.BlockSpec((4, 3 * BW), lambda i: (0, 0)),
        pl.BlockSpec((R, 8), lambda i: (bmap(i), 0)),
        pl.BlockSpec((nc, 8, GDN_CHUNK), lambda i: (bmap(i), 0, 0)),
        pl.BlockSpec((2, 4), lambda i: (0, 0)),
        pl.BlockSpec((4, 2), lambda i: (0, 0)),
    ]
    args = [qkv, qkv, qkv, conv_w, smd, smdt, gpr, gpc]
    if fin is not None:
        ob, z, og = fin
        in_specs += [
            pl.BlockSpec((R, BW), lambda i: (bmap(i), 0)),
            pl.BlockSpec((R, BW), lambda i: (bmap(i), 0)),
            pl.BlockSpec((1, GDN_DH), lambda i: (0, 0)),
        ]
        args += [ob, z, og.reshape(1, GDN_DH)]
    return pl.pallas_call(
        functools.partial(_gdn_kernel, reverse=reverse, finalize=fin is not None, R=R, nb=nb),
        grid=(nb,),
        in_specs=in_specs,
        out_specs=pl.BlockSpec((R, BW), lambda i: (bmap(i), 0)),
        out_shape=jax.ShapeDtypeStruct((L, BW), BF16 if fin is not None else F32),
        scratch_shapes=[
            pltpu.VMEM((GDN_HEADS, GDN_DH, GDN_DH), F32),
            pltpu.VMEM((R, BW), F32),
            pltpu.VMEM((R, BW), F32),
            pltpu.VMEM((R, BW), F32),
            pltpu.VMEM((R, 4), F32),
            pltpu.VMEM((R, 4), F32),
            pltpu.VMEM((R, BW), F32),
        ],
        compiler_params=_cp(("arbitrary",)),
        name="gdn_rev" if reverse else "gdn_fwd",
    )(*args)


def gdn_mixer(qkv, z, small, conv_w, a_log, dt_bias, o_gain, R=256):
    L = qkv.shape[0]
    coef = -jnp.exp(a_log.astype(F32))
    dtb = dt_bias.astype(F32)
    outs = []
    o_b = None
    for d in (1, 0):
        smd = jnp.concatenate([small[:, 4 * d:4 * d + 4], small[:, 8 + 4 * d:12 + 4 * d]], axis=1)
        smdt = smd.reshape(L // GDN_CHUNK, GDN_CHUNK, 8).transpose(0, 2, 1)
        gpr = jnp.stack([coef[d], dtb[d]])
        gpc = gpr.T
        if d == 1:
            o_b = gdn_direction_call(qkv, conv_w.astype(F32), smd, smdt, gpr, gpc, reverse=True, R=R)
        else:
            return gdn_direction_call(qkv, conv_w.astype(F32), smd, smdt, gpr, gpc, reverse=False,
                                      fin=(o_b, z, o_gain.astype(F32)), R=R)


GDN_HC = GDN_HEADS * GDN_CHUNK
GDN_UNROLL = 2


def _stack_heads(ref, r0):
    return jnp.concatenate(
        [ref[pl.ds(r0, GDN_CHUNK), h * GDN_DH:(h + 1) * GDN_DH] for h in range(GDN_HEADS)], axis=0)


def _gdn_pre_kernel(qkv_ref, prev_ref, next_ref, cw_ref, q_ref, k_ref, v_ref, xp_scr, *, R, nb):
    blk = pl.program_id(0)
    xp_scr[0:8] = jnp.where(blk > 0, prev_ref[...], 0.0)
    xp_scr[8:8 + R] = qkv_ref[...]
    xp_scr[8 + R:16 + R] = jnp.where(blk < nb - 1, next_ref[...], 0.0)
    w = cw_ref[...]
    conv = (w[0:1] * xp_scr[6:6 + R] + w[1:2] * xp_scr[7:7 + R]
            + w[2:3] * xp_scr[8:8 + R] + w[3:4] * xp_scr[9:9 + R])
    act = conv * _sigmoid(conv)
    for h in range(GDN_HEADS):
        sl = slice(h * GDN_DH, (h + 1) * GDN_DH)
        qh = act[:, sl]
        q_ref[:, sl] = qh * lax.rsqrt(jnp.sum(qh * qh, axis=-1, keepdims=True) + 1e-6) * (GDN_DH ** -0.5)
        kh = act[:, BW + h * GDN_DH:BW + (h + 1) * GDN_DH]
        k_ref[:, sl] = kh * lax.rsqrt(jnp.sum(kh * kh, axis=-1, keepdims=True) + 1e-6)
    v_ref[...] = act[:, 2 * BW:3 * BW]


def _gdn_prologue(d, R, q_ref, k_ref, v_ref, sm_ref, gt_ref, gpr_ref, gpc_ref,
                  q_scr, k_scr, v_scr, b_scr, gc_scr, gr_scr, tri):
    q_scr[d] = q_ref[...]
    k_scr[d] = k_ref[...]
    v_scr[d] = v_ref[...]
    sm = sm_ref[...]
    b_scr[d] = _sigmoid(sm[:, 0:4])
    g = gpr_ref[2 * d:2 * d + 1, :] * _softplus(sm[:, 4:8] + gpr_ref[2 * d + 1:2 * d + 2, :])
    gt = gpc_ref[:, 2 * d:2 * d + 1] * _softplus(gt_ref[...] + gpc_ref[:, 2 * d + 1:2 * d + 2])
    tri_l, tri_u = tri
    gc_scr[d] = jnp.dot(tri_l, g, preferred_element_type=F32, precision=lax.Precision.HIGHEST)
    gr = jnp.dot(gt, tri_u, preferred_element_type=F32, precision=lax.Precision.HIGHEST)
    for c in range(R // GDN_CHUNK):
        gr_scr[d, c] = gr[:, c * GDN_CHUNK:(c + 1) * GDN_CHUNK]


def _gdn_chunks(insts, m_scr, q_scr, k_scr, v_scr, b_scr, gc_scr, gr_scr, s_scr):
    C = GDN_CHUNK
    H = GDN_HEADS
    n = len(insts)
    rng = range(n)
    ds = [t[0] for t in insts]
    r0s = [pl.multiple_of(t[1] * C, C) for t in insts]
    lasts = [0 if t[2] else C - 1 for t in insts]
    ks = [_stack_heads(k_scr.at[ds[i]], r0s[i]) for i in rng]
    qs = [_stack_heads(q_scr.at[ds[i]], r0s[i]) for i in rng]
    vs = [_stack_heads(v_scr.at[ds[i]], r0s[i]) for i in rng]
    bch = [b_scr[ds[i], pl.ds(r0s[i], C), :] for i in rng]
    gch = [gc_scr[ds[i], pl.ds(r0s[i], C), :] for i in rng]
    grw = [gr_scr[ds[i], insts[i][1]] for i in rng]
    beta = [jnp.concatenate([b[:, h:h + 1] for h in range(H)], axis=0) for b in bch]
    gcol = [jnp.concatenate([g[:, h:h + 1] for h in range(H)], axis=0) for g in gch]
    grow = [jnp.concatenate([g[h:h + 1, :] for h in range(H)], axis=1) for g in grw]
    glast_r = [jnp.concatenate(
        [jnp.broadcast_to(grw[i][h:h + 1, lasts[i]:lasts[i] + 1], (1, C)) for h in range(H)], axis=1)
        for i in rng]
    e_cat = [jnp.concatenate(
        [jnp.broadcast_to(jnp.exp(gch[i][lasts[i]:lasts[i] + 1, h:h + 1]), (1, GDN_DH)) for h in range(H)],
        axis=1) for i in rng]

    kst = [k.T for k in ks]
    kstb = [k.astype(BF16) for k in kst]
    kbs = [ks[i] * beta[i] for i in rng]
    g2 = [_dot(jnp.concatenate([kbs[i], qs[i]], axis=0).astype(BF16), kstb[i]) for i in rng]
    decay = [jnp.exp((gcol[i] - grow[i]) + m_scr[ds[i], 0]) for i in rng]
    a = [g2[i][:GDN_HC] * decay[i] * m_scr[ds[i], 1] for i in rng]
    attn = [(g2[i][GDN_HC:] * decay[i]).astype(BF16) for i in rng]
    minv = [m_scr[ds[i], 8] - a[i] * m_scr[ds[i], 2] for i in rng]
    for lvl in range(1, 6):
        mb = [m.astype(BF16) for m in minv]
        em = [_dot((a[i] * m_scr[ds[i], 2 + lvl]).astype(BF16), mb[i]).astype(BF16) for i in rng]
        minv = [minv[i] - _dot(mb[i], em[i]) for i in rng]
    egc = [jnp.exp(g) for g in gcol]
    rhs = [jnp.concatenate([vs[i] * beta[i], kbs[i] * egc[i]], axis=1).astype(BF16) for i in rng]
    xs = [_dot(minv[i].astype(BF16), rhs[i]) for i in rng]
    qg = [qs[i] * egc[i] for i in rng]
    wq = [jnp.concatenate([xs[i][:, GDN_DH:], qg[i]], axis=0).astype(BF16) for i in rng]
    kdect = [(kst[i] * jnp.exp(glast_r[i] - grow[i])).astype(BF16) for i in rng]

    zero = jnp.zeros((C, GDN_DH), BF16)
    by_dir = {}
    for i in rng:
        by_dir.setdefault(ds[i], []).append(i)
    depth = max(len(v) for v in by_dir.values())
    s_cat = {d: s_scr[d] for d in by_dir}
    for step in range(depth):
        act = [v[step] for v in by_dir.values() if len(v) > step]
        p = [_dot(wq[i], s_cat[ds[i]].astype(BF16)) for i in act]
        vn = []
        qss = []
        for j, i in enumerate(act):
            ws = jnp.concatenate(
                [p[j][h * C:(h + 1) * C, h * GDN_DH:(h + 1) * GDN_DH] for h in range(H)], axis=0)
            qss.append(jnp.concatenate(
                [p[j][GDN_HC + h * C:GDN_HC + (h + 1) * C, h * GDN_DH:(h + 1) * GDN_DH] for h in range(H)],
                axis=0))
            vn.append((xs[i][:, :GDN_DH] - ws).astype(BF16))
        o = [qss[j] + _dot(attn[i], vn[j]) for j, i in enumerate(act)]
        for j, i in enumerate(act):
            vn_bd = jnp.concatenate(
                [jnp.concatenate([vn[j][h * C:(h + 1) * C] if hh == h else zero for hh in range(H)], axis=1)
                 for h in range(H)], axis=0)
            s_cat[ds[i]] = s_cat[ds[i]] * e_cat[i] + _dot(kdect[i], vn_bd)
            o_ref = insts[i][3]
            for h in range(H):
                o_ref[pl.ds(r0s[i], C), h * GDN_DH:(h + 1) * GDN_DH] = o[j][h * C:(h + 1) * C]
    for d in by_dir:
        s_scr[d] = s_cat[d]


def _gdn_bidir_kernel(qf_ref, kf_ref, vf_ref, qb_ref, kb_ref, vb_ref, smf_ref, smb_ref,
                      gtf_ref, gtb_ref, gpr_ref, gpc_ref, of_ref, ob_ref,
                      s_scr, q_scr, k_scr, v_scr, b_scr, gc_scr, gr_scr, m_scr, *, R, nb):
    C = GDN_CHUNK
    nc = R // C
    i = pl.program_id(0)

    @pl.when(i == 0)
    def _():
        s_scr[...] = jnp.zeros_like(s_scr)
        ii = lax.broadcasted_iota(jnp.int32, (GDN_HC, GDN_HC), 0)
        jj = lax.broadcasted_iota(jnp.int32, (GDN_HC, GDN_HC), 1)
        blockd = (ii // C) == (jj // C)
        for d, reverse in enumerate((False, True)):
            incl = blockd & ((jj >= ii) if reverse else (jj <= ii))
            strict = blockd & ((jj > ii) if reverse else (jj < ii))
            m_scr[d, 0] = jnp.where(incl, 0.0, -jnp.inf).astype(F32)
            m_scr[d, 1] = jnp.where(strict, 1.0, 0.0).astype(F32)
            for lvl in range(6):
                bi = ii >> lvl
                bj = jj >> lvl
                lo_, hi_ = (bi, bj) if reverse else (bj, bi)
                pm = ((ii >> (lvl + 1)) == (jj >> (lvl + 1))) & ((hi_ & 1) == 1) & ((lo_ & 1) == 0)
                m_scr[d, 2 + lvl] = jnp.where(pm, 1.0, 0.0).astype(F32)
            m_scr[d, 8] = jnp.where(ii == jj, 1.0, 0.0).astype(F32)

    ri = lax.broadcasted_iota(jnp.int32, (R, R), 0)
    rj = lax.broadcasted_iota(jnp.int32, (R, R), 1)
    same = (ri // C) == (rj // C)
    tris = []
    for reverse in (False, True):
        tl = jnp.where(same & ((rj >= ri) if reverse else (rj <= ri)), 1.0, 0.0).astype(F32)
        tu = jnp.where(same & ((ri >= rj) if reverse else (ri <= rj)), 1.0, 0.0).astype(F32)
        tris.append((tl, tu))
    _gdn_prologue(0, R, qf_ref, kf_ref, vf_ref, smf_ref, gtf_ref, gpr_ref, gpc_ref,
                  q_scr, k_scr, v_scr, b_scr, gc_scr, gr_scr, tris[0])
    _gdn_prologue(1, R, qb_ref, kb_ref, vb_ref, smb_ref, gtb_ref, gpr_ref, gpc_ref,
                  q_scr, k_scr, v_scr, b_scr, gc_scr, gr_scr, tris[1])

    def chunk(cc, carry):
        insts = []
        for uu in range(GDN_UNROLL):
            c = cc * GDN_UNROLL + uu
            insts.append((0, c, False, of_ref))
            insts.append((1, nc - 1 - c, True, ob_ref))
        _gdn_chunks(insts, m_scr, q_scr, k_scr, v_scr, b_scr, gc_scr, gr_scr, s_scr)
        return carry

    lax.fori_loop(0, nc // GDN_UNROLL, chunk, 0)


def _gdn_final_kernel(of_ref, ob_ref, z_ref, og_ref, o_ref):
    o = of_ref[...] + ob_ref[...]
    z = z_ref[...]
    og = og_ref[...]
    for h in range(GDN_HEADS):
        sl = slice(h * GDN_DH, (h + 1) * GDN_DH)
        oh = o[:, sl]
        oh = oh * lax.rsqrt(jnp.mean(oh * oh, axis=-1, keepdims=True) + NORM_EPS) * og
        zh = z[:, sl]
        o_ref[:, sl] = (oh * (zh * _sigmoid(zh))).astype(o_ref.dtype)


def gdn_mixer_fused(qkv, z, small, conv_w, a_log, dt_bias, o_gain, R=256, RF=512, zcol=0):
    L = qkv.shape[0]
    nb = L // R
    r8 = R // 8
    coef = -jnp.exp(a_log.astype(F32))
    dtb = dt_bias.astype(F32)
    gpr = jnp.stack([coef[0], dtb[0], coef[1], dtb[1]])
    gpc = gpr.T
    smf = jnp.concatenate([small[:, 0:4], small[:, 8:12]], axis=1)
    smb = jnp.concatenate([small[:, 4:8], small[:, 12:16]], axis=1)
    gtf = small[:, 8:12].T
    gtb = small[:, 12:16].T
    fmap = lambda i: i
    bmap = lambda i: nb - 1 - i
    blk = lambda m: pl.BlockSpec((R, 3 * BW), lambda i: (m(i), 0))
    prv = lambda m: pl.BlockSpec((8, 3 * BW), lambda i: (jnp.maximum(m(i) * r8 - 1, 0), 0))
    nxt = lambda m: pl.BlockSpec((8, 3 * BW), lambda i: (jnp.minimum((m(i) + 1) * r8, L // 8 - 1), 0))
    full = lambda a, b: pl.BlockSpec((a, b), lambda i: (0, 0))
    rowb = lambda m: pl.BlockSpec((R, BW), lambda i: (m(i), 0))
    qn, kn, vv = pl.pallas_call(
        functools.partial(_gdn_pre_kernel, R=R, nb=nb),
        grid=(nb,),
        in_specs=[blk(fmap), prv(fmap), nxt(fmap), full(4, 3 * BW)],
        out_specs=[rowb(fmap), rowb(fmap), rowb(fmap)],
        out_shape=[jax.ShapeDtypeStruct((L, BW), F32)] * 3,
        scratch_shapes=[pltpu.VMEM((R + 16, 3 * BW), F32)],
        compiler_params=_cp(("parallel",)),
        name="gdn_pre",
    )(qkv, qkv, qkv, conv_w.astype(F32))
    o_f, o_b = pl.pallas_call(
        functools.partial(_gdn_bidir_kernel, R=R, nb=nb),
        grid=(nb,),
        in_specs=[rowb(fmap), rowb(fmap), rowb(fmap), rowb(bmap), rowb(bmap), rowb(bmap),
                  pl.BlockSpec((R, 8), lambda i: (i, 0)),
                  pl.BlockSpec((R, 8), lambda i: (nb - 1 - i, 0)),
                  pl.BlockSpec((4, R), lambda i: (0, i)),
                  pl.BlockSpec((4, R), lambda i: (0, nb - 1 - i)),
                  full(4, 4), full(4, 4)],
        out_specs=[pl.BlockSpec((R, BW), lambda i: (i, 0)),
                   pl.BlockSpec((R, BW), lambda i: (nb - 1 - i, 0))],
        out_shape=[jax.ShapeDtypeStruct((L, BW), F32), jax.ShapeDtypeStruct((L, BW), F32)],
        scratch_shapes=[
            pltpu.VMEM((2, GDN_DH, BW), F32),
            pltpu.VMEM((2, R, BW), F32),
            pltpu.VMEM((2, R, BW), F32),
            pltpu.VMEM((2, R, BW), F32),
            pltpu.VMEM((2, R, 4), F32),
            pltpu.VMEM((2, R, 4), F32),
            pltpu.VMEM((2, R // GDN_CHUNK, 4, GDN_CHUNK), F32),
            pltpu.VMEM((2, 9, GDN_HC, GDN_HC), F32),
        ],
        compiler_params=_cp(("arbitrary",)),
        name="gdn_bidir",
    )(qn, kn, vv, qn, kn, vv, smf, smb, gtf, gtb, gpr, gpc)
    row = pl.BlockSpec((RF, BW), lambda i: (i, 0))
    zrow = pl.BlockSpec((RF, BW), lambda i: (i, zcol))
    return pl.pallas_call(
        _gdn_final_kernel,
        grid=(L // RF,),
        in_specs=[row, row, zrow, pl.BlockSpec((1, GDN_DH), lambda i: (0, 0))],
        out_specs=row,
        out_shape=jax.ShapeDtypeStruct((L, BW), BF16),
        compiler_params=_cp(("parallel",)),
        name="gdn_final",
    )(o_f, o_b, z, o_gain.astype(F32).reshape(1, GDN_DH))


def _t5_bucket(rel):
    nb = T5_BUCKETS // 2
    max_exact = nb // 2
    ret = jnp.where(rel > 0, nb, 0)
    n = jnp.abs(rel)
    nf = jnp.maximum(n, 1).astype(F32)
    large = max_exact + (jnp.log(nf / max_exact) / math.log(T5_MAX_DISTANCE / max_exact)
                         * (nb - max_exact)).astype(jnp.int32)
    large = jnp.minimum(large, nb - 1)
    return ret + jnp.where(n < max_exact, n, large)


def _swa_kernel(q_ref, kp_ref, kc_ref, kn_ref, vp_ref, vc_ref, vn_ref, bias_ref, sink_ref,
                o_ref, *, nb):
    i = pl.program_id(0)
    B = SWA_BLOCK
    G = SWA_HEADS // SWA_KV_HEADS
    qi = lax.broadcasted_iota(jnp.int32, (G * B, 3 * B), 0) % B
    sj = lax.broadcasted_iota(jnp.int32, (G * B, 3 * B), 1)
    rel = sj - B - qi
    lo = jnp.where(i == 0, B, 0)
    hi = jnp.where(i == nb - 1, 2 * B, 3 * B)
    valid = (jnp.abs(rel) <= WINDOW) & (sj >= lo) & (sj < hi)
    for g in range(SWA_KV_HEADS):
        kb = jnp.concatenate([kp_ref[g], kc_ref[g], kn_ref[g]], axis=0).astype(BF16)
        vb = jnp.concatenate([vp_ref[g], vc_ref[g], vn_ref[g]], axis=0).astype(BF16)
        q = q_ref[g * G:(g + 1) * G].reshape(G * B, SWA_DH).astype(BF16)
        bias = bias_ref[g * G:(g + 1) * G].reshape(G * B, 3 * B)
        s = _dot_nt(q, kb) * (SWA_DH ** -0.5) + bias
        s = jnp.where(valid, s, -1e30)
        sink = jnp.broadcast_to(sink_ref[g * G:(g + 1) * G], (G, B, 1)).reshape(G * B, 1)
        m = jnp.maximum(jnp.max(s, axis=-1, keepdims=True), sink)
        p = jnp.exp(s - m)
        den = jnp.sum(p, axis=-1, keepdims=True) + jnp.exp(sink - m)
        o = _dot(p.astype(BF16), vb) / den
        o_ref[g * G:(g + 1) * G] = o.reshape(G, B, SWA_DH).astype(o_ref.dtype)


def swa_mixer(q, kv, sink, t5_bias):
    L = q.shape[0]
    B = SWA_BLOCK
    nb = L // B
    qh = q.reshape(L, SWA_HEADS, SWA_DH).transpose(1, 0, 2)
    kh = kv[:, :SWA_KV_HEADS * SWA_DH].reshape(L, SWA_KV_HEADS, SWA_DH).transpose(1, 0, 2)
    vh = kv[:, SWA_KV_HEADS * SWA_DH:].reshape(L, SWA_KV_HEADS, SWA_DH).transpose(1, 0, 2)
    rel = jnp.arange(3 * B)[None, :] - B - jnp.arange(B)[:, None]
    bias = t5_bias.astype(F32)[_t5_bucket(rel)].transpose(2, 0, 1)
    kspec = lambda f: pl.BlockSpec((SWA_KV_HEADS, B, SWA_DH), lambda i: (0, f(i), 0))
    prev = lambda i: jnp.maximum(i - 1, 0)
    cur = lambda i: i
    nxt = lambda i: jnp.minimum(i + 1, nb - 1)
    out = pl.pallas_call(
        functools.partial(_swa_kernel, nb=nb),
        grid=(nb,),
        in_specs=[
            pl.BlockSpec((SWA_HEADS, B, SWA_DH), lambda i: (0, i, 0)),
            kspec(prev), kspec(cur), kspec(nxt),
            kspec(prev), kspec(cur), kspec(nxt),
            pl.BlockSpec((SWA_HEADS, B, 3 * B), lambda i: (0, 0, 0)),
            pl.BlockSpec((SWA_HEADS, 1, 1), lambda i: (0, 0, 0)),
        ],
        out_specs=pl.BlockSpec((SWA_HEADS, B, SWA_DH), lambda i: (0, i, 0)),
        out_shape=jax.ShapeDtypeStruct((SWA_HEADS, L, SWA_DH), BF16),
        compiler_params=_cp(("parallel",)),
        name="swa",
    )(qh, kh, kh, kh, vh, vh, vh, bias, sink.astype(F32).reshape(SWA_HEADS, 1, 1))
    return out.transpose(1, 0, 2).reshape(L, SWA_HEADS * SWA_DH)


SWA_G = SWA_HEADS // SWA_KV_HEADS
SWA_VT_ROWS = SWA_DH + 16


def _swa_t_kernel(q_ref, kvp_ref, kvc_ref, kvn_ref, bias_ref, sink_ref, o_ref, *, nb):
    i = pl.program_id(0)
    B = SWA_BLOCK
    KD = SWA_KV_HEADS * SWA_DH
    sj = lax.broadcasted_iota(jnp.int32, (3 * B, SWA_G * B), 0)
    qi = lax.broadcasted_iota(jnp.int32, (3 * B, SWA_G * B), 1) % B
    rel = sj - B - qi
    lo = jnp.where(i == 0, B, 0)
    hi = jnp.where(i == nb - 1, 2 * B, 3 * B)
    valid = (jnp.abs(rel) <= WINDOW) & (sj >= lo) & (sj < hi)
    qt = (q_ref[...] * (SWA_DH ** -0.5)).T
    kb = jnp.concatenate([kvp_ref[:, 0:KD], kvc_ref[:, 0:KD], kvn_ref[:, 0:KD]], axis=0).astype(BF16)
    vt = jnp.concatenate([kvp_ref[:, KD:2 * KD].T, kvc_ref[:, KD:2 * KD].T, kvn_ref[:, KD:2 * KD].T],
                         axis=1).astype(BF16)
    zq = jnp.zeros((SWA_DH, SWA_G * B), F32)
    ones = jnp.ones((SWA_VT_ROWS - SWA_DH, 3 * B), BF16)
    outs = []
    for g in range(SWA_KV_HEADS):
        qg = jnp.concatenate([qt[(g * SWA_G + hh) * SWA_DH:(g * SWA_G + hh + 1) * SWA_DH, :]
                              for hh in range(SWA_G)], axis=1)
        qpad = jnp.concatenate([qg if gg == g else zq for gg in range(SWA_KV_HEADS)], axis=0).astype(BF16)
        s = _dot(kb, qpad) + bias_ref[g]
        s = jnp.where(valid, s, -1e30)
        sink = sink_ref[g]
        m = jnp.maximum(jnp.max(s, axis=0, keepdims=True), sink)
        p = jnp.exp(s - m).astype(BF16)
        vg = jnp.concatenate([vt[g * SWA_DH:(g + 1) * SWA_DH, :], ones], axis=0)
        pv = _dot(vg, p)
        den = pv[SWA_DH:SWA_DH + 1] + jnp.exp(sink - m)
        o = pv[0:SWA_DH] / den
        outs += [o[:, hh * B:(hh + 1) * B] for hh in range(SWA_G)]
    ot = jnp.concatenate(outs, axis=0)
    o_ref[...] = ot.T.astype(o_ref.dtype)


def swa_mixer_t(q, kv, sink, t5_bias, qcol=0, kvcol=0):
    L = q.shape[0]
    B = SWA_BLOCK
    nb = L // B
    rel = jnp.arange(3 * B)[None, :] - B - jnp.arange(B)[:, None]
    onehot = (_t5_bucket(rel)[..., None] == jnp.arange(T5_BUCKETS)).astype(F32)
    bias = jnp.einsum('qsb,bh->hqs', onehot, t5_bias.astype(F32), precision=lax.Precision.HIGHEST)
    bias_t = bias.reshape(SWA_KV_HEADS, SWA_G, B, 3 * B).transpose(0, 3, 1, 2).reshape(
        SWA_KV_HEADS, 3 * B, SWA_G * B)
    sink_r = jnp.broadcast_to(sink.astype(F32).reshape(SWA_KV_HEADS, 1, SWA_G, 1),
                              (SWA_KV_HEADS, 1, SWA_G, B)).reshape(SWA_KV_HEADS, 1, SWA_G * B)
    kvspec = lambda f: pl.BlockSpec((B, 2 * SWA_KV_HEADS * SWA_DH), lambda i: (f(i), kvcol))
    return pl.pallas_call(
        functools.partial(_swa_t_kernel, nb=nb),
        grid=(nb,),
        in_specs=[
            pl.BlockSpec((B, SWA_HEADS * SWA_DH), lambda i: (i, qcol)),
            kvspec(lambda i: jnp.maximum(i - 1, 0)), kvspec(lambda i: i),
            kvspec(lambda i: jnp.minimum(i + 1, nb - 1)),
            pl.BlockSpec((SWA_KV_HEADS, 3 * B, SWA_G * B), lambda i: (0, 0, 0)),
            pl.BlockSpec((SWA_KV_HEADS, 1, SWA_G * B), lambda i: (0, 0, 0)),
        ],
        out_specs=pl.BlockSpec((B, SWA_HEADS * SWA_DH), lambda i: (i, 0)),
        out_shape=jax.ShapeDtypeStruct((L, SWA_HEADS * SWA_DH), BF16),
        compiler_params=_cp(("parallel",)),
        name="swa",
    )(q, kv, kv, kv, bias_t, sink_r)


def _mla_prep_kernel(cq_ref, ckv_ref, kr_ref, krr_ref, cos_ref, sin_ref, qg_ref, kvg_ref,
                     wqn_ref, wqp_ref, wqr_ref, wkv_ref, q_ref, k_ref, v_ref):
    cq = cq_ref[...]
    qn = (cq * lax.rsqrt(jnp.mean(cq * cq, axis=-1, keepdims=True) + NORM_EPS) * qg_ref[...]).astype(BF16)
    ckv = ckv_ref[...]
    kn = (ckv * lax.rsqrt(jnp.mean(ckv * ckv, axis=-1, keepdims=True) + NORM_EPS) * kvg_ref[...]).astype(BF16)
    cos = cos_ref[...]
    sin = sin_ref[...]
    q_nope = _dot(qn, wqn_ref[...])
    q_pe = _dot(qn, wqp_ref[...])
    q_rot = _dot(qn, wqr_ref[...])
    kv = _dot(kn, wkv_ref[...])
    k_pe = (kr_ref[...] * cos + krr_ref[...] * sin).astype(BF16)
    for h in range(MLA_HEADS):
        sl = slice(h * 128, (h + 1) * 128)
        q_ref[h, :, 0:128] = q_nope[:, sl].astype(BF16)
        q_ref[h, :, 128:256] = (q_pe[:, sl] * cos + q_rot[:, sl] * sin).astype(BF16)
        k_ref[h, :, 0:128] = kv[:, h * 256:h * 256 + 128].astype(BF16)
        k_ref[h, :, 128:256] = k_pe
        v_ref[h] = kv[:, h * 256 + 128:(h + 1) * 256].astype(BF16)


def _mla_attn_kernel(q_ref, k_ref, v_ref, o_ref, *, tk, nk):
    q = q_ref[0]
    tq = q.shape[0]
    scale = (MLA_NOPE + MLA_ROPE) ** -0.5

    def body(kc, carry):
        m, l, acc = carry
        k0 = pl.multiple_of(kc * tk, tk)
        k = k_ref[0, pl.ds(k0, tk), :]
        v = v_ref[0, pl.ds(k0, tk), :]
        s = _dot_nt(q, k) * scale
        m_new = jnp.maximum(m, jnp.max(s, axis=-1, keepdims=True))
        alpha = jnp.exp(m - m_new)
        p = jnp.exp(s - m_new)
        l = alpha * l + jnp.sum(p, axis=-1, keepdims=True)
        acc = alpha * acc + _dot(p.astype(BF16), v)
        return m_new, l, acc

    m0 = jnp.full((tq, 1), -jnp.inf, F32)
    l0 = jnp.zeros((tq, 1), F32)
    a0 = jnp.zeros((tq, MLA_V), F32)
    m, l, acc = lax.fori_loop(0, nk, body, (m0, l0, a0))
    o_ref[...] = (acc / l).astype(o_ref.dtype)


def mla_mixer(c_q, c_kv, kr_pad, krr_pad, q_gain, kv_gain, w_uq, w_ukv, R=512, tq=512, tk=512):
    L = c_q.shape[0]
    H = MLA_HEADS
    half = MLA_ROPE // 2
    pos = jnp.arange(L, dtype=F32)
    inv_freq = ROPE_THETA ** (-jnp.arange(0, MLA_ROPE, 2, dtype=F32) / MLA_ROPE)
    ang = pos[:, None] * inv_freq[None, :]
    cos, sin = jnp.cos(ang), jnp.sin(ang)
    zpad = jnp.zeros((L, 128 - MLA_ROPE), F32)
    cos_t = jnp.concatenate([cos, cos, zpad], axis=1)
    sin_t = jnp.concatenate([-sin, sin, zpad], axis=1)
    wq = w_uq.reshape(MLA_Q_RANK, H, MLA_NOPE + MLA_ROPE)
    wqn = wq[:, :, :MLA_NOPE].reshape(MLA_Q_RANK, H * MLA_NOPE).astype(BF16)
    wpe = wq[:, :, MLA_NOPE:]
    zw = jnp.zeros((MLA_Q_RANK, H, 128 - MLA_ROPE), w_uq.dtype)
    wqp = jnp.concatenate([wpe, zw], axis=2).reshape(MLA_Q_RANK, H * 128).astype(BF16)
    wrot = jnp.concatenate([wpe[:, :, half:], wpe[:, :, :half], zw], axis=2)
    wqr = wrot.reshape(MLA_Q_RANK, H * 128).astype(BF16)
    wkv = w_ukv.astype(BF16)
    row = lambda w: pl.BlockSpec((R, w), lambda i: (i, 0))
    full = lambda a, b: pl.BlockSpec((a, b), lambda i: (0, 0))
    q, k, v = pl.pallas_call(
        _mla_prep_kernel,
        grid=(L // R,),
        in_specs=[row(MLA_Q_RANK), row(MLA_KV_RANK), row(128), row(128), row(128), row(128),
                  full(1, MLA_Q_RANK), full(1, MLA_KV_RANK),
                  full(MLA_Q_RANK, H * 128), full(MLA_Q_RANK, H * 128), full(MLA_Q_RANK, H * 128),
                  full(MLA_KV_RANK, H * 256)],
        out_specs=[pl.BlockSpec((H, R, 256), lambda i: (0, i, 0)),
                   pl.BlockSpec((H, R, 256), lambda i: (0, i, 0)),
                   pl.BlockSpec((H, R, 128), lambda i: (0, i, 0))],
        out_shape=[jax.ShapeDtypeStruct((H, L, 256), BF16),
                   jax.ShapeDtypeStruct((H, L, 256), BF16),
                   jax.ShapeDtypeStruct((H, L, 128), BF16)],
        compiler_params=_cp(("parallel",)),
        name="mla_prep",
    )(c_q, c_kv, kr_pad, krr_pad, cos_t, sin_t,
      q_gain.astype(F32).reshape(1, -1), kv_gain.astype(F32).reshape(1, -1), wqn, wqp, wqr, wkv)
    return pl.pallas_call(
        functools.partial(_mla_attn_kernel, tk=tk, nk=L // tk),
        grid=(H, L // tq),
        in_specs=[pl.BlockSpec((1, tq, 256), lambda h, i: (h, i, 0)),
                  pl.BlockSpec((1, L, 256), lambda h, i: (h, 0, 0)),
                  pl.BlockSpec((1, L, 128), lambda h, i: (h, 0, 0))],
        out_specs=pl.BlockSpec((tq, MLA_V), lambda h, i: (i, h)),
        out_shape=jax.ShapeDtypeStruct((L, H * MLA_V), BF16),
        compiler_params=_cp(("parallel", "parallel")),
        name="mla_attn",
    )(q, k, v)


MLA_VT_ROWS = MLA_V + 16
LOG2E = 1.4426950408889634


def _mla_prep_t_kernel(cq_ref, ckv_ref, kr_ref, krr_ref, cos_ref, sin_ref, cost_ref, sint_ref,
                       qg_ref, kvg_ref, wqn_ref, wqp_ref, wqr_ref, wk_ref, wv_ref,
                       qt_ref, k_ref, vt_ref):
    cq = cq_ref[...]
    qn = (cq * lax.rsqrt(jnp.mean(cq * cq, axis=-1, keepdims=True) + NORM_EPS) * qg_ref[...]).astype(BF16)
    ckv = ckv_ref[...]
    kn = (ckv * lax.rsqrt(jnp.mean(ckv * ckv, axis=-1, keepdims=True) + NORM_EPS) * kvg_ref[...]).astype(BF16)
    c = (MLA_NOPE + MLA_ROPE) ** -0.5 * LOG2E
    qt_nope = _dot_nt(wqn_ref[...], qn)
    qt_pe = _dot_nt(wqp_ref[...], qn)
    qt_rot = _dot_nt(wqr_ref[...], qn)
    cost = cost_ref[...]
    sint = sint_ref[...]
    k_nope = _dot(kn, wk_ref[...])
    vt = _dot_nt(wv_ref[...], kn)
    k_pe = (kr_ref[...] * cos_ref[...] + krr_ref[...] * sin_ref[...]).astype(BF16)
    ones = jnp.ones((MLA_VT_ROWS - MLA_V, vt.shape[1]), BF16)
    for h in range(MLA_HEADS):
        sl = slice(h * 128, (h + 1) * 128)
        qt_ref[h, 0:128, :] = (qt_nope[sl] * c).astype(BF16)
        qt_ref[h, 128:256, :] = ((qt_pe[sl] * cost + qt_rot[sl] * sint) * c).astype(BF16)
        k_ref[h, :, 0:128] = k_nope[:, sl].astype(BF16)
        k_ref[h, :, 128:256] = k_pe
        vt_ref[h, 0:MLA_V, :] = vt[sl].astype(BF16)
        vt_ref[h, MLA_V:MLA_VT_ROWS, :] = ones


def _mla_attn_t_kernel(qt_ref, k_ref, vt_ref, o_ref, s_scr, *, tk, nk, tsub):
    tq = qt_ref.shape[2]
    nsub = tq // tsub
    subs = range(nsub)
    qts = [qt_ref[0, :, j * tsub:(j + 1) * tsub] for j in subs]

    def put_scores(kc, slot):
        k0 = pl.multiple_of(kc * tk, tk)
        kblk = k_ref[0, pl.ds(k0, tk), :]
        for j in subs:
            s_scr[slot, j] = _dot(kblk, qts[j])

    def consume(kc, slot, ms, accs):
        k0 = pl.multiple_of(kc * tk, tk)
        vt = vt_ref[0, :, pl.ds(k0, tk)]
        ss = [s_scr[slot, j] for j in subs]
        m_new = [jnp.maximum(ms[j], jnp.max(ss[j], axis=0, keepdims=True)) for j in subs]
        ps = [jnp.exp2(ss[j] - m_new[j]).astype(BF16) for j in subs]
        pv = [_dot(vt, ps[j]) for j in subs]
        accs = [jnp.exp2(ms[j] - m_new[j]) * accs[j] + pv[j] for j in subs]
        return m_new, accs

    def body(kk, carry):
        ms, accs = carry
        kc = 2 * kk
        put_scores(kc + 1, 1)
        ms, accs = consume(kc, 0, list(ms), list(accs))
        put_scores(jnp.minimum(kc + 2, nk - 1), 0)
        ms, accs = consume(kc + 1, 1, ms, accs)
        return tuple(ms), tuple(accs)

    m0 = tuple(jnp.full((1, tsub), -jnp.inf, F32) for _ in subs)
    a0 = tuple(jnp.zeros((MLA_VT_ROWS, tsub), F32) for _ in subs)
    put_scores(0, 0)
    ms, accs = lax.fori_loop(0, nk // 2, body, (m0, a0))
    for j in subs:
        o = accs[j][0:MLA_V] / accs[j][MLA_V:MLA_V + 1]
        o_ref[j * tsub:(j + 1) * tsub, :] = o.T.astype(o_ref.dtype)


def _rope_tables(L):
    pos = np.arange(L, dtype=np.float32)
    inv_freq = np.float32(ROPE_THETA) ** (-np.arange(0, MLA_ROPE, 2, dtype=np.float32) / np.float32(MLA_ROPE))
    ang = pos[:, None] * inv_freq.astype(np.float32)[None, :]
    return jnp.asarray(np.cos(ang), F32), jnp.asarray(np.sin(ang), F32)


def mla_mixer_t(c_q, c_kv, kr_pad, krr_pad, q_gain, kv_gain, w_uq, w_ukv, R=512, tq=1024, tk=512, tsub=256,
                cols=(0, 0, 0, 0)):
    L = c_q.shape[0]
    H = MLA_HEADS
    half = MLA_ROPE // 2
    cos, sin = _rope_tables(L)
    zpad = jnp.zeros((L, 128 - MLA_ROPE), F32)
    cos_t = jnp.concatenate([cos, cos, zpad], axis=1)
    sin_t = jnp.concatenate([-sin, sin, zpad], axis=1)
    wq = w_uq.reshape(MLA_Q_RANK, H, MLA_NOPE + MLA_ROPE)
    wqn = wq[:, :, :MLA_NOPE].reshape(MLA_Q_RANK, H * MLA_NOPE).T.astype(BF16)
    wpe = wq[:, :, MLA_NOPE:]
    zw = jnp.zeros((MLA_Q_RANK, H, 128 - MLA_ROPE), w_uq.dtype)
    wqp = jnp.concatenate([wpe, zw], axis=2).reshape(MLA_Q_RANK, H * 128).T.astype(BF16)
    wrot = jnp.concatenate([wpe[:, :, half:], wpe[:, :, :half], zw], axis=2)
    wqr = wrot.reshape(MLA_Q_RANK, H * 128).T.astype(BF16)
    wkv = w_ukv.reshape(MLA_KV_RANK, H, MLA_NOPE + MLA_V)
    wk = wkv[:, :, :MLA_NOPE].reshape(MLA_KV_RANK, H * MLA_NOPE).astype(BF16)
    wv = wkv[:, :, MLA_NOPE:].reshape(MLA_KV_RANK, H * MLA_V).T.astype(BF16)
    row = lambda w, c=0: pl.BlockSpec((R, w), lambda i: (i, c))
    col = lambda w: pl.BlockSpec((w, R), lambda i: (0, i))
    full = lambda a, b: pl.BlockSpec((a, b), lambda i: (0, 0))
    qt, k, vt = pl.pallas_call(
        _mla_prep_t_kernel,
        grid=(L // R,),
        in_specs=[row(MLA_Q_RANK, cols[0]), row(MLA_KV_RANK, cols[1]), row(128, cols[2]), row(128, cols[3]),
                  row(128), row(128), col(128), col(128),
                  full(1, MLA_Q_RANK), full(1, MLA_KV_RANK),
                  full(H * 128, MLA_Q_RANK), full(H * 128, MLA_Q_RANK), full(H * 128, MLA_Q_RANK),
                  full(MLA_KV_RANK, H * 128), full(H * 128, MLA_KV_RANK)],
        out_specs=[pl.BlockSpec((H, 256, R), lambda i: (0, 0, i)),
                   pl.BlockSpec((H, R, 256), lambda i: (0, i, 0)),
                   pl.BlockSpec((H, MLA_VT_ROWS, R), lambda i: (0, 0, i))],
        out_shape=[jax.ShapeDtypeStruct((H, 256, L), BF16),
                   jax.ShapeDtypeStruct((H, L, 256), BF16),
                   jax.ShapeDtypeStruct((H, MLA_VT_ROWS, L), BF16)],
        compiler_params=_cp(("parallel",)),
        name="mla_prep",
    )(c_q, c_kv, kr_pad, krr_pad, cos_t, sin_t, cos_t.T, sin_t.T,
      q_gain.astype(F32).reshape(1, -1), kv_gain.astype(F32).reshape(1, -1), wqn, wqp, wqr, wk, wv)
    return pl.pallas_call(
        functools.partial(_mla_attn_t_kernel, tk=tk, nk=L // tk, tsub=tsub),
        grid=(H, L // tq),
        in_specs=[pl.BlockSpec((1, 256, tq), lambda h, i: (h, 0, i)),
                  pl.BlockSpec((1, L, 256), lambda h, i: (h, 0, 0)),
                  pl.BlockSpec((1, MLA_VT_ROWS, L), lambda h, i: (h, 0, 0))],
        out_specs=pl.BlockSpec((tq, MLA_V), lambda h, i: (i, h)),
        out_shape=jax.ShapeDtypeStruct((L, H * MLA_V), BF16),
        scratch_shapes=[pltpu.VMEM((2, tq // tsub, tk, tsub), F32)],
        compiler_params=_cp(("parallel", "parallel")),
        name="mla_attn",
    )(qt, k, vt)


def _merge_kernel(h_ref, ya_ref, yb_ref, yc_ref, yd_ref, g0_ref, g1_ref, g2_ref, g3_ref, p_ref, o_ref):
    h = h_ref[...]
    acc = None
    for b, (y_ref, g_ref) in enumerate(((ya_ref, g0_ref), (yb_ref, g1_ref), (yc_ref, g2_ref), (yd_ref, g3_ref))):
        gate = _sigmoid(_dot(h, g_ref[...]))
        t = gate * _dot(y_ref[...], p_ref[b])
        acc = t if acc is None else acc + t
    o_ref[...] = acc.astype(o_ref.dtype)


def merge_branches(h, ys, w_gate, w_branch, tm=1024, tn=512):
    L = h.shape[0]
    nj = D_MODEL // tn
    gspec = lambda b: pl.BlockSpec((D_MODEL, tn), lambda i, j: (0, b * nj + j))
    yspec = pl.BlockSpec((tm, BW), lambda i, j: (i, 0))
    return pl.pallas_call(
        _merge_kernel,
        grid=(L // tm, nj),
        in_specs=[pl.BlockSpec((tm, D_MODEL), lambda i, j: (i, 0)), yspec, yspec, yspec, yspec,
                  gspec(0), gspec(1), gspec(2), gspec(3),
                  pl.BlockSpec((4, BW, tn), lambda i, j: (0, 0, j))],
        out_specs=pl.BlockSpec((tm, tn), lambda i, j: (i, j)),
        out_shape=jax.ShapeDtypeStruct((L, D_MODEL), BF16),
        compiler_params=_cp(("parallel", "arbitrary")),
        name="merge",
    )(h, *ys, w_gate, w_gate, w_gate, w_gate, w_branch)


def _out_proj_kernel(x_ref, m_ref, w_ref, g_ref, o_ref):
    f = _dot(m_ref[...], w_ref[...])
    f = f * lax.rsqrt(jnp.mean(f * f, axis=-1, keepdims=True) + NORM_EPS) * g_ref[...]
    o_ref[...] = x_ref[...] + f


def out_proj(x, m, w_out, gain, tm=512):
    L = x.shape[0]
    return pl.pallas_call(
        _out_proj_kernel,
        grid=(L // tm,),
        in_specs=[pl.BlockSpec((tm, D_MODEL), lambda i: (i, 0)),
                  pl.BlockSpec((tm, D_MODEL), lambda i: (i, 0)),
                  pl.BlockSpec((D_MODEL, D_MODEL), lambda i: (0, 0)),
                  pl.BlockSpec((1, D_MODEL), lambda i: (0, 0))],
        out_specs=pl.BlockSpec((tm, D_MODEL), lambda i: (i, 0)),
        out_shape=jax.ShapeDtypeStruct((L, D_MODEL), F32),
        compiler_params=_cp(("parallel",)),
        name="out_proj",
    )(x, m, w_out, gain.reshape(1, D_MODEL))


def _mlp_kernel(x_ref, g1_ref, w1_ref, w2_ref, g2_ref, o_ref, h_scr):
    j = pl.program_id(1)

    @pl.when(j == 0)
    def _():
        x = x_ref[...]
        ms = jnp.mean(x * x, axis=-1, keepdims=True)
        h_scr[...] = (x * lax.rsqrt(ms + NORM_EPS) * g1_ref[...]).astype(BF16)
        o_ref[...] = jnp.zeros_like(o_ref)

    a = _dot(h_scr[...], w1_ref[...])
    a = jnp.maximum(a, 0.0)
    a = (a * a).astype(BF16)
    o_ref[...] += _dot(a, w2_ref[...])

    @pl.when(j == pl.num_programs(1) - 1)
    def _():
        f = o_ref[...]
        f = f * lax.rsqrt(jnp.mean(f * f, axis=-1, keepdims=True) + NORM_EPS) * g2_ref[...]
        o_ref[...] = x_ref[...] + f


def mlp(x, g1, w1, w2, g2, tm=512, tf=1024):
    L = x.shape[0]
    return pl.pallas_call(
        _mlp_kernel,
        grid=(L // tm, D_FF // tf),
        in_specs=[pl.BlockSpec((tm, D_MODEL), lambda i, j: (i, 0)),
                  pl.BlockSpec((1, D_MODEL), lambda i, j: (0, 0)),
                  pl.BlockSpec((D_MODEL, tf), lambda i, j: (0, j)),
                  pl.BlockSpec((tf, D_MODEL), lambda i, j: (j, 0)),
                  pl.BlockSpec((1, D_MODEL), lambda i, j: (0, 0))],
        out_specs=pl.BlockSpec((tm, D_MODEL), lambda i, j: (i, 0)),
        out_shape=jax.ShapeDtypeStruct((L, D_MODEL), F32),
        scratch_shapes=[pltpu.VMEM((tm, D_MODEL), BF16)],
        compiler_params=_cp(("parallel", "arbitrary")),
        name="mlp",
    )(x, g1.reshape(1, D_MODEL), w1, w2, g2.reshape(1, D_MODEL))


def _mlp_w32_kernel(x_ref, g1_ref, w1_ref, w2_ref, g2_ref, o_ref, h_scr):
    j = pl.program_id(1)

    @pl.when(j == 0)
    def _():
        x = x_ref[...]
        ms = jnp.mean(x * x, axis=-1, keepdims=True)
        h_scr[...] = (x * lax.rsqrt(ms + NORM_EPS) * g1_ref[...]).astype(BF16)
        o_ref[...] = jnp.zeros_like(o_ref)

    a = _dot(h_scr[...], w1_ref[...].astype(BF16))
    a = jnp.maximum(a, 0.0)
    a = (a * a).astype(BF16)
    o_ref[...] += _dot(a, w2_ref[...].astype(BF16))

    @pl.when(j == pl.num_programs(1) - 1)
    def _():
        f = o_ref[...]
        f = f * lax.rsqrt(jnp.mean(f * f, axis=-1, keepdims=True) + NORM_EPS) * g2_ref[...]
        o_ref[...] = x_ref[...] + f


def mlp_w32(x, g1, w1_all, w2_all, g2, layer, tm=1024, tf=256):
    L = x.shape[0]
    return pl.pallas_call(
        _mlp_w32_kernel,
        grid=(L // tm, D_FF // tf),
        in_specs=[pl.BlockSpec((tm, D_MODEL), lambda i, j: (i, 0)),
                  pl.BlockSpec((1, D_MODEL), lambda i, j: (0, 0)),
                  pl.BlockSpec((None, D_MODEL, tf), lambda i, j: (layer, 0, j)),
                  pl.BlockSpec((None, tf, D_MODEL), lambda i, j: (layer, j, 0)),
                  pl.BlockSpec((1, D_MODEL), lambda i, j: (0, 0))],
        out_specs=pl.BlockSpec((tm, D_MODEL), lambda i, j: (i, 0)),
        out_shape=jax.ShapeDtypeStruct((L, D_MODEL), F32),
        scratch_shapes=[pltpu.VMEM((tm, D_MODEL), BF16)],
        compiler_params=_cp(("parallel", "arbitrary")),
        name="mlp",
    )(x, g1.reshape(1, D_MODEL), w1_all, w2_all, g2.reshape(1, D_MODEL))


def _mix_weight(w):
    half = MLA_ROPE // 2
    z = lambda n: jnp.zeros((D_MODEL, n), w.dtype)
    kr = w[:, _OFF_KR:_OFF_KR + MLA_ROPE]
    cols = [
        w[:, _OFF_QKV:_OFF_QKV + 3 * BW],
        w[:, _OFF_U:_OFF_U + BW],
        w[:, _OFF_Z:_OFF_Z + BW],
        w[:, _OFF_SWAQ:_OFF_SWAQ + BW],
        w[:, _OFF_CKV:_OFF_CKV + MLA_KV_RANK],
        w[:, _OFF_SWAKV:_OFF_SWAKV + 256],
        w[:, _OFF_CQ:_OFF_CQ + MLA_Q_RANK],
        kr, z(64),
        kr[:, half:], kr[:, :half], z(64),
        w[:, _OFF_BETA:_OFF_BETA + 16], z(112),
    ]
    return jnp.concatenate(cols, axis=1).astype(BF16)


def kernel(x, w_in, s5_lam_re, s5_lam_im, s5_log_step, s5_b_re, s5_b_im, s5_c_re, s5_c_im, s5_d, s5_w_glu, s5_b_glu, gdn_conv, gdn_a_log, gdn_dt_bias, gdn_o_gain, swa_sink, t5_bias, mla_q_gain, mla_kv_gain, mla_w_uq, mla_w_ukv, w_branch, w_out, mix_pre_gain, mix_post_gain, mlp_pre_gain, mlp_post_gain, w_mlp_in, w_mlp_out):
    B_, L, _ = x.shape
    outs = []
    for b in range(B_):
        xb = x[b].astype(F32)
        for l in range(DEPTH):
            w_mix = _mix_weight(w_in[l])
            w_gate = w_in[l][:, _OFF_GATE:].astype(BF16)
            proj, h = in_proj(xb, mix_pre_gain[l].astype(F32), w_mix, tm=1024)
            y_a = s5_mixer(proj, s5_lam_re[l], s5_lam_im[l], s5_log_step[l],
                           s5_b_re[l], s5_b_im[l], s5_c_re[l], s5_c_im[l], s5_d[l], s5_w_glu[l],
                           s5_b_glu[l], ucol=MIX_U // BW)
            y_b = gdn_mixer_fused(proj, proj, proj[:, MIX_SMALL:MIX_SMALL + 16], gdn_conv[l], gdn_a_log[l],
                                  gdn_dt_bias[l], gdn_o_gain[l], zcol=MIX_Z // BW)
            y_c = swa_mixer_t(proj, proj, swa_sink[l], t5_bias, qcol=MIX_SWAQ // BW, kvcol=MIX_SWAKV // 256)
            y_d = mla_mixer_t(proj, proj, proj, proj, mla_q_gain[l], mla_kv_gain[l], mla_w_uq[l], mla_w_ukv[l],
                              cols=(MIX_CQ // MLA_Q_RANK, MIX_CKV // MLA_KV_RANK, MIX_KR // 128,
                                    MIX_KRROT // 128))
            merged = merge_branches(h, (y_a, y_b, y_c, y_d), w_gate, w_branch[l].astype(BF16))
            xb = out_proj(xb, merged, w_out[l].astype(BF16), mix_post_gain[l].astype(F32))
            xb = mlp(xb, mlp_pre_gain[l].astype(F32), w_mlp_in[l].astype(BF16),
                     w_mlp_out[l].astype(BF16), mlp_post_gain[l].astype(F32))
        outs.append(xb)
    return jnp.stack(outs).astype(x.dtype)
```

```python
import functools
import math

import numpy as np
import jax
import jax.numpy as jnp
from jax import lax
from jax.experimental import pallas as pl
from jax.experimental.pallas import tpu as pltpu

F32 = jnp.float32
BF16 = jnp.bfloat16

D_MODEL = 2048
DEPTH = 4
BW = 512
D_FF = 4 * D_MODEL
NORM_EPS = 1e-6

S5_GROUP = 16
S5_GROUPS = 32
S5_STATE = 64
S5_NSTATE = S5_GROUPS * S5_STATE
S5_HALF = S5_NSTATE // 2

GDN_HEADS = 4
GDN_DH = 128
GDN_CHUNK = 64

SWA_HEADS = 8
SWA_KV_HEADS = 2
SWA_DH = 64
SWA_BLOCK = 128
WINDOW = 128
T5_BUCKETS = 32
T5_MAX_DISTANCE = 128

MLA_HEADS = 4
MLA_Q_RANK = 384
MLA_KV_RANK = 512
MLA_NOPE = 128
MLA_ROPE = 64
MLA_V = 128
ROPE_THETA = 10000.0

_OFF_U = 0
_OFF_QKV = 512
_OFF_Z = 2048
_OFF_BETA = 2560
_OFF_DECAY = 2568
_OFF_SWAQ = 2576
_OFF_SWAKV = 3088
_OFF_CQ = 3344
_OFF_CKV = 3728
_OFF_KR = 4240
_OFF_GATE = 4304
D_IN = _OFF_GATE + 4 * D_MODEL

MIX_QKV = 0
MIX_U = 1536
MIX_Z = 2048
MIX_SWAQ = 2560
MIX_CKV = 3072
MIX_SWAKV = 3584
MIX_CQ = 3840
MIX_KR = 4224
MIX_KRROT = 4352
MIX_SMALL = 4480
N_MIX = 4608

VMEM_LIMIT = 56 * 1024 * 1024


def _cp(sem, limit=VMEM_LIMIT):
    return pltpu.CompilerParams(dimension_semantics=sem, vmem_limit_bytes=limit)


def _dot(a, b):
    return jnp.dot(a, b, preferred_element_type=F32)


def _dot_nt(a, b):
    return lax.dot_general(a, b, (((1,), (1,)), ((), ())), preferred_element_type=F32)


def _sigmoid(x):
    return 1.0 / (1.0 + jnp.exp(-x))


def _softplus(x):
    return jnp.maximum(x, 0.0) + jnp.log(1.0 + jnp.exp(-jnp.abs(x)))


def _in_proj_kernel(x_ref, g_ref, w_ref, o_ref, h_ref, h_scr):
    @pl.when(pl.program_id(1) == 0)
    def _():
        x = x_ref[...]
        ms = jnp.mean(x * x, axis=-1, keepdims=True)
        h = (x * lax.rsqrt(ms + NORM_EPS) * g_ref[...]).astype(BF16)
        h_scr[...] = h
        h_ref[...] = h

    o_ref[...] = _dot(h_scr[...], w_ref[...])


def in_proj(x, gain, w_mix, tm=512, tn=512):
    L = x.shape[0]
    n = w_mix.shape[1]
    return pl.pallas_call(
        _in_proj_kernel,
        grid=(L // tm, n // tn),
        in_specs=[
            pl.BlockSpec((tm, D_MODEL), lambda i, j: (i, 0)),
            pl.BlockSpec((1, D_MODEL), lambda i, j: (0, 0)),
            pl.BlockSpec((D_MODEL, tn), lambda i, j: (0, j)),
        ],
        out_specs=[
            pl.BlockSpec((tm, tn), lambda i, j: (i, j)),
            pl.BlockSpec((tm, D_MODEL), lambda i, j: (i, 0)),
        ],
        out_shape=[
            jax.ShapeDtypeStruct((L, n), F32),
            jax.ShapeDtypeStruct((L, D_MODEL), BF16),
        ],
        scratch_shapes=[pltpu.VMEM((tm, D_MODEL), BF16)],
        compiler_params=_cp(("parallel", "arbitrary")),
        name="in_proj",
    )(x, gain.reshape(1, D_MODEL), w_mix)


def _s5_kernel(*refs, reverse, finalize, T):
    if finalize:
        (u_ref, bd_ref, cd_ref, hs_ref, p_ref, yb_ref, dsk_ref, wglu_ref, bglu_ref,
         o_ref, bur, bui, carry) = refs
    else:
        u_ref, bd_ref, cd_ref, hs_ref, p_ref, o_ref, bur, bui, carry = refs
    nt = T // 8

    @pl.when(pl.program_id(0) == 0)
    def _():
        carry[...] = jnp.zeros_like(carry)

    u = u_ref[...]
    ub = u.astype(BF16)
    for k in range(2):
        r = _dot(ub[:, 256 * k:256 * (k + 1)], bd_ref[k])
        bur[:, S5_HALF * k:S5_HALF * (k + 1)] = r[:, :S5_HALF]
        bui[:, S5_HALF * k:S5_HALF * (k + 1)] = r[:, S5_HALF:]

    def tile(kk, c):
        cr, ci = c
        idx = (nt - 1 - kk) if reverse else kk
        t0 = pl.multiple_of(idx * 8, 8)
        xr = bur[pl.ds(t0, 8), :]
        xi = bui[pl.ds(t0, 8), :]
        for s, d in enumerate((1, 2, 4)):
            sh = (8 - d) if reverse else d
            sr = pltpu.roll(xr, sh, 0)
            si = pltpu.roll(xi, sh, 0)
            ar = hs_ref[2 * s]
            ai = hs_ref[2 * s + 1]
            xr, xi = xr + ar * sr - ai * si, xi + ar * si + ai * sr
        pr = p_ref[0]
        pi = p_ref[1]
        xr, xi = xr + pr * cr - pi * ci, xi + pr * ci + pi * cr
        bur[pl.ds(t0, 8), :] = xr
        bui[pl.ds(t0, 8), :] = xi
        row = 0 if reverse else 7
        return (jnp.broadcast_to(xr[row:row + 1, :], (8, S5_NSTATE)),
                jnp.broadcast_to(xi[row:row + 1, :], (8, S5_NSTATE)))

    cr, ci = lax.fori_loop(0, nt, tile, (carry[0], carry[1]))
    carry[0] = cr
    carry[1] = ci

    ys = []
    for k in range(2):
        sr = bur[:, S5_HALF * k:S5_HALF * (k + 1)].astype(BF16)
        si = bui[:, S5_HALF * k:S5_HALF * (k + 1)].astype(BF16)
        ys.append(_dot(sr, cd_ref[k, 0]) + _dot(si, cd_ref[k, 1]))
    y = jnp.concatenate(ys, axis=1)
    if finalize:
        y = y + yb_ref[...] + dsk_ref[...] * u
        y = jax.nn.gelu(y)
        gate = _sigmoid(_dot(y.astype(BF16), wglu_ref[...]) + bglu_ref[...])
        o_ref[...] = (y * gate).astype(o_ref.dtype)
    else:
        o_ref[...] = y


def s5_direction_call(u, bd, cd, hs, pw, *, reverse, fin=None, T=256, ucol=0):
    L = u.shape[0]
    nb = L // T
    rmap = (lambda i: (nb - 1 - i, 0)) if reverse else (lambda i: (i, 0))
    umap = (lambda i: (nb - 1 - i, ucol)) if reverse else (lambda i: (i, ucol))
    c3 = lambda i: (0, 0, 0)
    in_specs = [
        pl.BlockSpec((T, BW), umap),
        pl.BlockSpec((2, 256, S5_NSTATE), c3),
        pl.BlockSpec((2, 2, S5_HALF, 256), lambda i: (0, 0, 0, 0)),
        pl.BlockSpec((6, 8, S5_NSTATE), c3),
        pl.BlockSpec((2, 8, S5_NSTATE), c3),
    ]
    args = [u, bd, cd, hs, pw]
    if fin is not None:
        yb, dsk, wglu, bglu = fin
        in_specs += [
            pl.BlockSpec((T, BW), rmap),
            pl.BlockSpec((1, BW), lambda i: (0, 0)),
            pl.BlockSpec((BW, BW), lambda i: (0, 0)),
            pl.BlockSpec((1, BW), lambda i: (0, 0)),
        ]
        args += [yb, dsk.reshape(1, BW), wglu, bglu.reshape(1, BW)]
    return pl.pallas_call(
        functools.partial(_s5_kernel, reverse=reverse, finalize=fin is not None, T=T),
        grid=(nb,),
        in_specs=in_specs,
        out_specs=pl.BlockSpec((T, BW), rmap),
        out_shape=jax.ShapeDtypeStruct((L, BW), BF16 if fin is not None else F32),
        scratch_shapes=[
            pltpu.VMEM((T, S5_NSTATE), F32),
            pltpu.VMEM((T, S5_NSTATE), F32),
            pltpu.VMEM((2, 8, S5_NSTATE), F32),
        ],
        compiler_params=_cp(("arbitrary",)),
        name="s5_rev" if reverse else "s5_fwd",
    )(*args)


def _s5_prep(lam_re, lam_im, log_step, b_re, b_im, c_re, c_im, reverse):
    G, P, Hg = S5_GROUPS, S5_STATE, S5_GROUP
    lam_re = jnp.minimum(lam_re.astype(F32), -1e-4)
    lam_im = lam_im.astype(F32)
    dt = jnp.exp(log_step.astype(F32))[:, None]
    mag = jnp.exp(lam_re * dt)
    abar_r = mag * jnp.cos(lam_im * dt)
    abar_i = mag * jnp.sin(lam_im * dt)
    den = lam_re * lam_re + lam_im * lam_im
    xr = abar_r - 1.0
    xi = abar_i
    coef_r = (xr * lam_re + xi * lam_im) / den
    coef_i = (xi * lam_re - xr * lam_im) / den
    b_re = b_re.astype(F32)
    b_im = b_im.astype(F32)
    bbar_r = coef_r[..., None] * b_re - coef_i[..., None] * b_im
    bbar_i = coef_r[..., None] * b_im + coef_i[..., None] * b_re

    eye = jnp.eye(16, dtype=F32)

    def bdiag_in(bb):
        blk = bb.transpose(0, 2, 1).reshape(2, 16, Hg, P)
        return jnp.einsum('gq,kghp->kghqp', eye, blk).reshape(2, 16 * Hg, 16 * P)

    bd = jnp.concatenate([bdiag_in(bbar_r), bdiag_in(bbar_i)], axis=-1).astype(BF16)

    def bdiag_out(cc):
        blk = cc.astype(F32).transpose(0, 2, 1).reshape(2, 16, P, Hg)
        return jnp.einsum('gq,kgph->kgpqh', eye, blk).reshape(2, 16 * P, 16 * Hg)

    cd = jnp.stack([bdiag_out(c_re), -bdiag_out(c_im)], axis=1).astype(BF16)

    ar = abar_r.reshape(-1)
    ai = abar_i.reshape(-1)

    def cm(a, b):
        return a[0] * b[0] - a[1] * b[1], a[0] * b[1] + a[1] * b[0]

    a1 = (ar, ai)
    a2 = cm(a1, a1)
    a3 = cm(a2, a1)
    a4 = cm(a2, a2)
    a5 = cm(a4, a1)
    a6 = cm(a4, a2)
    a7 = cm(a4, a3)
    a8 = cm(a4, a4)
    pows = [a1, a2, a3, a4, a5, a6, a7, a8]
    rows = np.arange(8)[:, None]
    hs = []
    for d, ad in ((1, a1), (2, a2), (4, a4)):
        mask = jnp.asarray((rows <= 7 - d) if reverse else (rows >= d), F32)
        hs.append(mask * ad[0][None, :])
        hs.append(mask * ad[1][None, :])
    hs = jnp.stack(hs)
    order = list(range(7, -1, -1)) if reverse else list(range(8))
    pw = jnp.stack([jnp.stack([pows[t][0] for t in order]),
                    jnp.stack([pows[t][1] for t in order])])
    return bd, cd, hs, pw


def s5_mixer(u, lam_re, lam_im, log_step, b_re, b_im, c_re, c_im, d_skip, w_glu, b_glu, T=256, ucol=0):
    pf = _s5_prep(lam_re[0], lam_im[0], log_step[0], b_re[0], b_im[0], c_re[0], c_im[0], False)
    pb = _s5_prep(lam_re[1], lam_im[1], log_step[1], b_re[1], b_im[1], c_re[1], c_im[1], True)
    yb = s5_direction_call(u, *pb, reverse=True, T=T, ucol=ucol)
    return s5_direction_call(u, *pf, reverse=False, T=T, ucol=ucol,
                             fin=(yb, d_skip.astype(F32), w_glu.astype(BF16), b_glu.astype(F32)))


GDN_HC = GDN_HEADS * GDN_CHUNK
GDN_UNROLL = 2


def _stack_heads(ref, r0):
    return jnp.concatenate(
        [ref[pl.ds(r0, GDN_CHUNK), h * GDN_DH:(h + 1) * GDN_DH] for h in range(GDN_HEADS)], axis=0)


def _gdn_pre_kernel(qkv_ref, prev_ref, next_ref, cw_ref, q_ref, k_ref, v_ref, xp_scr, *, R, nb):
    blk = pl.program_id(0)
    xp_scr[0:8] = jnp.where(blk > 0, prev_ref[...], 0.0)
    xp_scr[8:8 + R] = qkv_ref[...]
    xp_scr[8 + R:16 + R] = jnp.where(blk < nb - 1, next_ref[...], 0.0)
    w = cw_ref[...]
    conv = (w[0:1] * xp_scr[6:6 + R] + w[1:2] * xp_scr[7:7 + R]
            + w[2:3] * xp_scr[8:8 + R] + w[3:4] * xp_scr[9:9 + R])
    act = conv * _sigmoid(conv)
    for h in range(GDN_HEADS):
        sl = slice(h * GDN_DH, (h + 1) * GDN_DH)
        qh = act[:, sl]
        q_ref[:, sl] = qh * lax.rsqrt(jnp.sum(qh * qh, axis=-1, keepdims=True) + 1e-6) * (GDN_DH ** -0.5)
        kh = act[:, BW + h * GDN_DH:BW + (h + 1) * GDN_DH]
        k_ref[:, sl] = kh * lax.rsqrt(jnp.sum(kh * kh, axis=-1, keepdims=True) + 1e-6)
    v_ref[...] = act[:, 2 * BW:3 * BW]


def _gdn_prologue(d, R, q_ref, k_ref, v_ref, sm_ref, gt_ref, gpr_ref, gpc_ref,
                  q_scr, k_scr, v_scr, b_scr, gc_scr, gr_scr, tri):
    q_scr[d] = q_ref[...]
    k_scr[d] = k_ref[...]
    v_scr[d] = v_ref[...]
    sm = sm_ref[...]
    b_scr[d] = _sigmoid(sm[:, 0:4])
    g = gpr_ref[2 * d:2 * d + 1, :] * _softplus(sm[:, 4:8] + gpr_ref[2 * d + 1:2 * d + 2, :])
    gt = gpc_ref[:, 2 * d:2 * d + 1] * _softplus(gt_ref[...] + gpc_ref[:, 2 * d + 1:2 * d + 2])
    tri_l, tri_u = tri
    gc_scr[d] = jnp.dot(tri_l, g, preferred_element_type=F32, precision=lax.Precision.HIGHEST)
    gr = jnp.dot(gt, tri_u, preferred_element_type=F32, precision=lax.Precision.HIGHEST)
    for c in range(R // GDN_CHUNK):
        gr_scr[d, c] = gr[:, c * GDN_CHUNK:(c + 1) * GDN_CHUNK]


def _gdn_chunks(insts, m_scr, q_scr, k_scr, v_scr, b_scr, gc_scr, gr_scr, s_scr):
    C = GDN_CHUNK
    H = GDN_HEADS
    n = len(insts)
    rng = range(n)
    ds = [t[0] for t in insts]
    r0s = [pl.multiple_of(t[1] * C, C) for t in insts]
    lasts = [0 if t[2] else C - 1 for t in insts]
    ks = [_stack_heads(k_scr.at[ds[i]], r0s[i]) for i in rng]
    qs = [_stack_heads(q_scr.at[ds[i]], r0s[i]) for i in rng]
    vs = [_stack_heads(v_scr.at[ds[i]], r0s[i]) for i in rng]
    bch = [b_scr[ds[i], pl.ds(r0s[i], C), :] for i in rng]
    gch = [gc_scr[ds[i], pl.ds(r0s[i], C), :] for i in rng]
    grw = [gr_scr[ds[i], insts[i][1]] for i in rng]
    beta = [jnp.concatenate([b[:, h:h + 1] for h in range(H)], axis=0) for b in bch]
    gcol = [jnp.concatenate([g[:, h:h + 1] for h in range(H)], axis=0) for g in gch]
    grow = [jnp.concatenate([g[h:h + 1, :] for h in range(H)], axis=1) for g in grw]
    glast_r = [jnp.concatenate(
        [jnp.broadcast_to(grw[i][h:h + 1, lasts[i]:lasts[i] + 1], (1, C)) for h in range(H)], axis=1)
        for i in rng]
    e_cat = [jnp.concatenate(
        [jnp.broadcast_to(jnp.exp(gch[i][lasts[i]:lasts[i] + 1, h:h + 1]), (1, GDN_DH)) for h in range(H)],
        axis=1) for i in rng]

    kst = [k.T for k in ks]
    kstb = [k.astype(BF16) for k in kst]
    kbs = [ks[i] * beta[i] for i in rng]
    g2 = [_dot(jnp.concatenate([kbs[i], qs[i]], axis=0).astype(BF16), kstb[i]) for i in rng]
    decay = [jnp.exp((gcol[i] - grow[i]) + m_scr[ds[i], 0]) for i in rng]
    a = [g2[i][:GDN_HC] * decay[i] * m_scr[ds[i], 1] for i in rng]
    attn = [(g2[i][GDN_HC:] * decay[i]).astype(BF16) for i in rng]
    minv = [m_scr[ds[i], 8] - a[i] * m_scr[ds[i], 2] for i in rng]
    for lvl in range(1, 6):
        mb = [m.astype(BF16) for m in minv]
        em = [_dot((a[i] * m_scr[ds[i], 2 + lvl]).astype(BF16), mb[i]).astype(BF16) for i in rng]
        minv = [minv[i] - _dot(mb[i], em[i]) for i in rng]
    egc = [jnp.exp(g) for g in gcol]
    rhs = [jnp.concatenate([vs[i] * beta[i], kbs[i] * egc[i]], axis=1).astype(BF16) for i in rng]
    xs = [_dot(minv[i].astype(BF16), rhs[i]) for i in rng]
    qg = [qs[i] * egc[i] for i in rng]
    wq = [jnp.concatenate([xs[i][:, GDN_DH:], qg[i]], axis=0).astype(BF16) for i in rng]
    kdect = [(kst[i] * jnp.exp(glast_r[i] - grow[i])).astype(BF16) for i in rng]

    zero = jnp.zeros((C, GDN_DH), BF16)
    by_dir = {}
    for i in rng:
        by_dir.setdefault(ds[i], []).append(i)
    depth = max(len(v) for v in by_dir.values())
    s_cat = {d: s_scr[d] for d in by_dir}
    for step in range(depth):
        act = [v[step] for v in by_dir.values() if len(v) > step]
        p = [_dot(wq[i], s_cat[ds[i]].astype(BF16)) for i in act]
        vn = []
        qss = []
        for j, i in enumerate(act):
            ws = jnp.concatenate(
                [p[j][h * C:(h + 1) * C, h * GDN_DH:(h + 1) * GDN_DH] for h in range(H)], axis=0)
            qss.append(jnp.concatenate(
                [p[j][GDN_HC + h * C:GDN_HC + (h + 1) * C, h * GDN_DH:(h + 1) * GDN_DH] for h in range(H)],
                axis=0))
            vn.append((xs[i][:, :GDN_DH] - ws).astype(BF16))
        o = [qss[j] + _dot(attn[i], vn[j]) for j, i in enumerate(act)]
        for j, i in enumerate(act):
            vn_bd = jnp.concatenate(
                [jnp.concatenate([vn[j][h * C:(h + 1) * C] if hh == h else zero for hh in range(H)], axis=1)
                 for h in range(H)], axis=0)
            s_cat[ds[i]] = s_cat[ds[i]] * e_cat[i] + _dot(kdect[i], vn_bd)
            o_ref = insts[i][3]
            for h in range(H):
                o_ref[pl.ds(r0s[i], C), h * GDN_DH:(h + 1) * GDN_DH] = o[j][h * C:(h + 1) * C]
    for d in by_dir:
        s_scr[d] = s_cat[d]


def _gdn_bidir_kernel(qf_ref, kf_ref, vf_ref, qb_ref, kb_ref, vb_ref, smf_ref, smb_ref,
                      gtf_ref, gtb_ref, gpr_ref, gpc_ref, of_ref, ob_ref,
                      s_scr, q_scr, k_scr, v_scr, b_scr, gc_scr, gr_scr, m_scr, *, R, nb):
    C = GDN_CHUNK
    nc = R // C
    i = pl.program_id(0)

    @pl.when(i == 0)
    def _():
        s_scr[...] = jnp.zeros_like(s_scr)
        ii = lax.broadcasted_iota(jnp.int32, (GDN_HC, GDN_HC), 0)
        jj = lax.broadcasted_iota(jnp.int32, (GDN_HC, GDN_HC), 1)
        blockd = (ii // C) == (jj // C)
        for d, reverse in enumerate((False, True)):
            incl = blockd & ((jj >= ii) if reverse else (jj <= ii))
            strict = blockd & ((jj > ii) if reverse else (jj < ii))
            m_scr[d, 0] = jnp.where(incl, 0.0, -jnp.inf).astype(F32)
            m_scr[d, 1] = jnp.where(strict, 1.0, 0.0).astype(F32)
            for lvl in range(6):
                bi = ii >> lvl
                bj = jj >> lvl
                lo_, hi_ = (bi, bj) if reverse else (bj, bi)
                pm = ((ii >> (lvl + 1)) == (jj >> (lvl + 1))) & ((hi_ & 1) == 1) & ((lo_ & 1) == 0)
                m_scr[d, 2 + lvl] = jnp.where(pm, 1.0, 0.0).astype(F32)
            m_scr[d, 8] = jnp.where(ii == jj, 1.0, 0.0).astype(F32)

    ri = lax.broadcasted_iota(jnp.int32, (R, R), 0)
    rj = lax.broadcasted_iota(jnp.int32, (R, R), 1)
    same = (ri // C) == (rj // C)
    tris = []
    for reverse in (False, True):
        tl = jnp.where(same & ((rj >= ri) if reverse else (rj <= ri)), 1.0, 0.0).astype(F32)
        tu = jnp.where(same & ((ri >= rj) if reverse else (ri <= rj)), 1.0, 0.0).astype(F32)
        tris.append((tl, tu))
    _gdn_prologue(0, R, qf_ref, kf_ref, vf_ref, smf_ref, gtf_ref, gpr_ref, gpc_ref,
                  q_scr, k_scr, v_scr, b_scr, gc_scr, gr_scr, tris[0])
    _gdn_prologue(1, R, qb_ref, kb_ref, vb_ref, smb_ref, gtb_ref, gpr_ref, gpc_ref,
                  q_scr, k_scr, v_scr, b_scr, gc_scr, gr_scr, tris[1])

    def chunk(cc, carry):
        insts = []
        for uu in range(GDN_UNROLL):
            c = cc * GDN_UNROLL + uu
            insts.append((0, c, False, of_ref))
            insts.append((1, nc - 1 - c, True, ob_ref))
        _gdn_chunks(insts, m_scr, q_scr, k_scr, v_scr, b_scr, gc_scr, gr_scr, s_scr)
        return carry

    lax.fori_loop(0, nc // GDN_UNROLL, chunk, 0)


def _gdn_final_kernel(of_ref, ob_ref, z_ref, og_ref, o_ref):
    o = of_ref[...] + ob_ref[...]
    z = z_ref[...]
    og = og_ref[...]
    for h in range(GDN_HEADS):
        sl = slice(h * GDN_DH, (h + 1) * GDN_DH)
        oh = o[:, sl]
        oh = oh * lax.rsqrt(jnp.mean(oh * oh, axis=-1, keepdims=True) + NORM_EPS) * og
        zh = z[:, sl]
        o_ref[:, sl] = (oh * (zh * _sigmoid(zh))).astype(o_ref.dtype)


def gdn_mixer_fused(qkv, z, small, conv_w, a_log, dt_bias, o_gain, R=256, RF=512, zcol=0):
    L = qkv.shape[0]
    nb = L // R
    r8 = R // 8
    coef = -jnp.exp(a_log.astype(F32))
    dtb = dt_bias.astype(F32)
    gpr = jnp.stack([coef[0], dtb[0], coef[1], dtb[1]])
    gpc = gpr.T
    smf = jnp.concatenate([small[:, 0:4], small[:, 8:12]], axis=1)
    smb = jnp.concatenate([small[:, 4:8], small[:, 12:16]], axis=1)
    gtf = small[:, 8:12].T
    gtb = small[:, 12:16].T
    fmap = lambda i: i
    bmap = lambda i: nb - 1 - i
    blk = lambda m: pl.BlockSpec((R, 3 * BW), lambda i: (m(i), 0))
    prv = lambda m: pl.BlockSpec((8, 3 * BW), lambda i: (jnp.maximum(m(i) * r8 - 1, 0), 0))
    nxt = lambda m: pl.BlockSpec((8, 3 * BW), lambda i: (jnp.minimum((m(i) + 1) * r8, L // 8 - 1), 0))
    full = lambda a, b: pl.BlockSpec((a, b), lambda i: (0, 0))
    rowb = lambda m: pl.BlockSpec((R, BW), lambda i: (m(i), 0))
    qn, kn, vv = pl.pallas_call(
        functools.partial(_gdn_pre_kernel, R=R, nb=nb),
        grid=(nb,),
        in_specs=[blk(fmap), prv(fmap), nxt(fmap), full(4, 3 * BW)],
        out_specs=[rowb(fmap), rowb(fmap), rowb(fmap)],
        out_shape=[jax.ShapeDtypeStruct((L, BW), F32)] * 3,
        scratch_shapes=[pltpu.VMEM((R + 16, 3 * BW), F32)],
        compiler_params=_cp(("parallel",)),
        name="gdn_pre",
    )(qkv, qkv, qkv, conv_w.astype(F32))
    o_f, o_b = pl.pallas_call(
        functools.partial(_gdn_bidir_kernel, R=R, nb=nb),
        grid=(nb,),
        in_specs=[rowb(fmap), rowb(fmap), rowb(fmap), rowb(bmap), rowb(bmap), rowb(bmap),
                  pl.BlockSpec((R, 8), lambda i: (i, 0)),
                  pl.BlockSpec((R, 8), lambda i: (nb - 1 - i, 0)),
                  pl.BlockSpec((4, R), lambda i: (0, i)),
                  pl.BlockSpec((4, R), lambda i: (0, nb - 1 - i)),
                  full(4, 4), full(4, 4)],
        out_specs=[pl.BlockSpec((R, BW), lambda i: (i, 0)),
                   pl.BlockSpec((R, BW), lambda i: (nb - 1 - i, 0))],
        out_shape=[jax.ShapeDtypeStruct((L, BW), F32), jax.ShapeDtypeStruct((L, BW), F32)],
        scratch_shapes=[
            pltpu.VMEM((2, GDN_DH, BW), F32),
            pltpu.VMEM((2, R, BW), F32),
            pltpu.VMEM((2, R, BW), F32),
            pltpu.VMEM((2, R, BW), F32),
            pltpu.VMEM((2, R, 4), F32),
            pltpu.VMEM((2, R, 4), F32),
            pltpu.VMEM((2, R // GDN_CHUNK, 4, GDN_CHUNK), F32),
            pltpu.VMEM((2, 9, GDN_HC, GDN_HC), F32),
        ],
        compiler_params=_cp(("arbitrary",)),
        name="gdn_bidir",
    )(qn, kn, vv, qn, kn, vv, smf, smb, gtf, gtb, gpr, gpc)
    row = pl.BlockSpec((RF, BW), lambda i: (i, 0))
    zrow = pl.BlockSpec((RF, BW), lambda i: (i, zcol))
    return pl.pallas_call(
        _gdn_final_kernel,
        grid=(L // RF,),
        in_specs=[row, row, zrow, pl.BlockSpec((1, GDN_DH), lambda i: (0, 0))],
        out_specs=row,
        out_shape=jax.ShapeDtypeStruct((L, BW), BF16),
        compiler_params=_cp(("parallel",)),
        name="gdn_final",
    )(o_f, o_b, z, o_gain.astype(F32).reshape(1, GDN_DH))


SWA_G = SWA_HEADS // SWA_KV_HEADS
SWA_VT_ROWS = SWA_DH + 16


def _t5_bucket(rel):
    nb = T5_BUCKETS // 2
    max_exact = nb // 2
    ret = jnp.where(rel > 0, nb, 0)
    n = jnp.abs(rel)
    nf = jnp.maximum(n, 1).astype(F32)
    large = max_exact + (jnp.log(nf / max_exact) / math.log(T5_MAX_DISTANCE / max_exact)
                         * (nb - max_exact)).astype(jnp.int32)
    large = jnp.minimum(large, nb - 1)
    return ret + jnp.where(n < max_exact, n, large)


def _swa_t_kernel(q_ref, kvp_ref, kvc_ref, kvn_ref, bias_ref, sink_ref, o_ref, *, nb):
    i = pl.program_id(0)
    B = SWA_BLOCK
    KD = SWA_KV_HEADS * SWA_DH
    sj = lax.broadcasted_iota(jnp.int32, (3 * B, SWA_G * B), 0)
    qi = lax.broadcasted_iota(jnp.int32, (3 * B, SWA_G * B), 1) % B
    rel = sj - B - qi
    lo = jnp.where(i == 0, B, 0)
    hi = jnp.where(i == nb - 1, 2 * B, 3 * B)
    valid = (jnp.abs(rel) <= WINDOW) & (sj >= lo) & (sj < hi)
    qt = (q_ref[...] * (SWA_DH ** -0.5)).T
    kb = jnp.concatenate([kvp_ref[:, 0:KD], kvc_ref[:, 0:KD], kvn_ref[:, 0:KD]], axis=0).astype(BF16)
    vt = jnp.concatenate([kvp_ref[:, KD:2 * KD].T, kvc_ref[:, KD:2 * KD].T, kvn_ref[:, KD:2 * KD].T],
                         axis=1).astype(BF16)
    zq = jnp.zeros((SWA_DH, SWA_G * B), F32)
    ones = jnp.ones((SWA_VT_ROWS - SWA_DH, 3 * B), BF16)
    outs = []
    for g in range(SWA_KV_HEADS):
        qg = jnp.concatenate([qt[(g * SWA_G + hh) * SWA_DH:(g * SWA_G + hh + 1) * SWA_DH, :]
                              for hh in range(SWA_G)], axis=1)
        qpad = jnp.concatenate([qg if gg == g else zq for gg in range(SWA_KV_HEADS)], axis=0).astype(BF16)
        s = _dot(kb, qpad) + bias_ref[g]
        s = jnp.where(valid, s, -1e30)
        sink = sink_ref[g]
        m = jnp.maximum(jnp.max(s, axis=0, keepdims=True), sink)
        p = jnp.exp(s - m).astype(BF16)
        vg = jnp.concatenate([vt[g * SWA_DH:(g + 1) * SWA_DH, :], ones], axis=0)
        pv = _dot(vg, p)
        den = pv[SWA_DH:SWA_DH + 1] + jnp.exp(sink - m)
        o = pv[0:SWA_DH] / den
        outs += [o[:, hh * B:(hh + 1) * B] for hh in range(SWA_G)]
    ot = jnp.concatenate(outs, axis=0)
    o_ref[...] = ot.T.astype(o_ref.dtype)


def swa_mixer_t(q, kv, sink, t5_bias, qcol=0, kvcol=0):
    L = q.shape[0]
    B = SWA_BLOCK
    nb = L // B
    rel = jnp.arange(3 * B)[None, :] - B - jnp.arange(B)[:, None]
    onehot = (_t5_bucket(rel)[..., None] == jnp.arange(T5_BUCKETS)).astype(F32)
    bias = jnp.einsum('qsb,bh->hqs', onehot, t5_bias.astype(F32), precision=lax.Precision.HIGHEST)
    bias_t = bias.reshape(SWA_KV_HEADS, SWA_G, B, 3 * B).transpose(0, 3, 1, 2).reshape(
        SWA_KV_HEADS, 3 * B, SWA_G * B)
    sink_r = jnp.broadcast_to(sink.astype(F32).reshape(SWA_KV_HEADS, 1, SWA_G, 1),
                              (SWA_KV_HEADS, 1, SWA_G, B)).reshape(SWA_KV_HEADS, 1, SWA_G * B)
    kvspec = lambda f: pl.BlockSpec((B, 2 * SWA_KV_HEADS * SWA_DH), lambda i: (f(i), kvcol))
    return pl.pallas_call(
        functools.partial(_swa_t_kernel, nb=nb),
        grid=(nb,),
        in_specs=[
            pl.BlockSpec((B, SWA_HEADS * SWA_DH), lambda i: (i, qcol)),
            kvspec(lambda i: jnp.maximum(i - 1, 0)), kvspec(lambda i: i),
            kvspec(lambda i: jnp.minimum(i + 1, nb - 1)),
            pl.BlockSpec((SWA_KV_HEADS, 3 * B, SWA_G * B), lambda i: (0, 0, 0)),
            pl.BlockSpec((SWA_KV_HEADS, 1, SWA_G * B), lambda i: (0, 0, 0)),
        ],
        out_specs=pl.BlockSpec((B, SWA_HEADS * SWA_DH), lambda i: (i, 0)),
        out_shape=jax.ShapeDtypeStruct((L, SWA_HEADS * SWA_DH), BF16),
        compiler_params=_cp(("parallel",)),
        name="swa",
    )(q, kv, kv, kv, bias_t, sink_r)


MLA_VT_ROWS = MLA_V + 16
LOG2E = 1.4426950408889634


def _mla_prep_t_kernel(cq_ref, ckv_ref, kr_ref, krr_ref, cos_ref, sin_ref, cost_ref, sint_ref,
                       qg_ref, kvg_ref, wqn_ref, wqp_ref, wqr_ref, wk_ref, wv_ref,
                       qt_ref, k_ref, vt_ref):
    cq = cq_ref[...]
    qn = (cq * lax.rsqrt(jnp.mean(cq * cq, axis=-1, keepdims=True) + NORM_EPS) * qg_ref[...]).astype(BF16)
    ckv = ckv_ref[...]
    kn = (ckv * lax.rsqrt(jnp.mean(ckv * ckv, axis=-1, keepdims=True) + NORM_EPS) * kvg_ref[...]).astype(BF16)
    c = (MLA_NOPE + MLA_ROPE) ** -0.5 * LOG2E
    qt_nope = _dot_nt(wqn_ref[...], qn)
    qt_pe = _dot_nt(wqp_ref[...], qn)
    qt_rot = _dot_nt(wqr_ref[...], qn)
    cost = cost_ref[...]
    sint = sint_ref[...]
    k_nope = _dot(kn, wk_ref[...])
    vt = _dot_nt(wv_ref[...], kn)
    k_pe = (kr_ref[...] * cos_ref[...] + krr_ref[...] * sin_ref[...]).astype(BF16)
    ones = jnp.ones((MLA_VT_ROWS - MLA_V, vt.shape[1]), BF16)
    for h in range(MLA_HEADS):
        sl = slice(h * 128, (h + 1) * 128)
        qt_ref[h, 0:128, :] = (qt_nope[sl] * c).astype(BF16)
        qt_ref[h, 128:256, :] = ((qt_pe[sl] * cost + qt_rot[sl] * sint) * c).astype(BF16)
        k_ref[h, :, 0:128] = k_nope[:, sl].astype(BF16)
        k_ref[h, :, 128:256] = k_pe
        vt_ref[h, 0:MLA_V, :] = vt[sl].astype(BF16)
        vt_ref[h, MLA_V:MLA_VT_ROWS, :] = ones


def _mla_attn_t_kernel(qt_ref, k_ref, vt_ref, o_ref, s_scr, *, tk, nk, tsub):
    tq = qt_ref.shape[2]
    nsub = tq // tsub
    subs = range(nsub)
    qts = [qt_ref[0, :, j * tsub:(j + 1) * tsub] for j in subs]

    def put_scores(kc, slot):
        k0 = pl.multiple_of(kc * tk, tk)
        kblk = k_ref[0, pl.ds(k0, tk), :]
        for j in subs:
            s_scr[slot, j] = _dot(kblk, qts[j])

    def consume(kc, slot, ms, accs):
        k0 = pl.multiple_of(kc * tk, tk)
        vt = vt_ref[0, :, pl.ds(k0, tk)]
        ss = [s_scr[slot, j] for j in subs]
        m_new = [jnp.maximum(ms[j], jnp.max(ss[j], axis=0, keepdims=True)) for j in subs]
        ps = [jnp.exp2(ss[j] - m_new[j]).astype(BF16) for j in subs]
        pv = [_dot(vt, ps[j]) for j in subs]
        accs = [jnp.exp2(ms[j] - m_new[j]) * accs[j] + pv[j] for j in subs]
        return m_new, accs

    def body(kk, carry):
        ms, accs = carry
        kc = 2 * kk
        put_scores(kc + 1, 1)
        ms, accs = consume(kc, 0, list(ms), list(accs))
        put_scores(jnp.minimum(kc + 2, nk - 1), 0)
        ms, accs = consume(kc + 1, 1, ms, accs)
        return tuple(ms), tuple(accs)

    m0 = tuple(jnp.full((1, tsub), -jnp.inf, F32) for _ in subs)
    a0 = tuple(jnp.zeros((MLA_VT_ROWS, tsub), F32) for _ in subs)
    put_scores(0, 0)
    ms, accs = lax.fori_loop(0, nk // 2, body, (m0, a0))
    for j in subs:
        o = accs[j][0:MLA_V] / accs[j][MLA_V:MLA_V + 1]
        o_ref[j * tsub:(j + 1) * tsub, :] = o.T.astype(o_ref.dtype)


def _rope_tables(L):
    pos = np.arange(L, dtype=np.float32)
    inv_freq = np.float32(ROPE_THETA) ** (-np.arange(0, MLA_ROPE, 2, dtype=np.float32) / np.float32(MLA_ROPE))
    ang = pos[:, None] * inv_freq.astype(np.float32)[None, :]
    return jnp.asarray(np.cos(ang), F32), jnp.asarray(np.sin(ang), F32)


def mla_mixer_t(c_q, c_kv, kr_pad, krr_pad, q_gain, kv_gain, w_uq, w_ukv, R=512, tq=1024, tk=512, tsub=256,
                cols=(0, 0, 0, 0)):
    L = c_q.shape[0]
    H = MLA_HEADS
    half = MLA_ROPE // 2
    cos, sin = _rope_tables(L)
    zpad = jnp.zeros((L, 128 - MLA_ROPE), F32)
    cos_t = jnp.concatenate([cos, cos, zpad], axis=1)
    sin_t = jnp.concatenate([-sin, sin, zpad], axis=1)
    wq = w_uq.reshape(MLA_Q_RANK, H, MLA_NOPE + MLA_ROPE)
    wqn = wq[:, :, :MLA_NOPE].reshape(MLA_Q_RANK, H * MLA_NOPE).T.astype(BF16)
    wpe = wq[:, :, MLA_NOPE:]
    zw = jnp.zeros((MLA_Q_RANK, H, 128 - MLA_ROPE), w_uq.dtype)
    wqp = jnp.concatenate([wpe, zw], axis=2).reshape(MLA_Q_RANK, H * 128).T.astype(BF16)
    wrot = jnp.concatenate([wpe[:, :, half:], wpe[:, :, :half], zw], axis=2)
    wqr = wrot.reshape(MLA_Q_RANK, H * 128).T.astype(BF16)
    wkv = w_ukv.reshape(MLA_KV_RANK, H, MLA_NOPE + MLA_V)
    wk = wkv[:, :, :MLA_NOPE].reshape(MLA_KV_RANK, H * MLA_NOPE).astype(BF16)
    wv = wkv[:, :, MLA_NOPE:].reshape(MLA_KV_RANK, H * MLA_V).T.astype(BF16)
    row = lambda w, c=0: pl.BlockSpec((R, w), lambda i: (i, c))
    col = lambda w: pl.BlockSpec((w, R), lambda i: (0, i))
    full = lambda a, b: pl.BlockSpec((a, b), lambda i: (0, 0))
    qt, k, vt = pl.pallas_call(
        _mla_prep_t_kernel,
        grid=(L // R,),
        in_specs=[row(MLA_Q_RANK, cols[0]), row(MLA_KV_RANK, cols[1]), row(128, cols[2]), row(128, cols[3]),
                  row(128), row(128), col(128), col(128),
                  full(1, MLA_Q_RANK), full(1, MLA_KV_RANK),
                  full(H * 128, MLA_Q_RANK), full(H * 128, MLA_Q_RANK), full(H * 128, MLA_Q_RANK),
                  full(MLA_KV_RANK, H * 128), full(H * 128, MLA_KV_RANK)],
        out_specs=[pl.BlockSpec((H, 256, R), lambda i: (0, 0, i)),
                   pl.BlockSpec((H, R, 256), lambda i: (0, i, 0)),
                   pl.BlockSpec((H, MLA_VT_ROWS, R), lambda i: (0, 0, i))],
        out_shape=[jax.ShapeDtypeStruct((H, 256, L), BF16),
                   jax.ShapeDtypeStruct((H, L, 256), BF16),
                   jax.ShapeDtypeStruct((H, MLA_VT_ROWS, L), BF16)],
        compiler_params=_cp(("parallel",)),
        name="mla_prep",
    )(c_q, c_kv, kr_pad, krr_pad, cos_t, sin_t, cos_t.T, sin_t.T,
      q_gain.astype(F32).reshape(1, -1), kv_gain.astype(F32).reshape(1, -1), wqn, wqp, wqr, wk, wv)
    return pl.pallas_call(
        functools.partial(_mla_attn_t_kernel, tk=tk, nk=L // tk, tsub=tsub),
        grid=(H, L // tq),
        in_specs=[pl.BlockSpec((1, 256, tq), lambda h, i: (h, 0, i)),
                  pl.BlockSpec((1, L, 256), lambda h, i: (h, 0, 0)),
                  pl.BlockSpec((1, MLA_VT_ROWS, L), lambda h, i: (h, 0, 0))],
        out_specs=pl.BlockSpec((tq, MLA_V), lambda h, i: (i, h)),
        out_shape=jax.ShapeDtypeStruct((L, H * MLA_V), BF16),
        scratch_shapes=[pltpu.VMEM((2, tq // tsub, tk, tsub), F32)],
        compiler_params=_cp(("parallel", "parallel")),
        name="mla_attn",
    )(qt, k, vt)


def _merge_kernel(h_ref, ya_ref, yb_ref, yc_ref, yd_ref, g0_ref, g1_ref, g2_ref, g3_ref, p_ref, o_ref):
    h = h_ref[...]
    acc = None
    for b, (y_ref, g_ref) in enumerate(((ya_ref, g0_ref), (yb_ref, g1_ref), (yc_ref, g2_ref), (yd_ref, g3_ref))):
        gate = _sigmoid(_dot(h, g_ref[...]))
        t = gate * _dot(y_ref[...], p_ref[b])
        acc = t if acc is None else acc + t
    o_ref[...] = acc.astype(o_ref.dtype)


def merge_branches(h, ys, w_gate, w_branch, tm=1024, tn=512):
    L = h.shape[0]
    nj = D_MODEL // tn
    gspec = lambda b: pl.BlockSpec((D_MODEL, tn), lambda i, j: (0, b * nj + j))
    yspec = pl.BlockSpec((tm, BW), lambda i, j: (i, 0))
    return pl.pallas_call(
        _merge_kernel,
        grid=(L // tm, nj),
        in_specs=[pl.BlockSpec((tm, D_MODEL), lambda i, j: (i, 0)), yspec, yspec, yspec, yspec,
                  gspec(0), gspec(1), gspec(2), gspec(3),
                  pl.BlockSpec((4, BW, tn), lambda i, j: (0, 0, j))],
        out_specs=pl.BlockSpec((tm, tn), lambda i, j: (i, j)),
        out_shape=jax.ShapeDtypeStruct((L, D_MODEL), BF16),
        compiler_params=_cp(("parallel", "arbitrary")),
        name="merge",
    )(h, *ys, w_gate, w_gate, w_gate, w_gate, w_branch)


def _out_proj_kernel(x_ref, m_ref, w_ref, g_ref, o_ref):
    f = _dot(m_ref[...], w_ref[...])
    f = f * lax.rsqrt(jnp.mean(f * f, axis=-1, keepdims=True) + NORM_EPS) * g_ref[...]
    o_ref[...] = x_ref[...] + f


def out_proj(x, m, w_out, gain, tm=512):
    L = x.shape[0]
    return pl.pallas_call(
        _out_proj_kernel,
        grid=(L // tm,),
        in_specs=[pl.BlockSpec((tm, D_MODEL), lambda i: (i, 0)),
                  pl.BlockSpec((tm, D_MODEL), lambda i: (i, 0)),
                  pl.BlockSpec((D_MODEL, D_MODEL), lambda i: (0, 0)),
                  pl.BlockSpec((1, D_MODEL), lambda i: (0, 0))],
        out_specs=pl.BlockSpec((tm, D_MODEL), lambda i: (i, 0)),
        out_shape=jax.ShapeDtypeStruct((L, D_MODEL), F32),
        compiler_params=_cp(("parallel",)),
        name="out_proj",
    )(x, m, w_out, gain.reshape(1, D_MODEL))


def _mlp_kernel(x_ref, g1_ref, w1_ref, w2_ref, g2_ref, o_ref, h_scr):
    j = pl.program_id(1)

    @pl.when(j == 0)
    def _():
        x = x_ref[...]
        ms = jnp.mean(x * x, axis=-1, keepdims=True)
        h_scr[...] = (x * lax.rsqrt(ms + NORM_EPS) * g1_ref[...]).astype(BF16)
        o_ref[...] = jnp.zeros_like(o_ref)

    a = _dot(h_scr[...], w1_ref[...])
    a = jnp.maximum(a, 0.0)
    a = (a * a).astype(BF16)
    o_ref[...] += _dot(a, w2_ref[...])

    @pl.when(j == pl.num_programs(1) - 1)
    def _():
        f = o_ref[...]
        f = f * lax.rsqrt(jnp.mean(f * f, axis=-1, keepdims=True) + NORM_EPS) * g2_ref[...]
        o_ref[...] = x_ref[...] + f


def mlp(x, g1, w1, w2, g2, tm=512, tf=1024):
    L = x.shape[0]
    return pl.pallas_call(
        _mlp_kernel,
        grid=(L // tm, D_FF // tf),
        in_specs=[pl.BlockSpec((tm, D_MODEL), lambda i, j: (i, 0)),
                  pl.BlockSpec((1, D_MODEL), lambda i, j: (0, 0)),
                  pl.BlockSpec((D_MODEL, tf), lambda i, j: (0, j)),
                  pl.BlockSpec((tf, D_MODEL), lambda i, j: (j, 0)),
                  pl.BlockSpec((1, D_MODEL), lambda i, j: (0, 0))],
        out_specs=pl.BlockSpec((tm, D_MODEL), lambda i, j: (i, 0)),
        out_shape=jax.ShapeDtypeStruct((L, D_MODEL), F32),
        scratch_shapes=[pltpu.VMEM((tm, D_MODEL), BF16)],
        compiler_params=_cp(("parallel", "arbitrary")),
        name="mlp",
    )(x, g1.reshape(1, D_MODEL), w1, w2, g2.reshape(1, D_MODEL))


def _mix_weight(w):
    half = MLA_ROPE // 2
    z = lambda n: jnp.zeros((D_MODEL, n), w.dtype)
    kr = w[:, _OFF_KR:_OFF_KR + MLA_ROPE]
    cols = [
        w[:, _OFF_QKV:_OFF_QKV + 3 * BW],
        w[:, _OFF_U:_OFF_U + BW],
        w[:, _OFF_Z:_OFF_Z + BW],
        w[:, _OFF_SWAQ:_OFF_SWAQ + BW],
        w[:, _OFF_CKV:_OFF_CKV + MLA_KV_RANK],
        w[:, _OFF_SWAKV:_OFF_SWAKV + 256],
        w[:, _OFF_CQ:_OFF_CQ + MLA_Q_RANK],
        kr, z(64),
        kr[:, half:], kr[:, :half], z(64),
        w[:, _OFF_BETA:_OFF_BETA + 16], z(112),
    ]
    return jnp.concatenate(cols, axis=1).astype(BF16)


def kernel(x, w_in, s5_lam_re, s5_lam_im, s5_log_step, s5_b_re, s5_b_im, s5_c_re, s5_c_im, s5_d, s5_w_glu, s5_b_glu, gdn_conv, gdn_a_log, gdn_dt_bias, gdn_o_gain, swa_sink, t5_bias, mla_q_gain, mla_kv_gain, mla_w_uq, mla_w_ukv, w_branch, w_out, mix_pre_gain, mix_post_gain, mlp_pre_gain, mlp_post_gain, w_mlp_in, w_mlp_out):
    B_, L, _ = x.shape
    outs = []
    for b in range(B_):
        xb = x[b].astype(F32)
        for l in range(DEPTH):
            w_mix = _mix_weight(w_in[l])
            w_gate = w_in[l][:, _OFF_GATE:].astype(BF16)
            proj, h = in_proj(xb, mix_pre_gain[l].astype(F32), w_mix, tm=1024)
            y_a = s5_mixer(proj, s5_lam_re[l], s5_lam_im[l], s5_log_step[l],
                           s5_b_re[l], s5_b_im[l], s5_c_re[l], s5_c_im[l], s5_d[l], s5_w_glu[l],
                           s5_b_glu[l], ucol=MIX_U // BW)
            y_b = gdn_mixer_fused(proj, proj, proj[:, MIX_SMALL:MIX_SMALL + 16], gdn_conv[l], gdn_a_log[l],
                                  gdn_dt_bias[l], gdn_o_gain[l], zcol=MIX_Z // BW)
            y_c = swa_mixer_t(proj, proj, swa_sink[l], t5_bias, qcol=MIX_SWAQ // BW, kvcol=MIX_SWAKV // 256)
            y_d = mla_mixer_t(proj, proj, proj, proj, mla_q_gain[l], mla_kv_gain[l], mla_w_uq[l], mla_w_ukv[l],
                              cols=(MIX_CQ // MLA_Q_RANK, MIX_CKV // MLA_KV_RANK, MIX_KR // 128,
                                    MIX_KRROT // 128))
            merged = merge_branches(h, (y_a, y_b, y_c, y_d), w_gate, w_branch[l].astype(BF16))
            xb = out_proj(xb, merged, w_out[l].astype(BF16), mix_post_gain[l].astype(F32))
            xb = mlp(xb, mlp_pre_gain[l].astype(F32), w_mlp_in[l].astype(BF16),
                     w_mlp_out[l].astype(BF16), mlp_post_gain[l].astype(F32))
        outs.append(xb)
    return jnp.stack(outs).astype(x.dtype)
```

```python
import functools
import math

import numpy as np
import jax
import jax.numpy as jnp
from jax import lax
from jax.experimental import pallas as pl
from jax.experimental.pallas import tpu as pltpu

F32 = jnp.float32
BF16 = jnp.bfloat16

D_MODEL = 2048
DEPTH = 4
BW = 512
D_FF = 4 * D_MODEL
NORM_EPS = 1e-6

S5_GROUP = 16
S5_GROUPS = 32
S5_STATE = 64
S5_NSTATE = S5_GROUPS * S5_STATE
S5_HALF = S5_NSTATE // 2
S5_LANE_SPLIT = 4

GDN_HEADS = 4
GDN_DH = 128
GDN_CHUNK = 64

SWA_HEADS = 8
SWA_KV_HEADS = 2
SWA_DH = 64
SWA_BLOCK = 128
WINDOW = 128
T5_BUCKETS = 32
T5_MAX_DISTANCE = 128

MLA_HEADS = 4
MLA_Q_RANK = 384
MLA_KV_RANK = 512
MLA_NOPE = 128
MLA_ROPE = 64
MLA_V = 128
ROPE_THETA = 10000.0

_OFF_U = 0
_OFF_QKV = 512
_OFF_Z = 2048
_OFF_BETA = 2560
_OFF_DECAY = 2568
_OFF_SWAQ = 2576
_OFF_SWAKV = 3088
_OFF_CQ = 3344
_OFF_CKV = 3728
_OFF_KR = 4240
_OFF_GATE = 4304
D_IN = _OFF_GATE + 4 * D_MODEL

MIX_QKV = 0
MIX_U = 1536
MIX_Z = 2048
MIX_SWAQ = 2560
MIX_CKV = 3072
MIX_SWAKV = 3584
MIX_CQ = 3840
MIX_KR = 4224
MIX_KRROT = 4352
MIX_SMALL = 4480
N_MIX = 4608

VMEM_LIMIT = 56 * 1024 * 1024


def _cp(sem, limit=VMEM_LIMIT):
    return pltpu.CompilerParams(dimension_semantics=sem, vmem_limit_bytes=limit)


def _dot(a, b):
    return jnp.dot(a, b, preferred_element_type=F32)


def _dot_nt(a, b):
    return lax.dot_general(a, b, (((1,), (1,)), ((), ())), preferred_element_type=F32)


def _sigmoid(x):
    return 1.0 / (1.0 + jnp.exp(-x))


def _softplus(x):
    return jnp.maximum(x, 0.0) + jnp.log(1.0 + jnp.exp(-jnp.abs(x)))


def _in_proj_kernel(x_ref, g_ref, w_ref, o_ref, h_ref, h_scr):
    @pl.when(pl.program_id(1) == 0)
    def _():
        x = x_ref[...]
        ms = jnp.mean(x * x, axis=-1, keepdims=True)
        h = (x * lax.rsqrt(ms + NORM_EPS) * g_ref[...]).astype(BF16)
        h_scr[...] = h
        h_ref[...] = h

    o_ref[...] = _dot(h_scr[...], w_ref[...])


def in_proj(x, gain, w_mix, tm=512, tn=512):
    L = x.shape[0]
    n = w_mix.shape[1]
    return pl.pallas_call(
        _in_proj_kernel,
        grid=(L // tm, n // tn),
        in_specs=[
            pl.BlockSpec((tm, D_MODEL), lambda i, j: (i, 0)),
            pl.BlockSpec((1, D_MODEL), lambda i, j: (0, 0)),
            pl.BlockSpec((D_MODEL, tn), lambda i, j: (0, j)),
        ],
        out_specs=[
            pl.BlockSpec((tm, tn), lambda i, j: (i, j)),
            pl.BlockSpec((tm, D_MODEL), lambda i, j: (i, 0)),
        ],
        out_shape=[
            jax.ShapeDtypeStruct((L, n), F32),
            jax.ShapeDtypeStruct((L, D_MODEL), BF16),
        ],
        scratch_shapes=[pltpu.VMEM((tm, D_MODEL), BF16)],
        compiler_params=_cp(("parallel", "arbitrary")),
        name="in_proj",
    )(x, gain.reshape(1, D_MODEL), w_mix)


def _s5_kernel(*refs, reverse, finalize, T):
    if finalize:
        (u_ref, bd_ref, cd_ref, hs_ref, p_ref, yb_ref, dsk_ref, wglu_ref, bglu_ref,
         o_ref, bur, bui, carry) = refs
    else:
        u_ref, bd_ref, cd_ref, hs_ref, p_ref, o_ref, bur, bui, carry = refs
    nt = T // 8

    @pl.when(pl.program_id(0) == 0)
    def _():
        carry[...] = jnp.zeros_like(carry)

    u = u_ref[...]
    ub = u.astype(BF16)
    for k in range(2):
        r = _dot(ub[:, 256 * k:256 * (k + 1)], bd_ref[k])
        bur[:, S5_HALF * k:S5_HALF * (k + 1)] = r[:, :S5_HALF]
        bui[:, S5_HALF * k:S5_HALF * (k + 1)] = r[:, S5_HALF:]

    W = S5_NSTATE // S5_LANE_SPLIT
    slabs = [slice(q * W, (q + 1) * W) for q in range(S5_LANE_SPLIT)]

    def tile(kk, c):
        idx = (nt - 1 - kk) if reverse else kk
        t0 = pl.multiple_of(idx * 8, 8)
        row = 0 if reverse else 7
        out = []
        for q, sl in enumerate(slabs):
            cr, ci = c[2 * q], c[2 * q + 1]
            xr = bur[pl.ds(t0, 8), sl]
            xi = bui[pl.ds(t0, 8), sl]
            for s, d in enumerate((1, 2, 4)):
                sh = (8 - d) if reverse else d
                sr = pltpu.roll(xr, sh, 0)
                si = pltpu.roll(xi, sh, 0)
                ar = hs_ref[2 * s, :, sl]
                ai = hs_ref[2 * s + 1, :, sl]
                xr, xi = xr + ar * sr - ai * si, xi + ar * si + ai * sr
            pr = p_ref[0, :, sl]
            pi = p_ref[1, :, sl]
            xr, xi = xr + pr * cr - pi * ci, xi + pr * ci + pi * cr
            bur[pl.ds(t0, 8), sl] = xr
            bui[pl.ds(t0, 8), sl] = xi
            out += [jnp.broadcast_to(xr[row:row + 1, :], (8, W)),
                    jnp.broadcast_to(xi[row:row + 1, :], (8, W))]
        return tuple(out)

    c0 = []
    for sl in slabs:
        c0 += [carry[0, :, sl], carry[1, :, sl]]
    cf = lax.fori_loop(0, nt, tile, tuple(c0))
    for q, sl in enumerate(slabs):
        carry[0, :, sl] = cf[2 * q]
        carry[1, :, sl] = cf[2 * q + 1]

    ys = []
    for k in range(2):
        sr = bur[:, S5_HALF * k:S5_HALF * (k + 1)].astype(BF16)
        si = bui[:, S5_HALF * k:S5_HALF * (k + 1)].astype(BF16)
        ys.append(_dot(sr, cd_ref[k, 0]) + _dot(si, cd_ref[k, 1]))
    y = jnp.concatenate(ys, axis=1)
    if finalize:
        y = y + yb_ref[...] + dsk_ref[...] * u
        y = jax.nn.gelu(y)
        gate = _sigmoid(_dot(y.astype(BF16), wglu_ref[...]) + bglu_ref[...])
        o_ref[...] = (y * gate).astype(o_ref.dtype)
    else:
        o_ref[...] = y


def s5_direction_call(u, bd, cd, hs, pw, *, reverse, fin=None, T=256, ucol=0):
    L = u.shape[0]
    nb = L // T
    rmap = (lambda i: (nb - 1 - i, 0)) if reverse else (lambda i: (i, 0))
    umap = (lambda i: (nb - 1 - i, ucol)) if reverse else (lambda i: (i, ucol))
    c3 = lambda i: (0, 0, 0)
    in_specs = [
        pl.BlockSpec((T, BW), umap),
        pl.BlockSpec((2, 256, S5_NSTATE), c3),
        pl.BlockSpec((2, 2, S5_HALF, 256), lambda i: (0, 0, 0, 0)),
        pl.BlockSpec((6, 8, S5_NSTATE), c3),
        pl.BlockSpec((2, 8, S5_NSTATE), c3),
    ]
    args = [u, bd, cd, hs, pw]
    if fin is not None:
        yb, dsk, wglu, bglu = fin
        in_specs += [
            pl.BlockSpec((T, BW), rmap),
            pl.BlockSpec((1, BW), lambda i: (0, 0)),
            pl.BlockSpec((BW, BW), lambda i: (0, 0)),
            pl.BlockSpec((1, BW), lambda i: (0, 0)),
        ]
        args += [yb, dsk.reshape(1, BW), wglu, bglu.reshape(1, BW)]
    return pl.pallas_call(
        functools.partial(_s5_kernel, reverse=reverse, finalize=fin is not None, T=T),
        grid=(nb,),
        in_specs=in_specs,
        out_specs=pl.BlockSpec((T, BW), rmap),
        out_shape=jax.ShapeDtypeStruct((L, BW), BF16 if fin is not None else F32),
        scratch_shapes=[
            pltpu.VMEM((T, S5_NSTATE), F32),
            pltpu.VMEM((T, S5_NSTATE), F32),
            pltpu.VMEM((2, 8, S5_NSTATE), F32),
        ],
        compiler_params=_cp(("arbitrary",)),
        name="s5_rev" if reverse else "s5_fwd",
    )(*args)


def _s5_prep(lam_re, lam_im, log_step, b_re, b_im, c_re, c_im, reverse):
    G, P, Hg = S5_GROUPS, S5_STATE, S5_GROUP
    lam_re = jnp.minimum(lam_re.astype(F32), -1e-4)
    lam_im = lam_im.astype(F32)
    dt = jnp.exp(log_step.astype(F32))[:, None]
    mag = jnp.exp(lam_re * dt)
    abar_r = mag * jnp.cos(lam_im * dt)
    abar_i = mag * jnp.sin(lam_im * dt)
    den = lam_re * lam_re + lam_im * lam_im
    xr = abar_r - 1.0
    xi = abar_i
    coef_r = (xr * lam_re + xi * lam_im) / den
    coef_i = (xi * lam_re - xr * lam_im) / den
    b_re = b_re.astype(F32)
    b_im = b_im.astype(F32)
    bbar_r = coef_r[..., None] * b_re - coef_i[..., None] * b_im
    bbar_i = coef_r[..., None] * b_im + coef_i[..., None] * b_re

    eye = jnp.eye(16, dtype=F32)

    def bdiag_in(bb):
        blk = bb.transpose(0, 2, 1).reshape(2, 16, Hg, P)
        return jnp.einsum('gq,kghp->kghqp', eye, blk).reshape(2, 16 * Hg, 16 * P)

    bd = jnp.concatenate([bdiag_in(bbar_r), bdiag_in(bbar_i)], axis=-1).astype(BF16)

    def bdiag_out(cc):
        blk = cc.astype(F32).transpose(0, 2, 1).reshape(2, 16, P, Hg)
        return jnp.einsum('gq,kgph->kgpqh', eye, blk).reshape(2, 16 * P, 16 * Hg)

    cd = jnp.stack([bdiag_out(c_re), -bdiag_out(c_im)], axis=1).astype(BF16)

    ar = abar_r.reshape(-1)
    ai = abar_i.reshape(-1)

    def cm(a, b):
        return a[0] * b[0] - a[1] * b[1], a[0] * b[1] + a[1] * b[0]

    a1 = (ar, ai)
    a2 = cm(a1, a1)
    a3 = cm(a2, a1)
    a4 = cm(a2, a2)
    a5 = cm(a4, a1)
    a6 = cm(a4, a2)
    a7 = cm(a4, a3)
    a8 = cm(a4, a4)
    pows = [a1, a2, a3, a4, a5, a6, a7, a8]
    rows = np.arange(8)[:, None]
    hs = []
    for d, ad in ((1, a1), (2, a2), (4, a4)):
        mask = jnp.asarray((rows <= 7 - d) if reverse else (rows >= d), F32)
        hs.append(mask * ad[0][None, :])
        hs.append(mask * ad[1][None, :])
    hs = jnp.stack(hs)
    order = list(range(7, -1, -1)) if reverse else list(range(8))
    pw = jnp.stack([jnp.stack([pows[t][0] for t in order]),
                    jnp.stack([pows[t][1] for t in order])])
    return bd, cd, hs, pw


def s5_mixer(u, lam_re, lam_im, log_step, b_re, b_im, c_re, c_im, d_skip, w_glu, b_glu, T=256, ucol=0):
    pf = _s5_prep(lam_re[0], lam_im[0], log_step[0], b_re[0], b_im[0], c_re[0], c_im[0], False)
    pb = _s5_prep(lam_re[1], lam_im[1], log_step[1], b_re[1], b_im[1], c_re[1], c_im[1], True)
    yb = s5_direction_call(u, *pb, reverse=True, T=T, ucol=ucol)
    return s5_direction_call(u, *pf, reverse=False, T=T, ucol=ucol,
                             fin=(yb, d_skip.astype(F32), w_glu.astype(BF16), b_glu.astype(F32)))


GDN_HC = GDN_HEADS * GDN_CHUNK
GDN_UNROLL = 2


def _stack_heads(ref, r0):
    return jnp.concatenate(
        [ref[pl.ds(r0, GDN_CHUNK), h * GDN_DH:(h + 1) * GDN_DH] for h in range(GDN_HEADS)], axis=0)


def _gdn_pre_kernel(qkv_ref, prev_ref, next_ref, cw_ref, q_ref, k_ref, v_ref, xp_scr, *, R, nb):
    blk = pl.program_id(0)
    xp_scr[0:8] = jnp.where(blk > 0, prev_ref[...], 0.0)
    xp_scr[8:8 + R] = qkv_ref[...]
    xp_scr[8 + R:16 + R] = jnp.where(blk < nb - 1, next_ref[...], 0.0)
    w = cw_ref[...]
    conv = (w[0:1] * xp_scr[6:6 + R] + w[1:2] * xp_scr[7:7 + R]
            + w[2:3] * xp_scr[8:8 + R] + w[3:4] * xp_scr[9:9 + R])
    act = conv * _sigmoid(conv)
    for h in range(GDN_HEADS):
        sl = slice(h * GDN_DH, (h + 1) * GDN_DH)
        qh = act[:, sl]
        q_ref[:, sl] = qh * lax.rsqrt(jnp.sum(qh * qh, axis=-1, keepdims=True) + 1e-6) * (GDN_DH ** -0.5)
        kh = act[:, BW + h * GDN_DH:BW + (h + 1) * GDN_DH]
        k_ref[:, sl] = kh * lax.rsqrt(jnp.sum(kh * kh, axis=-1, keepdims=True) + 1e-6)
    v_ref[...] = act[:, 2 * BW:3 * BW]


def _gdn_prologue(d, R, q_ref, k_ref, v_ref, sm_ref, gt_ref, gpr_ref, gpc_ref,
                  q_scr, k_scr, v_scr, b_scr, gc_scr, gr_scr, tri):
    q_scr[d] = q_ref[...]
    k_scr[d] = k_ref[...]
    v_scr[d] = v_ref[...]
    sm = sm_ref[...]
    b_scr[d] = _sigmoid(sm[:, 0:4])
    g = gpr_ref[2 * d:2 * d + 1, :] * _softplus(sm[:, 4:8] + gpr_ref[2 * d + 1:2 * d + 2, :])
    gt = gpc_ref[:, 2 * d:2 * d + 1] * _softplus(gt_ref[...] + gpc_ref[:, 2 * d + 1:2 * d + 2])
    tri_l, tri_u = tri
    gc_scr[d] = jnp.dot(tri_l, g, preferred_element_type=F32, precision=lax.Precision.HIGHEST)
    gr = jnp.dot(gt, tri_u, preferred_element_type=F32, precision=lax.Precision.HIGHEST)
    for c in range(R // GDN_CHUNK):
        gr_scr[d, c] = gr[:, c * GDN_CHUNK:(c + 1) * GDN_CHUNK]


def _gdn_chunks(insts, m_scr, q_scr, k_scr, v_scr, b_scr, gc_scr, gr_scr, s_scr):
    C = GDN_CHUNK
    H = GDN_HEADS
    n = len(insts)
    rng = range(n)
    ds = [t[0] for t in insts]
    r0s = [pl.multiple_of(t[1] * C, C) for t in insts]
    lasts = [0 if t[2] else C - 1 for t in insts]
    ks = [_stack_heads(k_scr.at[ds[i]], r0s[i]) for i in rng]
    qs = [_stack_heads(q_scr.at[ds[i]], r0s[i]) for i in rng]
    vs = [_stack_heads(v_scr.at[ds[i]], r0s[i]) for i in rng]
    bch = [b_scr[ds[i], pl.ds(r0s[i], C), :] for i in rng]
    gch = [gc_scr[ds[i], pl.ds(r0s[i], C), :] for i in rng]
    grw = [gr_scr[ds[i], insts[i][1]] for i in rng]
    beta = [jnp.concatenate([b[:, h:h + 1] for h in range(H)], axis=0) for b in bch]
    gcol = [jnp.concatenate([g[:, h:h + 1] for h in range(H)], axis=0) for g in gch]
    grow = [jnp.concatenate([g[h:h + 1, :] for h in range(H)], axis=1) for g in grw]
    glast_r = [jnp.concatenate(
        [jnp.broadcast_to(grw[i][h:h + 1, lasts[i]:lasts[i] + 1], (1, C)) for h in range(H)], axis=1)
        for i in rng]
    e_cat = [jnp.concatenate(
        [jnp.broadcast_to(jnp.exp(gch[i][lasts[i]:lasts[i] + 1, h:h + 1]), (1, GDN_DH)) for h in range(H)],
        axis=1) for i in rng]

    kst = [k.T for k in ks]
    kstb = [k.astype(BF16) for k in kst]
    kbs = [ks[i] * beta[i] for i in rng]
    g2 = [_dot(jnp.concatenate([kbs[i], qs[i]], axis=0).astype(BF16), kstb[i]) for i in rng]
    decay = [jnp.exp((gcol[i] - grow[i]) + m_scr[ds[i], 0]) for i in rng]
    a = [g2[i][:GDN_HC] * decay[i] * m_scr[ds[i], 1] for i in rng]
    attn = [(g2[i][GDN_HC:] * decay[i]).astype(BF16) for i in rng]
    minv = [m_scr[ds[i], 8] - a[i] * m_scr[ds[i], 2] for i in rng]
    for lvl in range(1, 6):
        mb = [m.astype(BF16) for m in minv]
        em = [_dot((a[i] * m_scr[ds[i], 2 + lvl]).astype(BF16), mb[i]).astype(BF16) for i in rng]
        minv = [minv[i] - _dot(mb[i], em[i]) for i in rng]
    egc = [jnp.exp(g) for g in gcol]
    rhs = [jnp.concatenate([vs[i] * beta[i], kbs[i] * egc[i]], axis=1).astype(BF16) for i in rng]
    xs = [_dot(minv[i].astype(BF16), rhs[i]) for i in rng]
    qg = [qs[i] * egc[i] for i in rng]
    wq = [jnp.concatenate([xs[i][:, GDN_DH:], qg[i]], axis=0).astype(BF16) for i in rng]
    kdect = [(kst[i] * jnp.exp(glast_r[i] - grow[i])).astype(BF16) for i in rng]

    zero = jnp.zeros((C, GDN_DH), BF16)
    by_dir = {}
    for i in rng:
        by_dir.setdefault(ds[i], []).append(i)
    depth = max(len(v) for v in by_dir.values())
    s_cat = {d: s_scr[d] for d in by_dir}
    for step in range(depth):
        act = [v[step] for v in by_dir.values() if len(v) > step]
        p = [_dot(wq[i], s_cat[ds[i]].astype(BF16)) for i in act]
        vn = []
        qss = []
        for j, i in enumerate(act):
            ws = jnp.concatenate(
                [p[j][h * C:(h + 1) * C, h * GDN_DH:(h + 1) * GDN_DH] for h in range(H)], axis=0)
            qss.append(jnp.concatenate(
                [p[j][GDN_HC + h * C:GDN_HC + (h + 1) * C, h * GDN_DH:(h + 1) * GDN_DH] for h in range(H)],
                axis=0))
            vn.append((xs[i][:, :GDN_DH] - ws).astype(BF16))
        o = [qss[j] + _dot(attn[i], vn[j]) for j, i in enumerate(act)]
        for j, i in enumerate(act):
            vn_bd = jnp.concatenate(
                [jnp.concatenate([vn[j][h * C:(h + 1) * C] if hh == h else zero for hh in range(H)], axis=1)
                 for h in range(H)], axis=0)
            s_cat[ds[i]] = s_cat[ds[i]] * e_cat[i] + _dot(kdect[i], vn_bd)
            o_ref = insts[i][3]
            for h in range(H):
                o_ref[pl.ds(r0s[i], C), h * GDN_DH:(h + 1) * GDN_DH] = o[j][h * C:(h + 1) * C]
    for d in by_dir:
        s_scr[d] = s_cat[d]


def _gdn_bidir_kernel(qf_ref, kf_ref, vf_ref, qb_ref, kb_ref, vb_ref, smf_ref, smb_ref,
                      gtf_ref, gtb_ref, gpr_ref, gpc_ref, of_ref, ob_ref,
                      s_scr, q_scr, k_scr, v_scr, b_scr, gc_scr, gr_scr, m_scr, *, R, nb):
    C = GDN_CHUNK
    nc = R // C
    i = pl.program_id(0)

    @pl.when(i == 0)
    def _():
        s_scr[...] = jnp.zeros_like(s_scr)
        ii = lax.broadcasted_iota(jnp.int32, (GDN_HC, GDN_HC), 0)
        jj = lax.broadcasted_iota(jnp.int32, (GDN_HC, GDN_HC), 1)
        blockd = (ii // C) == (jj // C)
        for d, reverse in enumerate((False, True)):
            incl = blockd & ((jj >= ii) if reverse else (jj <= ii))
            strict = blockd & ((jj > ii) if reverse else (jj < ii))
            m_scr[d, 0] = jnp.where(incl, 0.0, -jnp.inf).astype(F32)
            m_scr[d, 1] = jnp.where(strict, 1.0, 0.0).astype(F32)
            for lvl in range(6):
                bi = ii >> lvl
                bj = jj >> lvl
                lo_, hi_ = (bi, bj) if reverse else (bj, bi)
                pm = ((ii >> (lvl + 1)) == (jj >> (lvl + 1))) & ((hi_ & 1) == 1) & ((lo_ & 1) == 0)
                m_scr[d, 2 + lvl] = jnp.where(pm, 1.0, 0.0).astype(F32)
            m_scr[d, 8] = jnp.where(ii == jj, 1.0, 0.0).astype(F32)

    ri = lax.broadcasted_iota(jnp.int32, (R, R), 0)
    rj = lax.broadcasted_iota(jnp.int32, (R, R), 1)
    same = (ri // C) == (rj // C)
    tris = []
    for reverse in (False, True):
        tl = jnp.where(same & ((rj >= ri) if reverse else (rj <= ri)), 1.0, 0.0).astype(F32)
        tu = jnp.where(same & ((ri >= rj) if reverse else (ri <= rj)), 1.0, 0.0).astype(F32)
        tris.append((tl, tu))
    _gdn_prologue(0, R, qf_ref, kf_ref, vf_ref, smf_ref, gtf_ref, gpr_ref, gpc_ref,
                  q_scr, k_scr, v_scr, b_scr, gc_scr, gr_scr, tris[0])
    _gdn_prologue(1, R, qb_ref, kb_ref, vb_ref, smb_ref, gtb_ref, gpr_ref, gpc_ref,
                  q_scr, k_scr, v_scr, b_scr, gc_scr, gr_scr, tris[1])

    def chunk(cc, carry):
        insts = []
        for uu in range(GDN_UNROLL):
            c = cc * GDN_UNROLL + uu
            insts.append((0, c, False, of_ref))
            insts.append((1, nc - 1 - c, True, ob_ref))
        _gdn_chunks(insts, m_scr, q_scr, k_scr, v_scr, b_scr, gc_scr, gr_scr, s_scr)
        return carry

    lax.fori_loop(0, nc // GDN_UNROLL, chunk, 0)


def _gdn_final_kernel(of_ref, ob_ref, z_ref, og_ref, o_ref):
    o = of_ref[...] + ob_ref[...]
    z = z_ref[...]
    og = og_ref[...]
    for h in range(GDN_HEADS):
        sl = slice(h * GDN_DH, (h + 1) * GDN_DH)
        oh = o[:, sl]
        oh = oh * lax.rsqrt(jnp.mean(oh * oh, axis=-1, keepdims=True) + NORM_EPS) * og
        zh = z[:, sl]
        o_ref[:, sl] = (oh * (zh * _sigmoid(zh))).astype(o_ref.dtype)


def gdn_mixer_fused(qkv, z, small, conv_w, a_log, dt_bias, o_gain, R=256, RF=512, zcol=0):
    L = qkv.shape[0]
    nb = L // R
    r8 = R // 8
    coef = -jnp.exp(a_log.astype(F32))
    dtb = dt_bias.astype(F32)
    gpr = jnp.stack([coef[0], dtb[0], coef[1], dtb[1]])
    gpc = gpr.T
    smf = jnp.concatenate([small[:, 0:4], small[:, 8:12]], axis=1)
    smb = jnp.concatenate([small[:, 4:8], small[:, 12:16]], axis=1)
    gtf = small[:, 8:12].T
    gtb = small[:, 12:16].T
    fmap = lambda i: i
    bmap = lambda i: nb - 1 - i
    blk = lambda m: pl.BlockSpec((R, 3 * BW), lambda i: (m(i), 0))
    prv = lambda m: pl.BlockSpec((8, 3 * BW), lambda i: (jnp.maximum(m(i) * r8 - 1, 0), 0))
    nxt = lambda m: pl.BlockSpec((8, 3 * BW), lambda i: (jnp.minimum((m(i) + 1) * r8, L // 8 - 1), 0))
    full = lambda a, b: pl.BlockSpec((a, b), lambda i: (0, 0))
    rowb = lambda m: pl.BlockSpec((R, BW), lambda i: (m(i), 0))
    qn, kn, vv = pl.pallas_call(
        functools.partial(_gdn_pre_kernel, R=R, nb=nb),
        grid=(nb,),
        in_specs=[blk(fmap), prv(fmap), nxt(fmap), full(4, 3 * BW)],
        out_specs=[rowb(fmap), rowb(fmap), rowb(fmap)],
        out_shape=[jax.ShapeDtypeStruct((L, BW), F32)] * 3,
        scratch_shapes=[pltpu.VMEM((R + 16, 3 * BW), F32)],
        compiler_params=_cp(("parallel",)),
        name="gdn_pre",
    )(qkv, qkv, qkv, conv_w.astype(F32))
    o_f, o_b = pl.pallas_call(
        functools.partial(_gdn_bidir_kernel, R=R, nb=nb),
        grid=(nb,),
        in_specs=[rowb(fmap), rowb(fmap), rowb(fmap), rowb(bmap), rowb(bmap), rowb(bmap),
                  pl.BlockSpec((R, 8), lambda i: (i, 0)),
                  pl.BlockSpec((R, 8), lambda i: (nb - 1 - i, 0)),
                  pl.BlockSpec((4, R), lambda i: (0, i)),
                  pl.BlockSpec((4, R), lambda i: (0, nb - 1 - i)),
                  full(4, 4), full(4, 4)],
        out_specs=[pl.BlockSpec((R, BW), lambda i: (i, 0)),
                   pl.BlockSpec((R, BW), lambda i: (nb - 1 - i, 0))],
        out_shape=[jax.ShapeDtypeStruct((L, BW), F32), jax.ShapeDtypeStruct((L, BW), F32)],
        scratch_shapes=[
            pltpu.VMEM((2, GDN_DH, BW), F32),
            pltpu.VMEM((2, R, BW), F32),
            pltpu.VMEM((2, R, BW), F32),
            pltpu.VMEM((2, R, BW), F32),
            pltpu.VMEM((2, R, 4), F32),
            pltpu.VMEM((2, R, 4), F32),
            pltpu.VMEM((2, R // GDN_CHUNK, 4, GDN_CHUNK), F32),
            pltpu.VMEM((2, 9, GDN_HC, GDN_HC), F32),
        ],
        compiler_params=_cp(("arbitrary",)),
        name="gdn_bidir",
    )(qn, kn, vv, qn, kn, vv, smf, smb, gtf, gtb, gpr, gpc)
    row = pl.BlockSpec((RF, BW), lambda i: (i, 0))
    zrow = pl.BlockSpec((RF, BW), lambda i: (i, zcol))
    return pl.pallas_call(
        _gdn_final_kernel,
        grid=(L // RF,),
        in_specs=[row, row, zrow, pl.BlockSpec((1, GDN_DH), lambda i: (0, 0))],
        out_specs=row,
        out_shape=jax.ShapeDtypeStruct((L, BW), BF16),
        compiler_params=_cp(("parallel",)),
        name="gdn_final",
    )(o_f, o_b, z, o_gain.astype(F32).reshape(1, GDN_DH))


SWA_G = SWA_HEADS // SWA_KV_HEADS
SWA_VT_ROWS = SWA_DH + 16


def _t5_bucket(rel):
    nb = T5_BUCKETS // 2
    max_exact = nb // 2
    ret = jnp.where(rel > 0, nb, 0)
    n = jnp.abs(rel)
    nf = jnp.maximum(n, 1).astype(F32)
    large = max_exact + (jnp.log(nf / max_exact) / math.log(T5_MAX_DISTANCE / max_exact)
                         * (nb - max_exact)).astype(jnp.int32)
    large = jnp.minimum(large, nb - 1)
    return ret + jnp.where(n < max_exact, n, large)


def _swa_t_kernel(q_ref, kvp_ref, kvc_ref, kvn_ref, bias_ref, sink_ref, o_ref, *, nb):
    i = pl.program_id(0)
    B = SWA_BLOCK
    KD = SWA_KV_HEADS * SWA_DH
    sj = lax.broadcasted_iota(jnp.int32, (3 * B, SWA_G * B), 0)
    qi = lax.broadcasted_iota(jnp.int32, (3 * B, SWA_G * B), 1) % B
    rel = sj - B - qi
    lo = jnp.where(i == 0, B, 0)
    hi = jnp.where(i == nb - 1, 2 * B, 3 * B)
    valid = (jnp.abs(rel) <= WINDOW) & (sj >= lo) & (sj < hi)
    qt = (q_ref[...] * (SWA_DH ** -0.5)).T
    kb = jnp.concatenate([kvp_ref[:, 0:KD], kvc_ref[:, 0:KD], kvn_ref[:, 0:KD]], axis=0).astype(BF16)
    vt = jnp.concatenate([kvp_ref[:, KD:2 * KD].T, kvc_ref[:, KD:2 * KD].T, kvn_ref[:, KD:2 * KD].T],
                         axis=1).astype(BF16)
    zq = jnp.zeros((SWA_DH, SWA_G * B), F32)
    ones = jnp.ones((SWA_VT_ROWS - SWA_DH, 3 * B), BF16)
    outs = []
    for g in range(SWA_KV_HEADS):
        qg = jnp.concatenate([qt[(g * SWA_G + hh) * SWA_DH:(g * SWA_G + hh + 1) * SWA_DH, :]
                              for hh in range(SWA_G)], axis=1)
        qpad = jnp.concatenate([qg if gg == g else zq for gg in range(SWA_KV_HEADS)], axis=0).astype(BF16)
        s = _dot(kb, qpad) + bias_ref[g]
        s = jnp.where(valid, s, -1e30)
        sink = sink_ref[g]
        m = jnp.maximum(jnp.max(s, axis=0, keepdims=True), sink)
        p = jnp.exp(s - m).astype(BF16)
        vg = jnp.concatenate([vt[g * SWA_DH:(g + 1) * SWA_DH, :], ones], axis=0)
        pv = _dot(vg, p)
        den = pv[SWA_DH:SWA_DH + 1] + jnp.exp(sink - m)
        o = pv[0:SWA_DH] / den
        outs += [o[:, hh * B:(hh + 1) * B] for hh in range(SWA_G)]
    ot = jnp.concatenate(outs, axis=0)
    o_ref[...] = ot.T.astype(o_ref.dtype)


def swa_mixer_t(q, kv, sink, t5_bias, qcol=0, kvcol=0):
    L = q.shape[0]
    B = SWA_BLOCK
    nb = L // B
    rel = jnp.arange(3 * B)[None, :] - B - jnp.arange(B)[:, None]
    onehot = (_t5_bucket(rel)[..., None] == jnp.arange(T5_BUCKETS)).astype(F32)
    bias = jnp.einsum('qsb,bh->hqs', onehot, t5_bias.astype(F32), precision=lax.Precision.HIGHEST)
    bias_t = bias.reshape(SWA_KV_HEADS, SWA_G, B, 3 * B).transpose(0, 3, 1, 2).reshape(
        SWA_KV_HEADS, 3 * B, SWA_G * B)
    sink_r = jnp.broadcast_to(sink.astype(F32).reshape(SWA_KV_HEADS, 1, SWA_G, 1),
                              (SWA_KV_HEADS, 1, SWA_G, B)).reshape(SWA_KV_HEADS, 1, SWA_G * B)
    kvspec = lambda f: pl.BlockSpec((B, 2 * SWA_KV_HEADS * SWA_DH), lambda i: (f(i), kvcol))
    return pl.pallas_call(
        functools.partial(_swa_t_kernel, nb=nb),
        grid=(nb,),
        in_specs=[
            pl.BlockSpec((B, SWA_HEADS * SWA_DH), lambda i: (i, qcol)),
            kvspec(lambda i: jnp.maximum(i - 1, 0)), kvspec(lambda i: i),
            kvspec(lambda i: jnp.minimum(i + 1, nb - 1)),
            pl.BlockSpec((SWA_KV_HEADS, 3 * B, SWA_G * B), lambda i: (0, 0, 0)),
            pl.BlockSpec((SWA_KV_HEADS, 1, SWA_G * B), lambda i: (0, 0, 0)),
        ],
        out_specs=pl.BlockSpec((B, SWA_HEADS * SWA_DH), lambda i: (i, 0)),
        out_shape=jax.ShapeDtypeStruct((L, SWA_HEADS * SWA_DH), BF16),
        compiler_params=_cp(("parallel",)),
        name="swa",
    )(q, kv, kv, kv, bias_t, sink_r)


MLA_VT_ROWS = MLA_V + 16
LOG2E = 1.4426950408889634


def _mla_prep_t_kernel(cq_ref, ckv_ref, kr_ref, krr_ref, cos_ref, sin_ref, cost_ref, sint_ref,
                       qg_ref, kvg_ref, wqn_ref, wqp_ref, wqr_ref, wk_ref, wv_ref,
                       qt_ref, k_ref, vt_ref):
    cq = cq_ref[...]
    qn = (cq * lax.rsqrt(jnp.mean(cq * cq, axis=-1, keepdims=True) + NORM_EPS) * qg_ref[...]).astype(BF16)
    ckv = ckv_ref[...]
    kn = (ckv * lax.rsqrt(jnp.mean(ckv * ckv, axis=-1, keepdims=True) + NORM_EPS) * kvg_ref[...]).astype(BF16)
    c = (MLA_NOPE + MLA_ROPE) ** -0.5 * LOG2E
    qt_nope = _dot_nt(wqn_ref[...], qn)
    qt_pe = _dot_nt(wqp_ref[...], qn)
    qt_rot = _dot_nt(wqr_ref[...], qn)
    cost = cost_ref[...]
    sint = sint_ref[...]
    k_nope = _dot(kn, wk_ref[...])
    vt = _dot_nt(wv_ref[...], kn)
    k_pe = (kr_ref[...] * cos_ref[...] + krr_ref[...] * sin_ref[...]).astype(BF16)
    ones = jnp.ones((MLA_VT_ROWS - MLA_V, vt.shape[1]), BF16)
    for h in range(MLA_HEADS):
        sl = slice(h * 128, (h + 1) * 128)
        qt_ref[h, 0:128, :] = (qt_nope[sl] * c).astype(BF16)
        qt_ref[h, 128:256, :] = ((qt_pe[sl] * cost + qt_rot[sl] * sint) * c).astype(BF16)
        k_ref[h, :, 0:128] = k_nope[:, sl].astype(BF16)
        k_ref[h, :, 128:256] = k_pe
        vt_ref[h, 0:MLA_V, :] = vt[sl].astype(BF16)
        vt_ref[h, MLA_V:MLA_VT_ROWS, :] = ones


def _mla_attn_t_kernel(qt_ref, k_ref, vt_ref, o_ref, s_scr, *, tk, nk, tsub):
    tq = qt_ref.shape[2]
    nsub = tq // tsub
    subs = range(nsub)
    qts = [qt_ref[0, :, j * tsub:(j + 1) * tsub] for j in subs]

    def put_scores(kc, slot):
        k0 = pl.multiple_of(kc * tk, tk)
        kblk = k_ref[0, pl.ds(k0, tk), :]
        for j in subs:
            s_scr[slot, j] = _dot(kblk, qts[j])

    def consume(kc, slot, ms, accs):
        k0 = pl.multiple_of(kc * tk, tk)
        vt = vt_ref[0, :, pl.ds(k0, tk)]
        ss = [s_scr[slot, j] for j in subs]
        m_new = [jnp.maximum(ms[j], jnp.max(ss[j], axis=0, keepdims=True)) for j in subs]
        ps = [jnp.exp2(ss[j] - m_new[j]).astype(BF16) for j in subs]
        pv = [_dot(vt, ps[j]) for j in subs]
        accs = [jnp.exp2(ms[j] - m_new[j]) * accs[j] + pv[j] for j in subs]
        return m_new, accs

    def body(kk, carry):
        ms, accs = carry
        kc = 2 * kk
        put_scores(kc + 1, 1)
        ms, accs = consume(kc, 0, list(ms), list(accs))
        put_scores(jnp.minimum(kc + 2, nk - 1), 0)
        ms, accs = consume(kc + 1, 1, ms, accs)
        return tuple(ms), tuple(accs)

    m0 = tuple(jnp.full((1, tsub), -jnp.inf, F32) for _ in subs)
    a0 = tuple(jnp.zeros((MLA_VT_ROWS, tsub), F32) for _ in subs)
    put_scores(0, 0)
    ms, accs = lax.fori_loop(0, nk // 2, body, (m0, a0))
    for j in subs:
        o = accs[j][0:MLA_V] / accs[j][MLA_V:MLA_V + 1]
        o_ref[j * tsub:(j + 1) * tsub, :] = o.T.astype(o_ref.dtype)


def _rope_tables(L):
    pos = np.arange(L, dtype=np.float32)
    inv_freq = np.float32(ROPE_THETA) ** (-np.arange(0, MLA_ROPE, 2, dtype=np.float32) / np.float32(MLA_ROPE))
    ang = pos[:, None] * inv_freq.astype(np.float32)[None, :]
    return jnp.asarray(np.cos(ang), F32), jnp.asarray(np.sin(ang), F32)


def mla_mixer_t(c_q, c_kv, kr_pad, krr_pad, q_gain, kv_gain, w_uq, w_ukv, R=512, tq=1024, tk=512, tsub=256,
                cols=(0, 0, 0, 0)):
    L = c_q.shape[0]
    H = MLA_HEADS
    half = MLA_ROPE // 2
    cos, sin = _rope_tables(L)
    zpad = jnp.zeros((L, 128 - MLA_ROPE), F32)
    cos_t = jnp.concatenate([cos, cos, zpad], axis=1)
    sin_t = jnp.concatenate([-sin, sin, zpad], axis=1)
    wq = w_uq.reshape(MLA_Q_RANK, H, MLA_NOPE + MLA_ROPE)
    wqn = wq[:, :, :MLA_NOPE].reshape(MLA_Q_RANK, H * MLA_NOPE).T.astype(BF16)
    wpe = wq[:, :, MLA_NOPE:]
    zw = jnp.zeros((MLA_Q_RANK, H, 128 - MLA_ROPE), w_uq.dtype)
    wqp = jnp.concatenate([wpe, zw], axis=2).reshape(MLA_Q_RANK, H * 128).T.astype(BF16)
    wrot = jnp.concatenate([wpe[:, :, half:], wpe[:, :, :half], zw], axis=2)
    wqr = wrot.reshape(MLA_Q_RANK, H * 128).T.astype(BF16)
    wkv = w_ukv.reshape(MLA_KV_RANK, H, MLA_NOPE + MLA_V)
    wk = wkv[:, :, :MLA_NOPE].reshape(MLA_KV_RANK, H * MLA_NOPE).astype(BF16)
    wv = wkv[:, :, MLA_NOPE:].reshape(MLA_KV_RANK, H * MLA_V).T.astype(BF16)
    row = lambda w, c=0: pl.BlockSpec((R, w), lambda i: (i, c))
    col = lambda w: pl.BlockSpec((w, R), lambda i: (0, i))
    full = lambda a, b: pl.BlockSpec((a, b), lambda i: (0, 0))
    qt, k, vt = pl.pallas_call(
        _mla_prep_t_kernel,
        grid=(L // R,),
        in_specs=[row(MLA_Q_RANK, cols[0]), row(MLA_KV_RANK, cols[1]), row(128, cols[2]), row(128, cols[3]),
                  row(128), row(128), col(128), col(128),
                  full(1, MLA_Q_RANK), full(1, MLA_KV_RANK),
                  full(H * 128, MLA_Q_RANK), full(H * 128, MLA_Q_RANK), full(H * 128, MLA_Q_RANK),
                  full(MLA_KV_RANK, H * 128), full(H * 128, MLA_KV_RANK)],
        out_specs=[pl.BlockSpec((H, 256, R), lambda i: (0, 0, i)),
                   pl.BlockSpec((H, R, 256), lambda i: (0, i, 0)),
                   pl.BlockSpec((H, MLA_VT_ROWS, R), lambda i: (0, 0, i))],
        out_shape=[jax.ShapeDtypeStruct((H, 256, L), BF16),
                   jax.ShapeDtypeStruct((H, L, 256), BF16),
                   jax.ShapeDtypeStruct((H, MLA_VT_ROWS, L), BF16)],
        compiler_params=_cp(("parallel",)),
        name="mla_prep",
    )(c_q, c_kv, kr_pad, krr_pad, cos_t, sin_t, cos_t.T, sin_t.T,
      q_gain.astype(F32).reshape(1, -1), kv_gain.astype(F32).reshape(1, -1), wqn, wqp, wqr, wk, wv)
    return pl.pallas_call(
        functools.partial(_mla_attn_t_kernel, tk=tk, nk=L // tk, tsub=tsub),
        grid=(H, L // tq),
        in_specs=[pl.BlockSpec((1, 256, tq), lambda h, i: (h, 0, i)),
                  pl.BlockSpec((1, L, 256), lambda h, i: (h, 0, 0)),
                  pl.BlockSpec((1, MLA_VT_ROWS, L), lambda h, i: (h, 0, 0))],
        out_specs=pl.BlockSpec((tq, MLA_V), lambda h, i: (i, h)),
        out_shape=jax.ShapeDtypeStruct((L, H * MLA_V), BF16),
        scratch_shapes=[pltpu.VMEM((2, tq // tsub, tk, tsub), F32)],
        compiler_params=_cp(("parallel", "parallel")),
        name="mla_attn",
    )(qt, k, vt)


def _merge_kernel(h_ref, ya_ref, yb_ref, yc_ref, yd_ref, g0_ref, g1_ref, g2_ref, g3_ref, p_ref, o_ref):
    h = h_ref[...]
    acc = None
    for b, (y_ref, g_ref) in enumerate(((ya_ref, g0_ref), (yb_ref, g1_ref), (yc_ref, g2_ref), (yd_ref, g3_ref))):
        gate = _sigmoid(_dot(h, g_ref[...]))
        t = gate * _dot(y_ref[...], p_ref[b])
        acc = t if acc is None else acc + t
    o_ref[...] = acc.astype(o_ref.dtype)


def merge_branches(h, ys, w_gate, w_branch, tm=1024, tn=512):
    L = h.shape[0]
    nj = D_MODEL // tn
    gspec = lambda b: pl.BlockSpec((D_MODEL, tn), lambda i, j: (0, b * nj + j))
    yspec = pl.BlockSpec((tm, BW), lambda i, j: (i, 0))
    return pl.pallas_call(
        _merge_kernel,
        grid=(L // tm, nj),
        in_specs=[pl.BlockSpec((tm, D_MODEL), lambda i, j: (i, 0)), yspec, yspec, yspec, yspec,
                  gspec(0), gspec(1), gspec(2), gspec(3),
                  pl.BlockSpec((4, BW, tn), lambda i, j: (0, 0, j))],
        out_specs=pl.BlockSpec((tm, tn), lambda i, j: (i, j)),
        out_shape=jax.ShapeDtypeStruct((L, D_MODEL), BF16),
        compiler_params=_cp(("parallel", "arbitrary")),
        name="merge",
    )(h, *ys, w_gate, w_gate, w_gate, w_gate, w_branch)


def _out_proj_kernel(x_ref, m_ref, w_ref, g_ref, o_ref):
    f = _dot(m_ref[...], w_ref[...])
    f = f * lax.rsqrt(jnp.mean(f * f, axis=-1, keepdims=True) + NORM_EPS) * g_ref[...]
    o_ref[...] = x_ref[...] + f


def out_proj(x, m, w_out, gain, tm=512):
    L = x.shape[0]
    return pl.pallas_call(
        _out_proj_kernel,
        grid=(L // tm,),
        in_specs=[pl.BlockSpec((tm, D_MODEL), lambda i: (i, 0)),
                  pl.BlockSpec((tm, D_MODEL), lambda i: (i, 0)),
                  pl.BlockSpec((D_MODEL, D_MODEL), lambda i: (0, 0)),
                  pl.BlockSpec((1, D_MODEL), lambda i: (0, 0))],
        out_specs=pl.BlockSpec((tm, D_MODEL), lambda i: (i, 0)),
        out_shape=jax.ShapeDtypeStruct((L, D_MODEL), F32),
        compiler_params=_cp(("parallel",)),
        name="out_proj",
    )(x, m, w_out, gain.reshape(1, D_MODEL))


def _mlp_kernel(x_ref, g1_ref, w1_ref, w2_ref, g2_ref, o_ref, h_scr):
    j = pl.program_id(1)

    @pl.when(j == 0)
    def _():
        x = x_ref[...]
        ms = jnp.mean(x * x, axis=-1, keepdims=True)
        h_scr[...] = (x * lax.rsqrt(ms + NORM_EPS) * g1_ref[...]).astype(BF16)
        o_ref[...] = jnp.zeros_like(o_ref)

    a = _dot(h_scr[...], w1_ref[...])
    a = jnp.maximum(a, 0.0)
    a = (a * a).astype(BF16)
    o_ref[...] += _dot(a, w2_ref[...])

    @pl.when(j == pl.num_programs(1) - 1)
    def _():
        f = o_ref[...]
        f = f * lax.rsqrt(jnp.mean(f * f, axis=-1, keepdims=True) + NORM_EPS) * g2_ref[...]
        o_ref[...] = x_ref[...] + f


def mlp(x, g1, w1, w2, g2, tm=512, tf=1024):
    L = x.shape[0]
    return pl.pallas_call(
        _mlp_kernel,
        grid=(L // tm, D_FF // tf),
        in_specs=[pl.BlockSpec((tm, D_MODEL), lambda i, j: (i, 0)),
                  pl.BlockSpec((1, D_MODEL), lambda i, j: (0, 0)),
                  pl.BlockSpec((D_MODEL, tf), lambda i, j: (0, j)),
                  pl.BlockSpec((tf, D_MODEL), lambda i, j: (j, 0)),
                  pl.BlockSpec((1, D_MODEL), lambda i, j: (0, 0))],
        out_specs=pl.BlockSpec((tm, D_MODEL), lambda i, j: (i, 0)),
        out_shape=jax.ShapeDtypeStruct((L, D_MODEL), F32),
        scratch_shapes=[pltpu.VMEM((tm, D_MODEL), BF16)],
        compiler_params=_cp(("parallel", "arbitrary")),
        name="mlp",
    )(x, g1.reshape(1, D_MODEL), w1, w2, g2.reshape(1, D_MODEL))


def _mix_weight(w):
    half = MLA_ROPE // 2
    z = lambda n: jnp.zeros((D_MODEL, n), w.dtype)
    kr = w[:, _OFF_KR:_OFF_KR + MLA_ROPE]
    cols = [
        w[:, _OFF_QKV:_OFF_QKV + 3 * BW],
        w[:, _OFF_U:_OFF_U + BW],
        w[:, _OFF_Z:_OFF_Z + BW],
        w[:, _OFF_SWAQ:_OFF_SWAQ + BW],
        w[:, _OFF_CKV:_OFF_CKV + MLA_KV_RANK],
        w[:, _OFF_SWAKV:_OFF_SWAKV + 256],
        w[:, _OFF_CQ:_OFF_CQ + MLA_Q_RANK],
        kr, z(64),
        kr[:, half:], kr[:, :half], z(64),
        w[:, _OFF_BETA:_OFF_BETA + 16], z(112),
    ]
    return jnp.concatenate(cols, axis=1).astype(BF16)


def kernel(x, w_in, s5_lam_re, s5_lam_im, s5_log_step, s5_b_re, s5_b_im, s5_c_re, s5_c_im, s5_d, s5_w_glu, s5_b_glu, gdn_conv, gdn_a_log, gdn_dt_bias, gdn_o_gain, swa_sink, t5_bias, mla_q_gain, mla_kv_gain, mla_w_uq, mla_w_ukv, w_branch, w_out, mix_pre_gain, mix_post_gain, mlp_pre_gain, mlp_post_gain, w_mlp_in, w_mlp_out):
    B_, L, _ = x.shape
    outs = []
    for b in range(B_):
        xb = x[b].astype(F32)
        for l in range(DEPTH):
            w_mix = _mix_weight(w_in[l])
            w_gate = w_in[l][:, _OFF_GATE:].astype(BF16)
            proj, h = in_proj(xb, mix_pre_gain[l].astype(F32), w_mix, tm=1024)
            y_a = s5_mixer(proj, s5_lam_re[l], s5_lam_im[l], s5_log_step[l],
                           s5_b_re[l], s5_b_im[l], s5_c_re[l], s5_c_im[l], s5_d[l], s5_w_glu[l],
                           s5_b_glu[l], ucol=MIX_U // BW)
            y_b = gdn_mixer_fused(proj, proj, proj[:, MIX_SMALL:MIX_SMALL + 16], gdn_conv[l], gdn_a_log[l],
                                  gdn_dt_bias[l], gdn_o_gain[l], zcol=MIX_Z // BW)
            y_c = swa_mixer_t(proj, proj, swa_sink[l], t5_bias, qcol=MIX_SWAQ // BW, kvcol=MIX_SWAKV // 256)
            y_d = mla_mixer_t(proj, proj, proj, proj, mla_q_gain[l], mla_kv_gain[l], mla_w_uq[l], mla_w_ukv[l],
                              cols=(MIX_CQ // MLA_Q_RANK, MIX_CKV // MLA_KV_RANK, MIX_KR // 128,
                                    MIX_KRROT // 128))
            merged = merge_branches(h, (y_a, y_b, y_c, y_d), w_gate, w_branch[l].astype(BF16))
            xb = out_proj(xb, merged, w_out[l].astype(BF16), mix_post_gain[l].astype(F32))
            xb = mlp(xb, mlp_pre_gain[l].astype(F32), w_mlp_in[l].astype(BF16),
                     w_mlp_out[l].astype(BF16), mlp_post_gain[l].astype(F32))
        outs.append(xb)
    return jnp.stack(outs).astype(x.dtype)
```
